```python
import math
import jax, jax.numpy as jnp
from jax import lax
import numpy as np

D_MODEL = 4096
BATCH = 4
SEQ = 2048
DEPTH = 2
DEC_BATCH = 32
DEC_SEQ = 1
PAST_LEN = 16384
PAGE_SIZE = 128

N_A_LAYERS = DEPTH // 2
N_B_LAYERS = DEPTH - N_A_LAYERS
M_HEADS = 8
M_QK_DIM = D_MODEL // (2 * M_HEADS)
M_V_DIM = D_MODEL // M_HEADS
M_CHUNK = 64
GATE_CAP = 15.0
HEAD_DIM = 64
A_HEADS = D_MODEL // HEAD_DIM
KV_HEADS = 8
GROUP = A_HEADS // KV_HEADS
WINDOW = 128
ROPE_THETA = 10000.0
D_FF = ((8 * D_MODEL + 3 * 256 - 1) // (3 * 256)) * 256
CONV_W = 3
EPS = 1e-6

kernel_name = "yoco_mlstm_swa_sink_convffn_step"


def rmsnorm(x, g):
    xf = x.astype(jnp.float32)
    y = xf * lax.rsqrt(jnp.mean(xf * xf, axis=-1, keepdims=True) + EPS)
    return (y * g.astype(jnp.float32)).astype(x.dtype)


def modulate(x, g, shift, scale):
    return rmsnorm(x, g) * (1 + scale[:, None, :]) + shift[:, None, :]


def rope(x, pos):
    half = x.shape[-1] // 2
    freq = ROPE_THETA ** (-jnp.arange(half, dtype=jnp.float32) / half)
    ang = pos.astype(jnp.float32)[:, None] * freq[None, :]
    cos = jnp.cos(ang)[None, :, None, :]
    sin = jnp.sin(ang)[None, :, None, :]
    xf = x.astype(jnp.float32)
    x1, x2 = xf[..., :half], xf[..., half:]
    return jnp.concatenate([x1 * cos - x2 * sin, x2 * cos + x1 * sin], axis=-1).astype(x.dtype)


def softcap(z):
    return GATE_CAP * jnp.tanh(z / GATE_CAP)


def mlstm_recurrence(q, k, v, ig, lf, C0, n0, m0):
    B, T, H, dk = q.shape
    L = M_CHUNK if T % M_CHUNK == 0 else T
    nc = T // L

    def to_chunks(a):
        a = a.reshape((B, nc, L) + a.shape[2:])
        return jnp.swapaxes(jnp.moveaxis(a, 1, 0), 2, 3)

    causal = jnp.tril(jnp.ones((L, L), dtype=bool))

    def step(carry, inp):
        C, n, m = carry
        qc, kc, vc, ic, fc = inp
        b = jnp.cumsum(fc, axis=-1)
        d = b[..., :, None] - b[..., None, :] + ic[..., None, :]
        d = jnp.where(causal, d, -jnp.inf)
        inter = b + m[..., None]
        mt = jnp.maximum(inter, jnp.max(d, axis=-1))
        w = jnp.exp(d - mt[..., None])
        si = jnp.exp(inter - mt)
        qk = jnp.einsum('bhtd,bhsd->bhts', qc, kc) * w
        num = jnp.einsum('bhts,bhsv->bhtv', qk, vc) + si[..., None] * jnp.einsum('bhtd,bhdv->bhtv', qc, C)
        den = jnp.sum(qk, axis=-1) + si * jnp.einsum('bhtd,bhd->bht', qc, n)
        h = num / jnp.maximum(jnp.abs(den), jnp.exp(-mt))[..., None]
        gl = b[..., -1:] - b + ic
        m_new = jnp.maximum(b[..., -1] + m, jnp.max(gl, axis=-1))
        wl = jnp.exp(gl - m_new[..., None])
        sd = jnp.exp(b[..., -1] + m - m_new)
        C_new = sd[..., None, None] * C + jnp.einsum('bhs,bhsd,bhsv->bhdv', wl, kc, vc)
        n_new = sd[..., None] * n + jnp.einsum('bhs,bhsd->bhd', wl, kc)
        return (C_new, n_new, m_new), h

    (C, n, m), hs = lax.scan(step, (C0, n0, m0),
                             (to_chunks(q), to_chunks(k), to_chunks(v), to_chunks(ig), to_chunks(lf)))
    hs = jnp.swapaxes(jnp.moveaxis(hs, 0, 1), 2, 3).reshape(B, T, H, v.shape[-1])
    return hs, C, n, m


def mlstm_mixer(h, w_in, b_i, b_f, g_head, w_out, C0, n0, m0):
    B, T, _ = h.shape
    qd = M_HEADS * M_QK_DIM
    vd = M_HEADS * M_V_DIM
    z = h @ w_in
    q, k, v, o, gi, gf = jnp.split(z, [qd, 2 * qd, 2 * qd + vd, 2 * qd + 2 * vd, 2 * qd + 2 * vd + M_HEADS], axis=-1)
    q = q.reshape(B, T, M_HEADS, M_QK_DIM).astype(jnp.float32) * (M_QK_DIM ** -0.5)
    k = k.reshape(B, T, M_HEADS, M_QK_DIM).astype(jnp.float32)
    v = v.reshape(B, T, M_HEADS, M_V_DIM).astype(jnp.float32)
    ig = softcap(gi.astype(jnp.float32) + b_i.astype(jnp.float32))
    lf = jax.nn.log_sigmoid(softcap(gf.astype(jnp.float32) + b_f.astype(jnp.float32)))
    hs, C, n, m = mlstm_recurrence(q, k, v, ig, lf, C0.astype(jnp.float32), n0.astype(jnp.float32), m0.astype(jnp.float32))
    hs = hs * lax.rsqrt(jnp.mean(hs * hs, axis=-1, keepdims=True) + EPS)
    hs = hs.reshape(B, T, D_MODEL) * g_head.astype(jnp.float32) * jax.nn.sigmoid(o.astype(jnp.float32))
    return hs.astype(h.dtype) @ w_out, (C, n, m)


def shared_kv(x, cs, pos, w_ada_kv, g_kv, w_kv, b_kv):
    B, T, _ = x.shape
    sh, sc = jnp.split(cs @ w_ada_kv, 2, axis=-1)
    hn = modulate(x, g_kv, sh, sc)
    k, v = jnp.split(hn @ w_kv + b_kv, 2, axis=-1)
    k = rope(k.reshape(B, T, KV_HEADS, HEAD_DIM), pos)
    v = v.reshape(B, T, KV_HEADS, HEAD_DIM)
    return k, v


def sink_attention(q, k, v, valid, sinks):
    s = jnp.einsum('bnqhgd,bnshd->bnhgqs', q, k, preferred_element_type=jnp.float32) * (HEAD_DIM ** -0.5)
    s = jnp.where(valid[None, :, None, None], s, -jnp.inf)
    sk = sinks.astype(jnp.float32).reshape(KV_HEADS, GROUP)[None, None, :, :, None, None]
    mx = jnp.maximum(jnp.max(s, axis=-1, keepdims=True), sk)
    p = jnp.exp(s - mx)
    p = p / (jnp.sum(p, axis=-1, keepdims=True) + jnp.exp(sk - mx))
    return jnp.einsum('bnhgqs,bnshd->bnqhgd', p.astype(v.dtype), v)


def window_attention(q, k_new, v_new, kbuf, vbuf, sinks):
    B, T, H, hd = q.shape
    if kbuf is None:
        nb = T // WINDOW
        qb = q.reshape(B, nb, WINDOW, KV_HEADS, GROUP, hd)
        kc = k_new.reshape(B, nb, WINDOW, KV_HEADS, hd)
        vc = v_new.reshape(B, nb, WINDOW, KV_HEADS, hd)
        pad = ((0, 0), (1, 0), (0, 0), (0, 0), (0, 0))
        kb = jnp.concatenate([jnp.pad(kc, pad)[:, :-1], kc], axis=2)
        vb = jnp.concatenate([jnp.pad(vc, pad)[:, :-1], vc], axis=2)
        i = jnp.arange(WINDOW)[:, None]
        j = jnp.arange(2 * WINDOW)[None, :]
        diff = i + WINDOW - j
        blk = jnp.arange(nb)[:, None, None]
        valid = (diff >= 0) & (diff < WINDOW) & ((blk > 0) | (j >= WINDOW))
        o = sink_attention(qb, kb, vb, valid, sinks)
    else:
        wc = kbuf.shape[1]
        k_ctx = jnp.concatenate([kbuf.astype(k_new.dtype), k_new], axis=1)
        v_ctx = jnp.concatenate([vbuf.astype(v_new.dtype), v_new], axis=1)
        i = jnp.arange(T)[:, None]
        j = jnp.arange(wc + T)[None, :]
        diff = i + wc - j
        valid = ((diff >= 0) & (diff < WINDOW))[None]
        o = sink_attention(q.reshape(B, 1, T, KV_HEADS, GROUP, hd), k_ctx[:, None], v_ctx[:, None], valid, sinks)
    return o.reshape(B, T, H * hd)


def conv_ffn(h, buf, w_in, w_conv, b_conv, w_out):
    T = h.shape[1]
    u = h @ w_in
    ext = jnp.concatenate([buf.astype(u.dtype), u], axis=1)
    y = b_conv
    for j in range(CONV_W):
        y = y + ext[:, j:j + T] * w_conv[j]
    gate, up = jnp.split(y, 2, axis=-1)
    return (jax.nn.silu(gate) * up) @ w_out, ext[:, T:]


def setup_inputs(seed: int = 0) -> dict:
    key = jax.random.key(seed)
    ks = iter(jax.random.split(key, 40))

    def nrm(shape, s):
        return jax.random.normal(next(ks), shape, jnp.float32) * s

    D, F = D_MODEL, D_FF
    qd, vd = M_HEADS * M_QK_DIM, M_HEADS * M_V_DIM
    m_in = 2 * qd + 2 * vd + 2 * M_HEADS
    cache_w = min(WINDOW, PAST_LEN)
    kvd = KV_HEADS * HEAD_DIM
    return {
        "x_prompt": nrm((BATCH, SEQ, D), 1.0),
        "x_sample": nrm((DEC_BATCH, DEC_SEQ, D), 1.0),
        "state_mlstm_C": nrm((N_A_LAYERS, DEC_BATCH, M_HEADS, M_QK_DIM, M_V_DIM), 0.3),
        "state_mlstm_n": nrm((N_A_LAYERS, DEC_BATCH, M_HEADS, M_QK_DIM), 0.3),
        "state_mlstm_m": nrm((N_A_LAYERS, DEC_BATCH, M_HEADS), 1.0),
        "cache_conv": nrm((DEPTH, DEC_BATCH, CONV_W - 1, 2 * F), 1.0),
        "cache_k_win": nrm((DEC_BATCH, cache_w, KV_HEADS, HEAD_DIM), 1.0),
        "cache_v_win": nrm((DEC_BATCH, cache_w, KV_HEADS, HEAD_DIM), 1.0),
        "c_prompt": nrm((BATCH, D), 1.0),
        "c_sample": nrm((DEC_BATCH, D), 1.0),
        "w_ada": nrm((DEPTH, D, 6 * D), 0.3 * D ** -0.5),
        "g_norm1": 1.0 + nrm((DEPTH, D), 0.02),
        "g_norm2": 1.0 + nrm((DEPTH, D), 0.02),
        "w_m_in": nrm((N_A_LAYERS, D, m_in), D ** -0.5),
        "b_m_i": nrm((N_A_LAYERS, M_HEADS), 0.5),
        "b_m_f": 3.0 + nrm((N_A_LAYERS, M_HEADS), 0.5),
        "g_m_head": 1.0 + nrm((N_A_LAYERS, D), 0.02),
        "w_m_out": nrm((N_A_LAYERS, vd, D), vd ** -0.5),
        "w_ada_kv": nrm((D, 2 * D), 0.3 * D ** -0.5),
        "g_kv": 1.0 + nrm((D,), 0.02),
        "w_kv": nrm((D, 2 * kvd), D ** -0.5),
        "b_kv": nrm((2 * kvd,), 0.01),
        "w_q": nrm((N_B_LAYERS, D, A_HEADS * HEAD_DIM), D ** -0.5),
        "b_q": nrm((N_B_LAYERS, A_HEADS * HEAD_DIM), 0.01),
        "sinks": nrm((N_B_LAYERS, A_HEADS), 1.0),
        "w_o": nrm((N_B_LAYERS, A_HEADS * HEAD_DIM, D), (A_HEADS * HEAD_DIM) ** -0.5),
        "b_o": nrm((N_B_LAYERS, D), 0.01),
        "w_ffn_in": nrm((DEPTH, D, 2 * F), D ** -0.5),
        "w_conv": nrm((DEPTH, CONV_W, 2 * F), CONV_W ** -0.5),
        "b_conv": nrm((DEPTH, 2 * F), 0.01),
        "w_ffn_out": nrm((DEPTH, F, D), F ** -0.5),
        "g_final": 1.0 + nrm((D,), 0.02),
    }


def reference(x_prompt, x_sample, state_mlstm_C, state_mlstm_n, state_mlstm_m, cache_conv, cache_k_win, cache_v_win, c_prompt, c_sample, w_ada, g_norm1, g_norm2, w_m_in, b_m_i, b_m_f, g_m_head, w_m_out, w_ada_kv, g_kv, w_kv, b_kv, w_q, b_q, sinks, w_o, b_o, w_ffn_in, w_conv, b_conv, w_ffn_out, g_final):
    def run(x, c, pos, C0, n0, m0, conv0, kbuf, vbuf):
        B, T, _ = x.shape
        cs = jax.nn.silu(c.astype(jnp.float32)).astype(x.dtype)
        new_C, new_n, new_m, new_conv = [], [], [], []
        k_new = v_new = k_out = v_out = None
        for l in range(DEPTH):
            sh1, sc1, ga1, sh2, sc2, ga2 = jnp.split(cs @ w_ada[l], 6, axis=-1)
            if l < N_A_LAYERS:
                hn = modulate(x, g_norm1[l], sh1, sc1)
                out, (C, n, m) = mlstm_mixer(hn, w_m_in[l], b_m_i[l], b_m_f[l], g_m_head[l], w_m_out[l], C0[l], n0[l], m0[l])
                new_C.append(C)
                new_n.append(n)
                new_m.append(m)
            else:
                bl = l - N_A_LAYERS
                if bl == 0:
                    k_new, v_new = shared_kv(x, cs, pos, w_ada_kv, g_kv, w_kv, b_kv)
                    if kbuf is None:
                        wc = min(WINDOW, T)
                        k_out, v_out = k_new[:, -wc:], v_new[:, -wc:]
                    else:
                        wc = kbuf.shape[1]
                        k_out = jnp.concatenate([kbuf.astype(k_new.dtype), k_new], axis=1)[:, -wc:]
                        v_out = jnp.concatenate([vbuf.astype(v_new.dtype), v_new], axis=1)[:, -wc:]
                hn = modulate(x, g_norm1[l], sh1, sc1)
                q = rope((hn @ w_q[bl] + b_q[bl]).reshape(B, T, A_HEADS, HEAD_DIM), pos)
                out = window_attention(q, k_new, v_new, kbuf, vbuf, sinks[bl]) @ w_o[bl] + b_o[bl]
            x = x + ga1[:, None, :] * out
            hn = modulate(x, g_norm2[l], sh2, sc2)
            f, cb = conv_ffn(hn, conv0[l], w_ffn_in[l], w_conv[l], b_conv[l], w_ffn_out[l])
            new_conv.append(cb)
            x = x + ga2[:, None, :] * f
        y = rmsnorm(x, g_final)
        return y, jnp.stack(new_C), jnp.stack(new_n), jnp.stack(new_m), jnp.stack(new_conv), k_out, v_out

    Bp, Tp, _ = x_prompt.shape
    C0p = jnp.zeros((N_A_LAYERS, Bp, M_HEADS, M_QK_DIM, M_V_DIM), jnp.float32)
    n0p = jnp.zeros((N_A_LAYERS, Bp, M_HEADS, M_QK_DIM), jnp.float32)
    m0p = jnp.zeros((N_A_LAYERS, Bp, M_HEADS), jnp.float32)
    conv0p = jnp.zeros((DEPTH, Bp, CONV_W - 1, 2 * D_FF), x_prompt.dtype)
    y_prompt, C_p, n_p, m_p, conv_p, k_win_p, v_win_p = run(
        x_prompt, c_prompt, jnp.arange(Tp, dtype=jnp.int32), C0p, n0p, m0p, conv0p, None, None)
    pos_s = PAST_LEN + jnp.arange(x_sample.shape[1], dtype=jnp.int32)
    y_sample, C_s, n_s, m_s, conv_s, k_win_s, v_win_s = run(
        x_sample, c_sample, pos_s, state_mlstm_C, state_mlstm_n, state_mlstm_m, cache_conv, cache_k_win, cache_v_win)
    return (y_prompt, y_sample, C_p, n_p, m_p, conv_p, k_win_p, v_win_p, C_s, n_s, m_s, conv_s, k_win_s, v_win_s)
```

```python
import functools

import jax
import jax.numpy as jnp
from jax import lax
from jax.experimental import pallas as pl
from jax.experimental.pallas import tpu as pltpu

BF = jnp.bfloat16
F32 = jnp.float32

D_MODEL = 4096
M_HEADS = 8
M_QK = 256
M_V = 512
M_CHUNK = 64
GATE_CAP = 15.0
HEAD_DIM = 64
A_HEADS = 64
KV_HEADS = 8
GROUP = 8
WINDOW = 128
ROPE_THETA = 10000.0
D_FF = 11008
PAST_LEN = 16384
EPS = 1e-6

VMEM_LIMIT_V7X = 58 * 1024 * 1024
LANES = 128


def _params(sem):
    return pltpu.CompilerParams(dimension_semantics=sem, vmem_limit_bytes=VMEM_LIMIT_V7X)


def _sigmoid(x):
    return 1.0 / (1.0 + jnp.exp(-x))


def _linear_kernel(x_ref, w_ref, *rest, n_extra, epilogue):
    extra = rest[:n_extra]
    o_ref = rest[n_extra]
    wb_ref = rest[n_extra + 1]

    @pl.when(pl.program_id(1) == 0)
    def _():
        wb_ref[...] = w_ref[...].astype(BF)

    acc = jnp.dot(x_ref[...].astype(BF), wb_ref[...], preferred_element_type=F32)
    o_ref[...] = epilogue(acc, *[e[...] for e in extra]).astype(o_ref.dtype)


def linear(x, w3, lead, *, name, n_cols, tm, tn, epilogue, extras=(), out_dtype=F32,
           col_off=0, k_blk=None, k_idx=0):
    M = x.shape[0]
    K = k_blk if k_blk is not None else x.shape[1]
    assert M % tm == 0 and n_cols % tn == 0
    grid = (n_cols // tn, M // tm)
    in_specs = [
        pl.BlockSpec((tm, K), lambda j, i: (i, k_idx)),
        pl.BlockSpec((None, K, tn), lambda j, i: (lead, k_idx, j + col_off)),
    ] + [s for _, s in extras]
    return pl.pallas_call(
        functools.partial(_linear_kernel, n_extra=len(extras), epilogue=epilogue),
        grid=grid,
        in_specs=in_specs,
        out_specs=pl.BlockSpec((tm, tn), lambda j, i: (i, j)),
        out_shape=jax.ShapeDtypeStruct((M, n_cols), out_dtype),
        scratch_shapes=[pltpu.VMEM((K, tn), BF)],
        compiler_params=_params(("arbitrary", "arbitrary")),
        name=name,
    )(x, w3, *[a for a, _ in extras])


def _row_vec(v, tn):
    return v.reshape(1, -1), pl.BlockSpec((1, tn), lambda j, i: (0, j))


def _seq_vec(a, tn, T, tm):
    if T == 1:
        return a, pl.BlockSpec((tm, tn), lambda j, i: (i, j))
    tps = T // tm
    return a.reshape(a.shape[0], 1, a.shape[1]), pl.BlockSpec((None, 1, tn), lambda j, i: (i // tps, 0, j))


def _tile(a, tn, T, tm, col0=0):
    return a, pl.BlockSpec((tm, tn), lambda j, i: (i, j + col0))


def _ep_plain(acc):
    return acc


def _ep_bias(acc, b):
    return acc + b


def _ep_resid(acc, r, g):
    return r + g * acc


def _ep_resid_bias(acc, b, r, g):
    return r + g * (acc + b)


def _ep_resid_part(acc, p, r, g):
    return r + g * (acc + p)


def _ep_rope(acc, b, cos, sin, *, scale):
    y = acc + b
    n = y.shape[1]
    lane = lax.broadcasted_iota(jnp.int32, y.shape, 1)
    first_half = (lane % HEAD_DIM) < (HEAD_DIM // 2)
    partner = jnp.where(first_half, pltpu.roll(y, n - HEAD_DIM // 2, axis=1), pltpu.roll(y, HEAD_DIM // 2, axis=1))
    return (y * cos + partner * sin) * scale


def _ep_gates(acc, b):
    z = GATE_CAP * jnp.tanh((acc + b) / GATE_CAP)
    logsig = jnp.minimum(z, 0.0) - jnp.log(1.0 + jnp.exp(-jnp.abs(z)))
    lane = lax.broadcasted_iota(jnp.int32, z.shape, 1)
    return jnp.where(lane < M_HEADS, z, logsig)


def _silu_kernel(c_ref, o_ref):
    c = c_ref[...]
    o_ref[...] = (c * _sigmoid(c)).astype(o_ref.dtype)


def silu_cast(c):
    return pl.pallas_call(
        _silu_kernel,
        out_shape=jax.ShapeDtypeStruct(c.shape, BF),
        name="silu_c",
    )(c)


def _modulate_kernel(x_ref, g_ref, sh_ref, sc_ref, o_ref):
    x = x_ref[...]
    y = x * lax.rsqrt(jnp.mean(x * x, axis=-1, keepdims=True) + EPS) * g_ref[...]
    o_ref[...] = (y * (1.0 + sc_ref[...]) + sh_ref[...]).astype(o_ref.dtype)


def modulate(x, g, sh, sc, *, T, tm):
    M, D = x.shape
    sh_a, sh_s = _seq_vec(sh, D, T, tm)
    sc_a, sc_s = _seq_vec(sc, D, T, tm)
    fix = lambda s: pl.BlockSpec(s.block_shape, functools.partial(lambda im, i: im(0, i), s.index_map))
    return pl.pallas_call(
        _modulate_kernel,
        grid=(M // tm,),
        in_specs=[pl.BlockSpec((tm, D), lambda i: (i, 0)),
                  pl.BlockSpec((1, D), lambda i: (0, 0)),
                  fix(sh_s), fix(sc_s)],
        out_specs=pl.BlockSpec((tm, D), lambda i: (i, 0)),
        out_shape=jax.ShapeDtypeStruct((M, D), BF),
        compiler_params=_params(("arbitrary",)),
        name="modulate",
    )(x, g.reshape(1, D), sh_a, sc_a)


def _rmsnorm_kernel(x_ref, g_ref, o_ref):
    x = x_ref[...]
    o_ref[...] = x * lax.rsqrt(jnp.mean(x * x, axis=-1, keepdims=True) + EPS) * g_ref[...]


def rmsnorm(x, g, *, tm):
    M, D = x.shape
    return pl.pallas_call(
        _rmsnorm_kernel,
        grid=(M // tm,),
        in_specs=[pl.BlockSpec((tm, D), lambda i: (i, 0)), pl.BlockSpec((1, D), lambda i: (0, 0))],
        out_specs=pl.BlockSpec((tm, D), lambda i: (i, 0)),
        out_shape=jax.ShapeDtypeStruct((M, D), F32),
        compiler_params=_params(("arbitrary",)),
        name="final_norm",
    )(x, g.reshape(1, D))


def _conv_gate(ug, uu, pg1, pg2, pu1, pu2, wcg, wcu, bcg, bcu):
    yg = bcg + pg2 * wcg[0:1] + pg1 * wcg[1:2] + ug * wcg[2:3]
    yu = bcu + pu2 * wcu[0:1] + pu1 * wcu[1:2] + uu * wcu[2:3]
    return yg * _sigmoid(yg) * yu


def _shift_rows(u, carry):
    row = lax.broadcasted_iota(jnp.int32, u.shape, 0)
    p1 = jnp.where(row == 0, carry[1:2], pltpu.roll(u, 1, axis=0))
    p2 = jnp.where(row == 0, carry[0:1], jnp.where(row == 1, carry[1:2], pltpu.roll(u, 2, axis=0)))
    return p1, p2


def _ffn_in_kernel(x_ref, wg_ref, wu_ref, wcg_ref, wcu_ref, bcg_ref, bcu_ref,
                   act_ref, cg_ref, cu_ref, wgb_ref, wub_ref, carg_ref, caru_ref, *, tps):
    i = pl.program_id(1)

    @pl.when(i == 0)
    def _():
        wgb_ref[...] = wg_ref[...].astype(BF)
        wub_ref[...] = wu_ref[...].astype(BF)

    @pl.when(i % tps == 0)
    def _():
        carg_ref[...] = jnp.zeros_like(carg_ref)
        caru_ref[...] = jnp.zeros_like(caru_ref)

    x = x_ref[...]
    ug = jnp.dot(x, wgb_ref[...], preferred_element_type=F32)
    uu = jnp.dot(x, wub_ref[...], preferred_element_type=F32)
    pg1, pg2 = _shift_rows(ug, carg_ref[...])
    pu1, pu2 = _shift_rows(uu, caru_ref[...])
    act_ref[...] = _conv_gate(ug, uu, pg1, pg2, pu1, pu2, wcg_ref[...], wcu_ref[...],
                              bcg_ref[...], bcu_ref[...]).astype(act_ref.dtype)
    tm = ug.shape[0]
    carg_ref[...] = ug[tm - 2:tm]
    caru_ref[...] = uu[tm - 2:tm]
    cg_ref[...] = ug[tm - 2:tm]
    cu_ref[...] = uu[tm - 2:tm]


def ffn_in_prompt(x, w3, lead, w_conv, b_conv, *, B, T, tm, tn):
    M, K = x.shape
    F = D_FF
    nj = F // tn
    tps = T // tm
    wc = w_conv
    bc = b_conv.reshape(1, 2 * F)
    act, cg, cu = pl.pallas_call(
        functools.partial(_ffn_in_kernel, tps=tps),
        grid=(nj, M // tm),
        in_specs=[
            pl.BlockSpec((tm, K), lambda j, i: (i, 0)),
            pl.BlockSpec((None, K, tn), lambda j, i: (lead, 0, j)),
            pl.BlockSpec((None, K, tn), lambda j, i: (lead, 0, j + nj)),
            pl.BlockSpec((3, tn), lambda j, i: (0, j)),
            pl.BlockSpec((3, tn), lambda j, i: (0, j + nj)),
            pl.BlockSpec((1, tn), lambda j, i: (0, j)),
            pl.BlockSpec((1, tn), lambda j, i: (0, j + nj)),
        ],
        out_specs=[
            pl.BlockSpec((tm, tn), lambda j, i: (i, j)),
            pl.BlockSpec((None, 2, tn), lambda j, i: (i // tps, 0, j)),
            pl.BlockSpec((None, 2, tn), lambda j, i: (i // tps, 0, j)),
        ],
        out_shape=[
            jax.ShapeDtypeStruct((M, F), BF),
            jax.ShapeDtypeStruct((B, 2, F), F32),
            jax.ShapeDtypeStruct((B, 2, F), F32),
        ],
        scratch_shapes=[pltpu.VMEM((K, tn), BF), pltpu.VMEM((K, tn), BF),
                        pltpu.VMEM((2, tn), F32), pltpu.VMEM((2, tn), F32)],
        compiler_params=_params(("arbitrary", "arbitrary")),
        name="ffn_in_p",
    )(x, w3, w3, wc, wc, bc, bc)
    return act, jnp.concatenate([cg, cu], axis=-1)


def _ffn_in_step_kernel(x_ref, wg_ref, wu_ref, wcg_ref, wcu_ref, bcg_ref, bcu_ref,
                        c0g_ref, c0u_ref, c1g_ref, c1u_ref, act_ref, ug_ref, uu_ref):
    x = x_ref[...]
    ug = jnp.dot(x, wg_ref[...].astype(BF), preferred_element_type=F32)
    uu = jnp.dot(x, wu_ref[...].astype(BF), preferred_element_type=F32)
    act_ref[...] = _conv_gate(ug, uu, c1g_ref[...], c0g_ref[...], c1u_ref[...], c0u_ref[...],
                              wcg_ref[...], wcu_ref[...], bcg_ref[...], bcu_ref[...]).astype(act_ref.dtype)
    ug_ref[...] = ug
    uu_ref[...] = uu


def ffn_in_step(x, w3, lead, w_conv, b_conv, cache, *, tn):
    Bs, K = x.shape
    F = D_FF
    nj = F // tn
    bc = b_conv.reshape(1, 2 * F)
    cflat = cache.reshape(Bs, 4 * F)
    vec = lambda off: pl.BlockSpec((Bs, tn), lambda j: (0, j + off * nj))
    act, ug, uu = pl.pallas_call(
        _ffn_in_step_kernel,
        grid=(nj,),
        in_specs=[
            pl.BlockSpec((Bs, K), lambda j: (0, 0)),
            pl.BlockSpec((None, K, tn), lambda j: (lead, 0, j)),
            pl.BlockSpec((None, K, tn), lambda j: (lead, 0, j + nj)),
            pl.BlockSpec((3, tn), lambda j: (0, j)),
            pl.BlockSpec((3, tn), lambda j: (0, j + nj)),
            pl.BlockSpec((1, tn), lambda j: (0, j)),
            pl.BlockSpec((1, tn), lambda j: (0, j + nj)),
            vec(0), vec(1), vec(2), vec(3),
        ],
        out_specs=[pl.BlockSpec((Bs, tn), lambda j: (0, j))] * 3,
        out_shape=[jax.ShapeDtypeStruct((Bs, F), BF),
                   jax.ShapeDtypeStruct((Bs, F), F32),
                   jax.ShapeDtypeStruct((Bs, F), F32)],
        compiler_params=_params(("arbitrary",)),
        name="ffn_in_s",
    )(x, w3, w3, w_conv, w_conv, bc, bc, cflat, cflat, cflat, cflat)
    new_cache = jnp.stack([cache[:, 1, :], jnp.concatenate([ug, uu], axis=-1)], axis=1)
    return act, new_cache


def _mlstm_chunk_kernel(q_ref, k_ref, v_ref, o_ref, g_ref, gt_ref, gh_ref,
                        hs_ref, C_ref, n_ref, m_ref):
    h = pl.program_id(1)
    c = pl.program_id(2)
    L = M_CHUNK

    @pl.when(c == 0)
    def _():
        C_ref[...] = jnp.zeros_like(C_ref)
        n_ref[...] = jnp.zeros_like(n_ref)
        m_ref[...] = jnp.zeros_like(m_ref)

    g = g_ref[...]
    gt = gt_ref[...]
    lane = lax.broadcasted_iota(jnp.int32, g.shape, 1)
    sub = lax.broadcasted_iota(jnp.int32, gt.shape, 0)
    ic_col = jnp.sum(jnp.where(lane == h, g, 0.0), axis=1, keepdims=True)
    fc_col = jnp.sum(jnp.where(lane == h + M_HEADS, g, 0.0), axis=1, keepdims=True)
    ic_row = jnp.sum(jnp.where(sub == h, gt, 0.0), axis=0, keepdims=True)
    fc_row = jnp.sum(jnp.where(sub == h + M_HEADS, gt, 0.0), axis=0, keepdims=True)

    t_idx = lax.broadcasted_iota(jnp.int32, (L, L), 0)
    s_idx = lax.broadcasted_iota(jnp.int32, (L, L), 1)
    causal = s_idx <= t_idx
    b_col = jnp.sum(jnp.where(causal, fc_row, 0.0), axis=1, keepdims=True)
    b_row = jnp.sum(jnp.where(t_idx <= s_idx, fc_col, 0.0), axis=0, keepdims=True)

    m_old = m_ref[0:1, 0:1]
    d = jnp.where(causal, b_col - b_row + ic_row, -jnp.inf)
    inter = b_col + m_old
    mt = jnp.maximum(inter, jnp.max(d, axis=1, keepdims=True))
    w = jnp.exp(d - mt)
    si = jnp.exp(inter - mt)

    kf = k_ref[...]
    qb = (q_ref[...] * (M_QK ** -0.5)).astype(BF)
    kb = kf.astype(BF)
    vb = v_ref[...].astype(BF)
    C_old = C_ref[...]
    n_old = n_ref[...]
    qk = lax.dot_general(qb, kb, (((1,), (1,)), ((), ())), preferred_element_type=F32) * w
    num = (jnp.dot(qk.astype(BF), vb, preferred_element_type=F32)
           + si * jnp.dot(qb, C_old.astype(BF), preferred_element_type=F32))
    den = jnp.sum(qk, axis=1, keepdims=True) + si * jnp.sum(qb.astype(F32) * n_old, axis=1, keepdims=True)
    hh = num / jnp.maximum(jnp.abs(den), jnp.exp(-mt))

    b_last = b_col[L - 1:L, :]
    gl = b_last - b_col + ic_col
    m_new = jnp.maximum(b_last + m_old, jnp.max(gl, axis=0, keepdims=True))
    wl = jnp.exp(gl - m_new)
    sd = jnp.exp(b_last + m_old - m_new)
    kw = kf * wl
    C_ref[...] = sd * C_old + jnp.dot(kw.T.astype(BF), vb, preferred_element_type=F32)
    n_ref[...] = sd * n_old + jnp.sum(kw, axis=0, keepdims=True)
    m_ref[...] = jnp.broadcast_to(m_new, m_ref.shape)

    hn = hh * lax.rsqrt(jnp.mean(hh * hh, axis=1, keepdims=True) + EPS)
    hs_ref[...] = (hn * gh_ref[...] * _sigmoid(o_ref[...])).astype(hs_ref.dtype)


def mlstm_prompt(z, gates, g_head, *, B, T):
    M = B * T
    L = M_CHUNK
    nc = T // L
    gt = jnp.swapaxes(gates[:, :2 * M_HEADS].reshape(B * nc, L, 2 * M_HEADS), 1, 2)
    row = lambda b, h, c: b * nc + c
    kq = (M_HEADS * M_QK) // M_QK
    kv = (2 * M_HEADS * M_QK) // M_V
    ko = kv + M_HEADS
    hs, C, n, m = pl.pallas_call(
        _mlstm_chunk_kernel,
        grid=(B, M_HEADS, nc),
        in_specs=[
            pl.BlockSpec((L, M_QK), lambda b, h, c: (row(b, h, c), h)),
            pl.BlockSpec((L, M_QK), lambda b, h, c: (row(b, h, c), kq + h)),
            pl.BlockSpec((L, M_V), lambda b, h, c: (row(b, h, c), kv + h)),
            pl.BlockSpec((L, M_V), lambda b, h, c: (row(b, h, c), ko + h)),
            pl.BlockSpec((L, LANES), lambda b, h, c: (row(b, h, c), 0)),
            pl.BlockSpec((None, 2 * M_HEADS, L), lambda b, h, c: (row(b, h, c), 0, 0)),
            pl.BlockSpec((1, M_V), lambda b, h, c: (0, h)),
        ],
        out_specs=[
            pl.BlockSpec((L, M_V), lambda b, h, c: (row(b, h, c), h)),
            pl.BlockSpec((None, None, M_QK, M_V), lambda b, h, c: (b, h, 0, 0)),
            pl.BlockSpec((None, None, 1, M_QK), lambda b, h, c: (b, h, 0, 0)),
            pl.BlockSpec((None, None, 1, LANES), lambda b, h, c: (b, h, 0, 0)),
        ],
        out_shape=[
            jax.ShapeDtypeStruct((M, D_MODEL), BF),
            jax.ShapeDtypeStruct((B, M_HEADS, M_QK, M_V), F32),
            jax.ShapeDtypeStruct((B, M_HEADS, 1, M_QK), F32),
            jax.ShapeDtypeStruct((B, M_HEADS, 1, LANES), F32),
        ],
        compiler_params=_params(("arbitrary", "arbitrary", "arbitrary")),
        name="mlstm_p",
    )(z, z, z, z, gates, gt, g_head.reshape(1, D_MODEL))
    return hs, C, n[:, :, 0, :], m[:, :, 0, 0]


def _mlstm_step_kernel(q_ref, k_ref, v_ref, o_ref, ig_ref, lf_ref, m0_ref, n0_ref, C0_ref, gh_ref,
                       hs_ref, C_ref, n_ref, m_ref):
    ic = ig_ref[...]
    fc = lf_ref[...]
    m_old = m0_ref[...]
    inter = fc + m_old
    mt = jnp.maximum(inter, ic)
    w = jnp.exp(ic - mt)
    si = jnp.exp(inter - mt)
    q = (q_ref[...] * (M_QK ** -0.5)).astype(BF).astype(F32)
    k = k_ref[...]
    v = v_ref[...]
    n_old = n0_ref[...]
    qk = jnp.sum(q * k.astype(BF).astype(F32), axis=1, keepdims=True) * w
    wl = jnp.exp(ic - mt)
    sd = jnp.exp(inter - mt)
    kw = k * wl
    qT = q.T
    kwT = kw.T
    rows = []
    for h in range(M_HEADS):
        C_old = C0_ref[h]
        rows.append(jnp.sum(qT[:, h:h + 1] * C_old, axis=0, keepdims=True))
        C_ref[h] = sd[h:h + 1, :] * C_old + kwT[:, h:h + 1] * v[h:h + 1, :]
    qC = jnp.concatenate(rows, axis=0)
    num = qk.astype(BF).astype(F32) * v.astype(BF).astype(F32) + si * qC
    den = qk + si * jnp.sum(q * n_old, axis=1, keepdims=True)
    hh = num / jnp.maximum(jnp.abs(den), jnp.exp(-mt))
    n_ref[...] = sd * n_old + kw
    m_ref[...] = mt
    hn = hh * lax.rsqrt(jnp.mean(hh * hh, axis=1, keepdims=True) + EPS)
    hs_ref[...] = (hn * gh_ref[...] * _sigmoid(o_ref[...])).astype(hs_ref.dtype)


def mlstm_step(z, gates, g_head, C0, n0, m0):
    Bs = z.shape[0]
    H = M_HEADS
    qd, vd = H * M_QK, H * M_V
    q = z[:, :qd].reshape(Bs, H, M_QK)
    k = z[:, qd:2 * qd].reshape(Bs, H, M_QK)
    v = z[:, 2 * qd:2 * qd + vd].reshape(Bs, H, M_V)
    o = z[:, 2 * qd + vd:].reshape(Bs, H, M_V)
    ig = gates[:, :H].reshape(Bs, H, 1)
    lf = gates[:, H:2 * H].reshape(Bs, H, 1)
    per_b = lambda *tail: pl.BlockSpec((None,) + tail, lambda b: (b,) + (0,) * len(tail))
    hs, C, n, m = pl.pallas_call(
        _mlstm_step_kernel,
        grid=(Bs,),
        in_specs=[
            per_b(H, M_QK), per_b(H, M_QK), per_b(H, M_V), per_b(H, M_V),
            per_b(H, 1), per_b(H, 1), per_b(H, 1),
            pl.BlockSpec((None, None, H, M_QK), lambda b: (0, b, 0, 0)),
            pl.BlockSpec((None, None, H, M_QK, M_V), lambda b: (0, b, 0, 0, 0)),
            pl.BlockSpec((H, M_V), lambda b: (0, 0)),
        ],
        out_specs=[
            per_b(H, M_V),
            pl.BlockSpec((None, None, H, M_QK, M_V), lambda b: (0, b, 0, 0, 0)),
            pl.BlockSpec((None, None, H, M_QK), lambda b: (0, b, 0, 0)),
            per_b(H, 1),
        ],
        out_shape=[
            jax.ShapeDtypeStruct((Bs, H, M_V), BF),
            jax.ShapeDtypeStruct(C0.shape, F32),
            jax.ShapeDtypeStruct(n0.shape, F32),
            jax.ShapeDtypeStruct((Bs, H, 1), F32),
        ],
        compiler_params=_params(("arbitrary",)),
        name="mlstm_s",
    )(q, k, v, o, ig, lf, m0.reshape(Bs, H, 1), n0, C0, g_head.reshape(H, M_V))
    return hs.reshape(Bs, D_MODEL), C, n, m.reshape(1, Bs, H)


def _attn_kernel(sink_ref, q_ref, kc_ref, kp_ref, vc_ref, vp_ref, o_ref):
    n = pl.program_id(1)
    W = WINDOW
    R = GROUP * W
    kcat = jnp.concatenate([kp_ref[...], kc_ref[...]], axis=0).astype(BF)
    vcat = jnp.concatenate([vp_ref[...], vc_ref[...]], axis=0).astype(BF)
    i_idx = lax.broadcasted_iota(jnp.int32, (R, 2 * W), 0) % W
    j_idx = lax.broadcasted_iota(jnp.int32, (R, 2 * W), 1)
    has_prev = jnp.where(n > 0, 1, 0)
    lo = has_prev * (i_idx + 1) + (1 - has_prev) * W
    valid = (j_idx >= lo) & (j_idx <= i_idx + W)
    rowg = lax.broadcasted_iota(jnp.int32, (R, 1), 0) // W
    for h in range(KV_HEADS):
        qh = jnp.concatenate(
            [q_ref[:, (h * GROUP + g) * HEAD_DIM:(h * GROUP + g + 1) * HEAD_DIM] for g in range(GROUP)], axis=0)
        kh = kcat[:, h * HEAD_DIM:(h + 1) * HEAD_DIM]
        vh = vcat[:, h * HEAD_DIM:(h + 1) * HEAD_DIM]
        s = lax.dot_general(qh, kh, (((1,), (1,)), ((), ())), preferred_element_type=F32)
        s = jnp.where(valid, s, -jnp.inf)
        sk = jnp.zeros((R, 1), F32)
        for g in range(GROUP):
            sk = jnp.where(rowg == g, sink_ref[h * GROUP + g], sk)
        mx = jnp.maximum(jnp.max(s, axis=1, keepdims=True), sk)
        p = jnp.exp(s - mx)
        p = p / (jnp.sum(p, axis=1, keepdims=True) + jnp.exp(sk - mx))
        oh = jnp.dot(p.astype(BF), vh, preferred_element_type=F32)
        for g in range(GROUP):
            c0 = (h * GROUP + g) * HEAD_DIM
            o_ref[:, c0:c0 + HEAD_DIM] = oh[g * W:(g + 1) * W, :].astype(o_ref.dtype)


def attention_prompt(q, k, v, sinks, *, B, T):
    M = B * T
    W = WINDOW
    nb = T // W
    kvd = KV_HEADS * HEAD_DIM
    cur = lambda b, n: (b * nb + n, 0)
    prev = lambda b, n: (b * nb + jnp.maximum(n - 1, 0), 0)
    return pl.pallas_call(
        _attn_kernel,
        grid=(B, nb),
        in_specs=[
            pl.BlockSpec(memory_space=pltpu.SMEM),
            pl.BlockSpec((W, D_MODEL), cur),
            pl.BlockSpec((W, kvd), cur), pl.BlockSpec((W, kvd), prev),
            pl.BlockSpec((W, kvd), cur), pl.BlockSpec((W, kvd), prev),
        ],
        out_specs=pl.BlockSpec((W, D_MODEL), cur),
        out_shape=jax.ShapeDtypeStruct((M, D_MODEL), BF),
        compiler_params=_params(("arbitrary", "arbitrary")),
        name="attn_p",
    )(sinks, q, k, k, v, v)


def _attn_step_kernel(q_ref, kc_ref, vc_ref, kn_ref, vn_ref, sk_ref, o_ref, ko_ref, vo_ref):
    W = WINDOW
    kc = kc_ref[...]
    vc = vc_ref[...]
    kn = kn_ref[...]
    vn = vn_ref[...]
    kcb = kc.astype(BF)
    vcb = vc.astype(BF)
    jl = lax.broadcasted_iota(jnp.int32, (GROUP, W), 1)
    for h in range(KV_HEADS):
        sl = slice(h * HEAD_DIM, (h + 1) * HEAD_DIM)
        qh = q_ref[h * GROUP:(h + 1) * GROUP, :].astype(BF)
        s = lax.dot_general(qh, kcb[:, sl], (((1,), (1,)), ((), ())), preferred_element_type=F32)
        s = jnp.where(jl >= 1, s, -jnp.inf)
        sn = jnp.sum(qh.astype(F32) * kn[:, sl].astype(BF).astype(F32), axis=1, keepdims=True)
        sk = sk_ref[h * GROUP:(h + 1) * GROUP, :]
        mx = jnp.maximum(jnp.maximum(jnp.max(s, axis=1, keepdims=True), sn), sk)
        p = jnp.exp(s - mx)
        pn = jnp.exp(sn - mx)
        den = jnp.sum(p, axis=1, keepdims=True) + pn + jnp.exp(sk - mx)
        oh = (jnp.dot((p / den).astype(BF), vcb[:, sl], preferred_element_type=F32)
              + (pn / den).astype(BF).astype(F32) * vn[:, sl].astype(BF).astype(F32))
        o_ref[h * GROUP:(h + 1) * GROUP, :] = oh
    ko_ref[0:W - 1, :] = kc[1:W, :]
    ko_ref[W - 1:W, :] = kn
    vo_ref[0:W - 1, :] = vc[1:W, :]
    vo_ref[W - 1:W, :] = vn


def attention_step(q, k_new, v_new, kbuf, vbuf, sinks):
    Bs = q.shape[0]
    W = WINDOW
    kvd = KV_HEADS * HEAD_DIM
    per_b = lambda *tail: pl.BlockSpec((None,) + tail, lambda b: (b,) + (0,) * len(tail))
    o, ko, vo = pl.pallas_call(
        _attn_step_kernel,
        grid=(Bs,),
        in_specs=[per_b(A_HEADS, HEAD_DIM), per_b(W, kvd), per_b(W, kvd), per_b(1, kvd), per_b(1, kvd),
                  pl.BlockSpec((A_HEADS, 1), lambda b: (0, 0))],
        out_specs=[per_b(A_HEADS, HEAD_DIM), per_b(W, kvd), per_b(W, kvd)],
        out_shape=[jax.ShapeDtypeStruct((Bs, A_HEADS, HEAD_DIM), F32),
                   jax.ShapeDtypeStruct((Bs, W, kvd), F32),
                   jax.ShapeDtypeStruct((Bs, W, kvd), F32)],
        compiler_params=_params(("arbitrary",)),
        name="attn_s",
    )(q.reshape(Bs, A_HEADS, HEAD_DIM), kbuf.reshape(Bs, W, kvd), vbuf.reshape(Bs, W, kvd),
      k_new.reshape(Bs, 1, kvd), v_new.reshape(Bs, 1, kvd), sinks.reshape(A_HEADS, 1))
    return (o.reshape(Bs, D_MODEL), ko.reshape(Bs, W, KV_HEADS, HEAD_DIM), vo.reshape(Bs, W, KV_HEADS, HEAD_DIM))


def _rope_tables(pos, width):
    half = HEAD_DIM // 2
    freq = ROPE_THETA ** (-jnp.arange(half, dtype=F32) / half)
    ang = pos.astype(F32)[:, None] * freq[None, :]
    cos = jnp.cos(ang)
    sin = jnp.sin(ang)
    reps = width // HEAD_DIM
    return (jnp.tile(jnp.concatenate([cos, cos], axis=1), (1, reps)),
            jnp.tile(jnp.concatenate([-sin, sin], axis=1), (1, reps)))


def _run(x, ada, pos, state, P, *, B, T, tm, tm_ffo, tm_mod):
    D = D_MODEL
    M = B * T
    tn = 512
    tps = T // tm if T > 1 else 1
    seq = lambda a, tm_=tm, tn_=tn: _seq_vec(a, tn_, T, tm_)
    cos, sin = _rope_tables(pos, tn)
    if T > 1:
        rope_spec = pl.BlockSpec((tm, tn), lambda j, i: (i % tps, 0))
    else:
        rope_spec = pl.BlockSpec((1, tn), lambda j, i: (0, 0))
    out = {}

    sh1, sc1, ga1, sh2, sc2, ga2 = ada[0]
    hn = modulate(x, P["g_norm1"][0], sh1, sc1, T=T, tm=tm_mod)
    qd, vd = M_HEADS * M_QK, M_HEADS * M_V
    tag = "_p" if T > 1 else "_s"
    z = linear(hn, P["w_m_in"], 0, name="m_in" + tag, n_cols=2 * qd + 2 * vd, tm=tm, tn=tn, epilogue=_ep_plain)
    wg = jnp.pad(P["w_m_in"][:, :, 2 * qd + 2 * vd:], ((0, 0), (0, 0), (0, LANES - 2 * M_HEADS)))
    bg = jnp.pad(jnp.concatenate([P["b_m_i"][0], P["b_m_f"][0]]), (0, LANES - 2 * M_HEADS))
    gates = linear(hn, wg, 0, name="m_gates" + tag, n_cols=LANES, tm=tm, tn=LANES, epilogue=_ep_gates,
                   extras=[_row_vec(bg, LANES)])
    if T > 1:
        hs, C, n, m = mlstm_prompt(z, gates, P["g_m_head"][0], B=B, T=T)
        out["C"], out["n"], out["m"] = C[None], n[None], m[None]
    else:
        hs, out["C"], out["n"], out["m"] = mlstm_step(z, gates, P["g_m_head"][0], *state["mlstm"])
    x = linear(hs, P["w_m_out"], 0, name="m_out" + tag, n_cols=D, tm=tm, tn=tn, epilogue=_ep_resid,
               extras=[_tile(x, tn, T, tm), seq(ga1)])

    convs = []
    for l in range(2):
        if l == 1:
            sh1, sc1, ga1, sh2, sc2, ga2 = ada[1]
            shk, sck = ada[2]
            hk = modulate(x, P["g_kv"], shk, sck, T=T, tm=tm_mod)
            kvd = KV_HEADS * HEAD_DIM
            k = linear(hk, P["w_kv"], 0, name="kv_k" + tag, n_cols=kvd, tm=tm, tn=tn,
                       epilogue=functools.partial(_ep_rope, scale=1.0),
                       extras=[_row_vec(P["b_kv"][:kvd], tn), (cos, rope_spec), (sin, rope_spec)])
            v = linear(hk, P["w_kv"], 0, name="kv_v" + tag, n_cols=kvd, tm=tm, tn=tn, col_off=kvd // tn, epilogue=_ep_bias,
                       extras=[_row_vec(P["b_kv"][kvd:], tn)])
            hn = modulate(x, P["g_norm1"][1], sh1, sc1, T=T, tm=tm_mod)
            q = linear(hn, P["w_q"], 0, name="attn_q" + tag, n_cols=D, tm=tm, tn=tn,
                       epilogue=functools.partial(_ep_rope, scale=HEAD_DIM ** -0.5),
                       extras=[_row_vec(P["b_q"][0], tn), (cos, rope_spec), (sin, rope_spec)],
                       out_dtype=BF if T > 1 else F32)
            if T > 1:
                o = attention_prompt(q, k, v, P["sinks"][0], B=B, T=T)
                out["k_win"] = k.reshape(B, T, KV_HEADS, HEAD_DIM)[:, T - WINDOW:]
                out["v_win"] = v.reshape(B, T, KV_HEADS, HEAD_DIM)[:, T - WINDOW:]
            else:
                o, out["k_win"], out["v_win"] = attention_step(q, k, v, state["kbuf"], state["vbuf"], P["sinks"][0])
            x = linear(o, P["w_o"], 0, name="attn_o" + tag, n_cols=D, tm=tm, tn=tn, epilogue=_ep_resid_bias,
                       extras=[_row_vec(P["b_o"][0], tn), _tile(x, tn, T, tm), seq(ga1)])

        hn = modulate(x, P["g_norm2"][l], sh2, sc2, T=T, tm=tm_mod)
        if T > 1:
            act, cb = ffn_in_prompt(hn, P["w_ffn_in"], l, P["w_conv"][l], P["b_conv"][l], B=B, T=T, tm=tm, tn=256)
        else:
            act, cb = ffn_in_step(hn, P["w_ffn_in"], l, P["w_conv"][l], P["b_conv"][l], state["conv"][l], tn=256)
        convs.append(cb)
        kh = D_FF // 2
        part = linear(act, P["w_ffn_out"], l, name="ffn_out_a" + tag, n_cols=D, tm=tm_ffo, tn=tn, epilogue=_ep_plain,
                      k_blk=kh, k_idx=0)
        x = linear(act, P["w_ffn_out"], l, name="ffn_out_b" + tag, n_cols=D, tm=tm_ffo, tn=tn, epilogue=_ep_resid_part, k_blk=kh, k_idx=1,
                   extras=[_tile(part, tn, T, tm_ffo), _tile(x, tn, T, tm_ffo), seq(ga2, tm_ffo)])

    out["conv"] = jnp.stack(convs)
    out["y"] = rmsnorm(x, P["g_final"], tm=tm_mod)
    return out


def kernel(x_prompt, x_sample, state_mlstm_C, state_mlstm_n, state_mlstm_m, cache_conv, cache_k_win, cache_v_win, c_prompt, c_sample, w_ada, g_norm1, g_norm2, w_m_in, b_m_i, b_m_f, g_m_head, w_m_out, w_ada_kv, g_kv, w_kv, b_kv, w_q, b_q, sinks, w_o, b_o, w_ffn_in, w_conv, b_conv, w_ffn_out, g_final):
    D = D_MODEL
    Bp, Tp, _ = x_prompt.shape
    Bs, Ts, _ = x_sample.shape
    assert Ts == 1
    P = dict(g_norm1=g_norm1, g_norm2=g_norm2, w_m_in=w_m_in, b_m_i=b_m_i, b_m_f=b_m_f, g_m_head=g_m_head,
             w_m_out=w_m_out, g_kv=g_kv, w_kv=w_kv[None], b_kv=b_kv, w_q=w_q, b_q=b_q, sinks=sinks, w_o=w_o,
             b_o=b_o, w_ffn_in=w_ffn_in, w_conv=w_conv, b_conv=b_conv, w_ffn_out=w_ffn_out, g_final=g_final)

    n_c = Bp + Bs
    pad_c = -n_c % 16
    cs = silu_cast(jnp.concatenate([c_prompt, c_sample, jnp.zeros((pad_c, D), F32)], axis=0))
    rows_c = n_c + pad_c
    ada_all = []
    for l in range(2):
        a = linear(cs, w_ada, l, name="ada", n_cols=6 * D, tm=rows_c, tn=512, epilogue=_ep_plain)
        ada_all.append(a)
    a_kv = linear(cs, w_ada_kv[None], 0, name="ada_kv", n_cols=2 * D, tm=rows_c, tn=512, epilogue=_ep_plain)

    def split(lo, hi):
        per_layer = [tuple(a[lo:hi, i * D:(i + 1) * D] for i in range(6)) for a in ada_all]
        return per_layer + [(a_kv[lo:hi, :D], a_kv[lo:hi, D:])]

    po = _run(x_prompt.reshape(Bp * Tp, D), split(0, Bp), jnp.arange(Tp, dtype=jnp.int32), None, P,
              B=Bp, T=Tp, tm=1024, tm_ffo=512, tm_mod=256)
    state = dict(mlstm=(state_mlstm_C, state_mlstm_n, state_mlstm_m), conv=cache_conv,
                 kbuf=cache_k_win, vbuf=cache_v_win)
    so = _run(x_sample.reshape(Bs, D), split(Bp, Bp + Bs), PAST_LEN + jnp.arange(1, dtype=jnp.int32), state, P,
              B=Bs, T=1, tm=Bs, tm_ffo=Bs, tm_mod=Bs)
    return (po["y"].reshape(Bp, Tp, D), so["y"].reshape(Bs, 1, D),
            po["C"], po["n"], po["m"], po["conv"], po["k_win"], po["v_win"],
            so["C"], so["n"], so["m"], so["conv"], so["k_win"], so["v_win"])
```

```python
import functools

import jax
import jax.numpy as jnp
from jax import lax
from jax.experimental import pallas as pl
from jax.experimental.pallas import tpu as pltpu

BF = jnp.bfloat16
F32 = jnp.float32

D_MODEL = 4096
M_HEADS = 8
M_QK = 256
M_V = 512
M_CHUNK = 64
GATE_CAP = 15.0
HEAD_DIM = 64
A_HEADS = 64
KV_HEADS = 8
GROUP = 8
WINDOW = 128
ROPE_THETA = 10000.0
D_FF = 11008
PAST_LEN = 16384
EPS = 1e-6

VMEM_LIMIT_V7X = 58 * 1024 * 1024
LANES = 128


def _params(sem):
    return pltpu.CompilerParams(dimension_semantics=sem, vmem_limit_bytes=VMEM_LIMIT_V7X)


def _sigmoid(x):
    return 1.0 / (1.0 + jnp.exp(-x))


def _lag_row(j, i):
    return i * jnp.minimum(j, 1)


def _lag_col(j):
    return jnp.maximum(j - 1, 0)


def _lagged(index_map):
    return lambda j, i: index_map(_lag_col(j), _lag_row(j, i))


def _stage_weight(j, i, nj, w_ref, wb_ref):
    ck = w_ref.shape[0]

    @pl.when(j < nj)
    def _():
        wb_ref[j % 2, pl.ds(pl.multiple_of(i * ck, ck), ck), :] = w_ref[...].astype(BF)


def _linear_kernel(x_ref, w_ref, *rest, n_extra, epilogue, n_sub, transposed):
    extra = rest[:n_extra]
    o_ref = rest[n_extra]
    wb_ref = rest[n_extra + 1]
    j = pl.program_id(0)
    i = pl.program_id(1)
    nj = pl.num_programs(0) - 1
    tm = x_ref.shape[0]
    sub = tm // n_sub
    _stage_weight(j, i, nj, w_ref, wb_ref)

    @pl.when(j > 0)
    def _():
        w = wb_ref[(j + 1) % 2]
        for r in range(n_sub):
            rows = slice(r * sub, (r + 1) * sub)
            xs = x_ref[rows, :].astype(BF)
            if transposed:
                acc = lax.dot_general(xs, w, (((1,), (1,)), ((), ())), preferred_element_type=F32)
            else:
                acc = jnp.dot(xs, w, preferred_element_type=F32)
            ex = [e[rows, :] if e.shape[0] == tm else e[...] for e in extra]
            o_ref[rows, :] = epilogue(acc, *ex).astype(o_ref.dtype)


def linear(x, w3, lead, *, name, n_cols, tm, tn, epilogue, extras=(), out_dtype=F32,
           col_off=0, n_sub=1, transposed=False):
    M, K = x.shape
    assert M % tm == 0 and n_cols % tn == 0 and tm % n_sub == 0
    nj, ni = n_cols // tn, M // tm
    col = lambda j: jnp.minimum(j, nj - 1) + col_off
    if transposed:
        cn = tn // ni
        assert cn * ni == tn and cn % 16 == 0
        w_spec = pl.BlockSpec((None, cn, K), lambda j, i: (lead, col(j) * ni + i, 0))
        wb_shape = (2, tn, K)
    else:
        ck = K // ni
        assert ck * ni == K and ck % 16 == 0
        w_spec = pl.BlockSpec((None, ck, tn), lambda j, i: (lead, i, col(j)))
        wb_shape = (2, K, tn)
    in_specs = [pl.BlockSpec((tm, K), lambda j, i: (_lag_row(j, i), 0)), w_spec]
    in_specs += [pl.BlockSpec(s.block_shape, _lagged(s.index_map)) for _, s in extras]
    return pl.pallas_call(
        functools.partial(_linear_kernel, n_extra=len(extras), epilogue=epilogue, n_sub=n_sub,
                          transposed=transposed),
        grid=(nj + 1, ni),
        in_specs=in_specs,
        out_specs=pl.BlockSpec((tm, tn), lambda j, i: (_lag_row(j, i), _lag_col(j))),
        out_shape=jax.ShapeDtypeStruct((M, n_cols), out_dtype),
        scratch_shapes=[pltpu.VMEM(wb_shape, BF)],
        compiler_params=_params(("arbitrary", "arbitrary")),
        name=name,
    )(x, w3, *[a for a, _ in extras])


def _row_vec(v, tn):
    return v.reshape(1, -1), pl.BlockSpec((1, tn), lambda j, i: (0, j))


def _seq_vec(a, tn, T, tm):
    if T == 1:
        return a, pl.BlockSpec((tm, tn), lambda j, i: (i, j))
    tps = T // tm
    return a.reshape(a.shape[0], 1, a.shape[1]), pl.BlockSpec((None, 1, tn), lambda j, i: (i // tps, 0, j))


def _tile(a, tn, T, tm, col0=0):
    return a, pl.BlockSpec((tm, tn), lambda j, i: (i, j + col0))


def _ep_plain(acc):
    return acc


def _ep_bias(acc, b):
    return acc + b


def _ep_resid(acc, r, g):
    return r + g * acc


def _ep_resid_bias(acc, b, r, g):
    return r + g * (acc + b)


def _ep_rope(acc, b, cos, sin, *, scale):
    y = acc + b
    n = y.shape[1]
    cos = jnp.concatenate([cos] * (n // LANES), axis=1)
    sin = jnp.concatenate([sin] * (n // LANES), axis=1)
    lane = lax.broadcasted_iota(jnp.int32, y.shape, 1)
    first_half = (lane % HEAD_DIM) < (HEAD_DIM // 2)
    partner = jnp.where(first_half, pltpu.roll(y, n - HEAD_DIM // 2, axis=1), pltpu.roll(y, HEAD_DIM // 2, axis=1))
    return (y * cos + partner * sin) * scale


def _ep_gates(acc, b):
    z = GATE_CAP * jnp.tanh((acc + b) / GATE_CAP)
    logsig = jnp.minimum(z, 0.0) - jnp.log(1.0 + jnp.exp(-jnp.abs(z)))
    lane = lax.broadcasted_iota(jnp.int32, z.shape, 1)
    return jnp.where(lane < M_HEADS, z, logsig)


def _silu_kernel(c_ref, o_ref):
    c = c_ref[...]
    o_ref[...] = (c * _sigmoid(c)).astype(o_ref.dtype)


def silu_cast(c):
    return pl.pallas_call(
        _silu_kernel,
        out_shape=jax.ShapeDtypeStruct(c.shape, BF),
        name="silu_c",
    )(c)


def _modulate_kernel(x_ref, g_ref, sh_ref, sc_ref, o_ref):
    x = x_ref[...]
    y = x * lax.rsqrt(jnp.mean(x * x, axis=-1, keepdims=True) + EPS) * g_ref[...]
    o_ref[...] = (y * (1.0 + sc_ref[...]) + sh_ref[...]).astype(o_ref.dtype)


def modulate(x, g, sh, sc, *, T, tm):
    M, D = x.shape
    sh_a, sh_s = _seq_vec(sh, D, T, tm)
    sc_a, sc_s = _seq_vec(sc, D, T, tm)
    fix = lambda s: pl.BlockSpec(s.block_shape, functools.partial(lambda im, i: im(0, i), s.index_map))
    return pl.pallas_call(
        _modulate_kernel,
        grid=(M // tm,),
        in_specs=[pl.BlockSpec((tm, D), lambda i: (i, 0)),
                  pl.BlockSpec((1, D), lambda i: (0, 0)),
                  fix(sh_s), fix(sc_s)],
        out_specs=pl.BlockSpec((tm, D), lambda i: (i, 0)),
        out_shape=jax.ShapeDtypeStruct((M, D), BF),
        compiler_params=_params(("arbitrary",)),
        name="modulate",
    )(x, g.reshape(1, D), sh_a, sc_a)


def _rmsnorm_kernel(x_ref, g_ref, o_ref):
    x = x_ref[...]
    o_ref[...] = x * lax.rsqrt(jnp.mean(x * x, axis=-1, keepdims=True) + EPS) * g_ref[...]


def rmsnorm(x, g, *, tm):
    M, D = x.shape
    return pl.pallas_call(
        _rmsnorm_kernel,
        grid=(M // tm,),
        in_specs=[pl.BlockSpec((tm, D), lambda i: (i, 0)), pl.BlockSpec((1, D), lambda i: (0, 0))],
        out_specs=pl.BlockSpec((tm, D), lambda i: (i, 0)),
        out_shape=jax.ShapeDtypeStruct((M, D), F32),
        compiler_params=_params(("arbitrary",)),
        name="final_norm",
    )(x, g.reshape(1, D))


def _conv_gate(ug, uu, pg1, pg2, pu1, pu2, wcg, wcu, bcg, bcu):
    yg = bcg + pg2 * wcg[0:1] + pg1 * wcg[1:2] + ug * wcg[2:3]
    yu = bcu + pu2 * wcu[0:1] + pu1 * wcu[1:2] + uu * wcu[2:3]
    return yg * _sigmoid(yg) * yu


def _shift_rows(u, carry):
    row = lax.broadcasted_iota(jnp.int32, u.shape, 0)
    p1 = jnp.where(row == 0, carry[1:2], pltpu.roll(u, 1, axis=0))
    p2 = jnp.where(row == 0, carry[0:1], jnp.where(row == 1, carry[1:2], pltpu.roll(u, 2, axis=0)))
    return p1, p2


def _ffn_in_kernel(x_ref, wg_ref, wu_ref, wcg_ref, wcu_ref, bcg_ref, bcu_ref,
                   act_ref, cg_ref, cu_ref, wgb_ref, wub_ref, carg_ref, caru_ref, *, tps, n_sub):
    j = pl.program_id(0)
    i = pl.program_id(1)
    nj = pl.num_programs(0) - 1
    tm = x_ref.shape[0]
    sub = tm // n_sub
    _stage_weight(j, i, nj, wg_ref, wgb_ref)
    _stage_weight(j, i, nj, wu_ref, wub_ref)

    @pl.when(j > 0)
    def _():
        @pl.when(i % tps == 0)
        def _():
            carg_ref[...] = jnp.zeros_like(carg_ref)
            caru_ref[...] = jnp.zeros_like(caru_ref)

        wg = wgb_ref[(j + 1) % 2]
        wu = wub_ref[(j + 1) % 2]
        cg = carg_ref[...]
        cu = caru_ref[...]
        for r in range(n_sub):
            rows = slice(r * sub, (r + 1) * sub)
            x = x_ref[rows, :]
            ug = jnp.dot(x, wg, preferred_element_type=F32)
            uu = jnp.dot(x, wu, preferred_element_type=F32)
            pg1, pg2 = _shift_rows(ug, cg)
            pu1, pu2 = _shift_rows(uu, cu)
            act_ref[rows, :] = _conv_gate(ug, uu, pg1, pg2, pu1, pu2, wcg_ref[...], wcu_ref[...],
                                          bcg_ref[...], bcu_ref[...]).astype(act_ref.dtype)
            cg = ug[sub - 2:sub]
            cu = uu[sub - 2:sub]
        carg_ref[...] = cg
        caru_ref[...] = cu
        cg_ref[...] = cg
        cu_ref[...] = cu


def ffn_in_prompt(x, w3, lead, w_conv, b_conv, *, B, T, tm, tn, n_sub):
    M, K = x.shape
    F = D_FF
    nj = F // tn
    ni = M // tm
    ck = K // ni
    assert ck * ni == K and ck % 16 == 0
    tps = T // tm
    wc = w_conv
    bc = b_conv.reshape(1, 2 * F)
    nxt = lambda j: jnp.minimum(j, nj - 1)
    act, cg, cu = pl.pallas_call(
        functools.partial(_ffn_in_kernel, tps=tps, n_sub=n_sub),
        grid=(nj + 1, ni),
        in_specs=[
            pl.BlockSpec((tm, K), lambda j, i: (_lag_row(j, i), 0)),
            pl.BlockSpec((None, ck, tn), lambda j, i: (lead, i, nxt(j))),
            pl.BlockSpec((None, ck, tn), lambda j, i: (lead, i, nxt(j) + nj)),
            pl.BlockSpec((3, tn), lambda j, i: (0, _lag_col(j))),
            pl.BlockSpec((3, tn), lambda j, i: (0, _lag_col(j) + nj)),
            pl.BlockSpec((1, tn), lambda j, i: (0, _lag_col(j))),
            pl.BlockSpec((1, tn), lambda j, i: (0, _lag_col(j) + nj)),
        ],
        out_specs=[
            pl.BlockSpec((tm, tn), lambda j, i: (_lag_row(j, i), _lag_col(j))),
            pl.BlockSpec((None, 2, tn), lambda j, i: (_lag_row(j, i) // tps, 0, _lag_col(j))),
            pl.BlockSpec((None, 2, tn), lambda j, i: (_lag_row(j, i) // tps, 0, _lag_col(j))),
        ],
        out_shape=[
            jax.ShapeDtypeStruct((M, F), BF),
            jax.ShapeDtypeStruct((B, 2, F), F32),
            jax.ShapeDtypeStruct((B, 2, F), F32),
        ],
        scratch_shapes=[pltpu.VMEM((2, K, tn), BF), pltpu.VMEM((2, K, tn), BF),
                        pltpu.VMEM((2, tn), F32), pltpu.VMEM((2, tn), F32)],
        compiler_params=_params(("arbitrary", "arbitrary")),
        name="ffn_in_p",
    )(x, w3, w3, wc, wc, bc, bc)
    return act, jnp.concatenate([cg, cu], axis=-1)


def _ffn_in_step_kernel(x_ref, wg_ref, wu_ref, wcg_ref, wcu_ref, bcg_ref, bcu_ref,
                        c0g_ref, c0u_ref, c1g_ref, c1u_ref, act_ref, ug_ref, uu_ref):
    x = x_ref[...]
    ug = jnp.dot(x, wg_ref[...].astype(BF), preferred_element_type=F32)
    uu = jnp.dot(x, wu_ref[...].astype(BF), preferred_element_type=F32)
    act_ref[...] = _conv_gate(ug, uu, c1g_ref[...], c0g_ref[...], c1u_ref[...], c0u_ref[...],
                              wcg_ref[...], wcu_ref[...], bcg_ref[...], bcu_ref[...]).astype(act_ref.dtype)
    ug_ref[...] = ug
    uu_ref[...] = uu


def ffn_in_step(x, w3, lead, w_conv, b_conv, cache, *, tn):
    Bs, K = x.shape
    F = D_FF
    nj = F // tn
    bc = b_conv.reshape(1, 2 * F)
    cflat = cache.reshape(Bs, 4 * F)
    vec = lambda off: pl.BlockSpec((Bs, tn), lambda j: (0, j + off * nj))
    act, ug, uu = pl.pallas_call(
        _ffn_in_step_kernel,
        grid=(nj,),
        in_specs=[
            pl.BlockSpec((Bs, K), lambda j: (0, 0)),
            pl.BlockSpec((None, K, tn), lambda j: (lead, 0, j)),
            pl.BlockSpec((None, K, tn), lambda j: (lead, 0, j + nj)),
            pl.BlockSpec((3, tn), lambda j: (0, j)),
            pl.BlockSpec((3, tn), lambda j: (0, j + nj)),
            pl.BlockSpec((1, tn), lambda j: (0, j)),
            pl.BlockSpec((1, tn), lambda j: (0, j + nj)),
            vec(0), vec(1), vec(2), vec(3),
        ],
        out_specs=[pl.BlockSpec((Bs, tn), lambda j: (0, j))] * 3,
        out_shape=[jax.ShapeDtypeStruct((Bs, F), BF),
                   jax.ShapeDtypeStruct((Bs, F), F32),
                   jax.ShapeDtypeStruct((Bs, F), F32)],
        compiler_params=_params(("arbitrary",)),
        name="ffn_in_s",
    )(x, w3, w3, w_conv, w_conv, bc, bc, cflat, cflat, cflat, cflat)
    new_cache = jnp.stack([cache[:, 1, :], jnp.concatenate([ug, uu], axis=-1)], axis=1)
    return act, new_cache


def _mlstm_chunk_kernel(q_ref, k_ref, v_ref, o_ref, g_ref, gt_ref, gh_ref,
                        hs_ref, C_ref, n_ref, m_ref, *, hb):
    hblk = pl.program_id(1)
    c = pl.program_id(2)
    L = q_ref.shape[0]

    @pl.when(c == 0)
    def _():
        C_ref[...] = jnp.zeros_like(C_ref)
        n_ref[...] = jnp.zeros_like(n_ref)
        m_ref[...] = jnp.zeros_like(m_ref)

    g = g_ref[...]
    gt = gt_ref[...]
    lane = lax.broadcasted_iota(jnp.int32, g.shape, 1)
    sub = lax.broadcasted_iota(jnp.int32, gt.shape, 0)
    t_idx = lax.broadcasted_iota(jnp.int32, (L, L), 0)
    s_idx = lax.broadcasted_iota(jnp.int32, (L, L), 1)
    causal = s_idx <= t_idx
    q_scale = jnp.asarray(M_QK ** -0.5, BF)

    for hh_i in range(hb):
        h = hblk * hb + hh_i
        ic_col = jnp.sum(jnp.where(lane == h, g, 0.0), axis=1, keepdims=True)
        fc_col = jnp.sum(jnp.where(lane == h + M_HEADS, g, 0.0), axis=1, keepdims=True)
        ic_row = jnp.sum(jnp.where(sub == h, gt, 0.0), axis=0, keepdims=True)
        fc_row = jnp.sum(jnp.where(sub == h + M_HEADS, gt, 0.0), axis=0, keepdims=True)
        b_col = jnp.sum(jnp.where(causal, fc_row, 0.0), axis=1, keepdims=True)
        b_row = jnp.sum(jnp.where(t_idx <= s_idx, fc_col, 0.0), axis=0, keepdims=True)

        m_old = m_ref[hh_i][0:1, 0:1]
        d = jnp.where(causal, b_col - b_row + ic_row, -jnp.inf)
        inter = b_col + m_old
        mt = jnp.maximum(inter, jnp.max(d, axis=1, keepdims=True))
        w = jnp.exp(d - mt)
        si = jnp.exp(inter - mt)

        qb = q_ref[:, hh_i * M_QK:(hh_i + 1) * M_QK] * q_scale
        kb = k_ref[:, hh_i * M_QK:(hh_i + 1) * M_QK]
        vb = v_ref[:, hh_i * M_V:(hh_i + 1) * M_V]
        C_old = C_ref[hh_i]
        n_old = n_ref[hh_i]
        qk = lax.dot_general(qb, kb, (((1,), (1,)), ((), ())), preferred_element_type=F32) * w
        num = (jnp.dot(qk.astype(BF), vb, preferred_element_type=F32)
               + si * jnp.dot(qb, C_old.astype(BF), preferred_element_type=F32))
        den = jnp.sum(qk, axis=1, keepdims=True) + si * jnp.sum(qb.astype(F32) * n_old, axis=1, keepdims=True)
        hh = num / jnp.maximum(jnp.abs(den), jnp.exp(-mt))

        b_last = b_col[L - 1:L, :]
        gl = b_last - b_col + ic_col
        m_new = jnp.maximum(b_last + m_old, jnp.max(gl, axis=0, keepdims=True))
        wl = jnp.exp(gl - m_new)
        sd = jnp.exp(b_last + m_old - m_new)
        kw = kb.astype(F32) * wl
        C_ref[hh_i] = sd * C_old + jnp.dot(kw.T.astype(BF), vb, preferred_element_type=F32)
        n_ref[hh_i] = sd * n_old + jnp.sum(kw, axis=0, keepdims=True)
        m_ref[hh_i] = jnp.broadcast_to(m_new, (1, LANES))

        hn = hh * lax.rsqrt(jnp.mean(hh * hh, axis=1, keepdims=True) + EPS)
        o = o_ref[:, hh_i * M_V:(hh_i + 1) * M_V].astype(F32)
        hs_ref[:, hh_i * M_V:(hh_i + 1) * M_V] = (
            hn * gh_ref[:, hh_i * M_V:(hh_i + 1) * M_V] * _sigmoid(o)).astype(hs_ref.dtype)


def mlstm_prompt(z, gates, g_head, *, B, T, L, hb):
    M = B * T
    nc = T // L
    H = M_HEADS
    gt = jnp.swapaxes(gates[:, :2 * H].reshape(B * nc, L, 2 * H), 1, 2)
    row = lambda b, h, c: b * nc + c
    nhb = H // hb
    kq = nhb
    kv = nhb
    ko = 2 * nhb
    hs, C, n, m = pl.pallas_call(
        functools.partial(_mlstm_chunk_kernel, hb=hb),
        grid=(B, nhb, nc),
        in_specs=[
            pl.BlockSpec((L, hb * M_QK), lambda b, h, c: (row(b, h, c), h)),
            pl.BlockSpec((L, hb * M_QK), lambda b, h, c: (row(b, h, c), kq + h)),
            pl.BlockSpec((L, hb * M_V), lambda b, h, c: (row(b, h, c), kv + h)),
            pl.BlockSpec((L, hb * M_V), lambda b, h, c: (row(b, h, c), ko + h)),
            pl.BlockSpec((L, LANES), lambda b, h, c: (row(b, h, c), 0)),
            pl.BlockSpec((None, 2 * H, L), lambda b, h, c: (row(b, h, c), 0, 0)),
            pl.BlockSpec((1, hb * M_V), lambda b, h, c: (0, h)),
        ],
        out_specs=[
            pl.BlockSpec((L, hb * M_V), lambda b, h, c: (row(b, h, c), h)),
            pl.BlockSpec((None, hb, M_QK, M_V), lambda b, h, c: (b, h, 0, 0)),
            pl.BlockSpec((None, hb, 1, M_QK), lambda b, h, c: (b, h, 0, 0)),
            pl.BlockSpec((None, hb, 1, LANES), lambda b, h, c: (b, h, 0, 0)),
        ],
        out_shape=[
            jax.ShapeDtypeStruct((M, D_MODEL), BF),
            jax.ShapeDtypeStruct((B, M_HEADS, M_QK, M_V), F32),
            jax.ShapeDtypeStruct((B, M_HEADS, 1, M_QK), F32),
            jax.ShapeDtypeStruct((B, M_HEADS, 1, LANES), F32),
        ],
        compiler_params=_params(("arbitrary", "arbitrary", "arbitrary")),
        name="mlstm_p",
    )(z, z, z, z, gates, gt, g_head.reshape(1, D_MODEL))
    return hs, C, n[:, :, 0, :], m[:, :, 0, 0]


def _mlstm_step_kernel(q_ref, k_ref, v_ref, o_ref, ig_ref, lf_ref, m0_ref, n0_ref, C0_ref, gh_ref,
                       hs_ref, C_ref, n_ref, m_ref):
    ic = ig_ref[...]
    fc = lf_ref[...]
    m_old = m0_ref[...]
    inter = fc + m_old
    mt = jnp.maximum(inter, ic)
    w = jnp.exp(ic - mt)
    si = jnp.exp(inter - mt)
    q = (q_ref[...] * (M_QK ** -0.5)).astype(BF).astype(F32)
    k = k_ref[...]
    v = v_ref[...]
    n_old = n0_ref[...]
    qk = jnp.sum(q * k.astype(BF).astype(F32), axis=1, keepdims=True) * w
    wl = jnp.exp(ic - mt)
    sd = jnp.exp(inter - mt)
    kw = k * wl
    qT = q.T
    kwT = kw.T
    rows = []
    for h in range(M_HEADS):
        C_old = C0_ref[h]
        rows.append(jnp.sum(qT[:, h:h + 1] * C_old, axis=0, keepdims=True))
        C_ref[h] = sd[h:h + 1, :] * C_old + kwT[:, h:h + 1] * v[h:h + 1, :]
    qC = jnp.concatenate(rows, axis=0)
    num = qk.astype(BF).astype(F32) * v.astype(BF).astype(F32) + si * qC
    den = qk + si * jnp.sum(q * n_old, axis=1, keepdims=True)
    hh = num / jnp.maximum(jnp.abs(den), jnp.exp(-mt))
    n_ref[...] = sd * n_old + kw
    m_ref[...] = mt
    hn = hh * lax.rsqrt(jnp.mean(hh * hh, axis=1, keepdims=True) + EPS)
    hs_ref[...] = (hn * gh_ref[...] * _sigmoid(o_ref[...])).astype(hs_ref.dtype)


def mlstm_step(z, gates, g_head, C0, n0, m0):
    Bs = z.shape[0]
    H = M_HEADS
    qd, vd = H * M_QK, H * M_V
    q = z[:, :qd].reshape(Bs, H, M_QK)
    k = z[:, qd:2 * qd].reshape(Bs, H, M_QK)
    v = z[:, 2 * qd:2 * qd + vd].reshape(Bs, H, M_V)
    o = z[:, 2 * qd + vd:].reshape(Bs, H, M_V)
    ig = gates[:, :H].reshape(Bs, H, 1)
    lf = gates[:, H:2 * H].reshape(Bs, H, 1)
    per_b = lambda *tail: pl.BlockSpec((None,) + tail, lambda b: (b,) + (0,) * len(tail))
    hs, C, n, m = pl.pallas_call(
        _mlstm_step_kernel,
        grid=(Bs,),
        in_specs=[
            per_b(H, M_QK), per_b(H, M_QK), per_b(H, M_V), per_b(H, M_V),
            per_b(H, 1), per_b(H, 1), per_b(H, 1),
            pl.BlockSpec((None, None, H, M_QK), lambda b: (0, b, 0, 0)),
            pl.BlockSpec((None, None, H, M_QK, M_V), lambda b: (0, b, 0, 0, 0)),
            pl.BlockSpec((H, M_V), lambda b: (0, 0)),
        ],
        out_specs=[
            per_b(H, M_V),
            pl.BlockSpec((None, None, H, M_QK, M_V), lambda b: (0, b, 0, 0, 0)),
            pl.BlockSpec((None, None, H, M_QK), lambda b: (0, b, 0, 0)),
            per_b(H, 1),
        ],
        out_shape=[
            jax.ShapeDtypeStruct((Bs, H, M_V), BF),
            jax.ShapeDtypeStruct(C0.shape, F32),
            jax.ShapeDtypeStruct(n0.shape, F32),
            jax.ShapeDtypeStruct((Bs, H, 1), F32),
        ],
        compiler_params=_params(("arbitrary",)),
        name="mlstm_s",
    )(q, k, v, o, ig, lf, m0.reshape(Bs, H, 1), n0, C0, g_head.reshape(H, M_V))
    return hs.reshape(Bs, D_MODEL), C, n, m.reshape(1, Bs, H)


def _attn_kernel(sink_ref, q_ref, kc_ref, kp_ref, vc_ref, vp_ref, o_ref):
    n = pl.program_id(1)
    W = WINDOW
    R = GROUP * W
    kcat = jnp.concatenate([kp_ref[...], kc_ref[...]], axis=0).astype(BF)
    vcat = jnp.concatenate([vp_ref[...], vc_ref[...]], axis=0).astype(BF)
    i_idx = lax.broadcasted_iota(jnp.int32, (R, 2 * W), 0) % W
    j_idx = lax.broadcasted_iota(jnp.int32, (R, 2 * W), 1)
    has_prev = jnp.where(n > 0, 1, 0)
    lo = has_prev * (i_idx + 1) + (1 - has_prev) * W
    valid = (j_idx >= lo) & (j_idx <= i_idx + W)
    rowg = lax.broadcasted_iota(jnp.int32, (R, 1), 0) // W
    for h in range(KV_HEADS):
        qh = jnp.concatenate(
            [q_ref[:, (h * GROUP + g) * HEAD_DIM:(h * GROUP + g + 1) * HEAD_DIM] for g in range(GROUP)], axis=0)
        kh = kcat[:, h * HEAD_DIM:(h + 1) * HEAD_DIM]
        vh = vcat[:, h * HEAD_DIM:(h + 1) * HEAD_DIM]
        s = lax.dot_general(qh, kh, (((1,), (1,)), ((), ())), preferred_element_type=F32)
        s = jnp.where(valid, s, -jnp.inf)
        sk = jnp.zeros((R, 1), F32)
        for g in range(GROUP):
            sk = jnp.where(rowg == g, sink_ref[h * GROUP + g], sk)
        mx = jnp.maximum(jnp.max(s, axis=1, keepdims=True), sk)
        p = jnp.exp(s - mx)
        p = p / (jnp.sum(p, axis=1, keepdims=True) + jnp.exp(sk - mx))
        oh = jnp.dot(p.astype(BF), vh, preferred_element_type=F32)
        for g in range(GROUP):
            c0 = (h * GROUP + g) * HEAD_DIM
            o_ref[:, c0:c0 + HEAD_DIM] = oh[g * W:(g + 1) * W, :].astype(o_ref.dtype)


def attention_prompt(q, k, v, sinks, *, B, T):
    M = B * T
    W = WINDOW
    nb = T // W
    kvd = KV_HEADS * HEAD_DIM
    cur = lambda b, n: (b * nb + n, 0)
    prev = lambda b, n: (b * nb + jnp.maximum(n - 1, 0), 0)
    return pl.pallas_call(
        _attn_kernel,
        grid=(B, nb),
        in_specs=[
            pl.BlockSpec(memory_space=pltpu.SMEM),
            pl.BlockSpec((W, D_MODEL), cur),
            pl.BlockSpec((W, kvd), cur), pl.BlockSpec((W, kvd), prev),
            pl.BlockSpec((W, kvd), cur), pl.BlockSpec((W, kvd), prev),
        ],
        out_specs=pl.BlockSpec((W, D_MODEL), cur),
        out_shape=jax.ShapeDtypeStruct((M, D_MODEL), BF),
        compiler_params=_params(("arbitrary", "arbitrary")),
        name="attn_p",
    )(sinks, q, k, k, v, v)


def _per_head_rows(row):
    return jnp.concatenate(
        [jnp.broadcast_to(row[:, h * HEAD_DIM:(h + 1) * HEAD_DIM], (GROUP, HEAD_DIM)) for h in range(KV_HEADS)],
        axis=0)


def _attn_step_kernel(q_ref, kc_ref, vc_ref, kn_ref, vn_ref, sk_ref, o_ref, ko_ref, vo_ref, *, bb):
    W = WINDOW
    jl = lax.broadcasted_iota(jnp.int32, (A_HEADS, W), 1)
    sk = sk_ref[...]
    heads = lambda a, h: a[h * GROUP:(h + 1) * GROUP, :]
    scores = []
    for b in range(bb):
        q = q_ref[b]
        kcb = kc_ref[b].astype(BF)
        s = jnp.concatenate(
            [lax.dot_general(heads(q, h).astype(BF), kcb[:, h * HEAD_DIM:(h + 1) * HEAD_DIM],
                             (((1,), (1,)), ((), ())), preferred_element_type=F32) for h in range(KV_HEADS)],
            axis=0)
        kne = _per_head_rows(kn_ref[b]).astype(BF).astype(F32)
        sn = jnp.sum(q.astype(BF).astype(F32) * kne, axis=1, keepdims=True)
        scores.append((s, sn))
    probs = []
    for s, sn in scores:
        s = jnp.where(jl >= 1, s, -jnp.inf)
        mx = jnp.maximum(jnp.maximum(jnp.max(s, axis=1, keepdims=True), sn), sk)
        p = jnp.exp(s - mx)
        pn = jnp.exp(sn - mx)
        den = jnp.sum(p, axis=1, keepdims=True) + pn + jnp.exp(sk - mx)
        probs.append((p / den, pn / den))
    for b in range(bb):
        p, pn = probs[b]
        vcb = vc_ref[b].astype(BF)
        o = jnp.concatenate(
            [jnp.dot(heads(p, h).astype(BF), vcb[:, h * HEAD_DIM:(h + 1) * HEAD_DIM],
                     preferred_element_type=F32) for h in range(KV_HEADS)], axis=0)
        vne = _per_head_rows(vn_ref[b]).astype(BF).astype(F32)
        o_ref[b] = o + pn.astype(BF).astype(F32) * vne
        ko_ref[b, 0:W - 1, :] = kc_ref[b, 1:W, :]
        ko_ref[b, W - 1:W, :] = kn_ref[b]
        vo_ref[b, 0:W - 1, :] = vc_ref[b, 1:W, :]
        vo_ref[b, W - 1:W, :] = vn_ref[b]


def attention_step(q, k_new, v_new, kbuf, vbuf, sinks):
    Bs = q.shape[0]
    W = WINDOW
    kvd = KV_HEADS * HEAD_DIM
    bb = 4
    assert Bs % bb == 0
    per_b = lambda *tail: pl.BlockSpec((bb,) + tail, lambda b: (b,) + (0,) * len(tail))
    o, ko, vo = pl.pallas_call(
        functools.partial(_attn_step_kernel, bb=bb),
        grid=(Bs // bb,),
        in_specs=[per_b(A_HEADS, HEAD_DIM), per_b(W, kvd), per_b(W, kvd), per_b(1, kvd), per_b(1, kvd),
                  pl.BlockSpec((A_HEADS, 1), lambda b: (0, 0))],
        out_specs=[per_b(A_HEADS, HEAD_DIM), per_b(W, kvd), per_b(W, kvd)],
        out_shape=[jax.ShapeDtypeStruct((Bs, A_HEADS, HEAD_DIM), F32),
                   jax.ShapeDtypeStruct((Bs, W, kvd), F32),
                   jax.ShapeDtypeStruct((Bs, W, kvd), F32)],
        compiler_params=_params(("arbitrary",)),
        name="attn_s",
    )(q.reshape(Bs, A_HEADS, HEAD_DIM), kbuf.reshape(Bs, W, kvd), vbuf.reshape(Bs, W, kvd),
      k_new.reshape(Bs, 1, kvd), v_new.reshape(Bs, 1, kvd), sinks.reshape(A_HEADS, 1))
    return (o.reshape(Bs, D_MODEL), ko.reshape(Bs, W, KV_HEADS, HEAD_DIM), vo.reshape(Bs, W, KV_HEADS, HEAD_DIM))


def _rope_tables(pos, width):
    half = HEAD_DIM // 2
    freq = ROPE_THETA ** (-jnp.arange(half, dtype=F32) / half)
    ang = pos.astype(F32)[:, None] * freq[None, :]
    cos = jnp.cos(ang)
    sin = jnp.sin(ang)
    reps = width // HEAD_DIM
    return (jnp.tile(jnp.concatenate([cos, cos], axis=1), (1, reps)),
            jnp.tile(jnp.concatenate([-sin, sin], axis=1), (1, reps)))


def _tiles(T):
    if T > 1:
        return dict(tm=1024, tn=1024, n_sub=2,
                    tm_r=512, tn_r=1024, n_sub_r=1,
                    tn_kv=512, tn_ffi=256, n_sub_ffi=2,
                    tm_ffo=512, tn_ffo=512, n_sub_ffo=1, tm_mod=256)
    return dict(tm=None, tn=512, n_sub=1, tm_r=None, tn_r=512, n_sub_r=1, tn_kv=512,
                tn_ffi=256, n_sub_ffi=1, tm_ffo=None, tn_ffo=256, n_sub_ffo=1, tm_mod=None)


def _run(x, ada, pos, state, P, *, B, T):
    D = D_MODEL
    M = B * T
    c = {k: (M if v is None else v) for k, v in _tiles(T).items()}
    tm, tn, tm_r, tn_r, tm_mod = c["tm"], c["tn"], c["tm_r"], c["tn_r"], c["tm_mod"]
    cos, sin = _rope_tables(pos, LANES)
    if T > 1:
        tps = T // tm
        rope_spec = pl.BlockSpec((tm, LANES), lambda j, i: (i % tps, 0))
    else:
        rope_spec = pl.BlockSpec((1, LANES), lambda j, i: (0, 0))

    out = {}
    tag = "_p" if T > 1 else "_s"

    sh1, sc1, ga1, sh2, sc2, ga2 = ada[0]
    hn = modulate(x, P["g_norm1"][0], sh1, sc1, T=T, tm=tm_mod)
    qd, vd = M_HEADS * M_QK, M_HEADS * M_V
    w_inT = jnp.swapaxes(P["w_m_in"], 1, 2)
    z = linear(hn, w_inT, 0, name="m_in" + tag, n_cols=2 * qd + 2 * vd, tm=tm, tn=tn, n_sub=c["n_sub"],
               epilogue=_ep_plain, transposed=True, out_dtype=BF if T > 1 else F32)
    wgT = jnp.pad(w_inT[:, 2 * qd + 2 * vd:, :], ((0, 0), (0, LANES - 2 * M_HEADS), (0, 0)))
    bg = jnp.pad(jnp.concatenate([P["b_m_i"][0], P["b_m_f"][0]]), (0, LANES - 2 * M_HEADS))
    gates = linear(hn, wgT, 0, name="m_gates" + tag, n_cols=LANES, tm=tm, tn=LANES, epilogue=_ep_gates,
                   extras=[_row_vec(bg, LANES)], transposed=True)
    if T > 1:
        hs, C, n, m = mlstm_prompt(z, gates, P["g_m_head"][0], B=B, T=T, L=256, hb=2)
        out["C"], out["n"], out["m"] = C[None], n[None], m[None]
    else:
        hs, out["C"], out["n"], out["m"] = mlstm_step(z, gates, P["g_m_head"][0], *state["mlstm"])
    x = linear(hs, P["w_m_out"], 0, name="m_out" + tag, n_cols=D, tm=tm_r, tn=tn_r, n_sub=c["n_sub_r"],
               epilogue=_ep_resid, extras=[_tile(x, tn_r, T, tm_r), _seq_vec(ga1, tn_r, T, tm_r)])

    convs = []
    for l in range(2):
        if l == 1:
            sh1, sc1, ga1, sh2, sc2, ga2 = ada[1]
            shk, sck = ada[2]
            hk = modulate(x, P["g_kv"], shk, sck, T=T, tm=tm_mod)
            kvd = KV_HEADS * HEAD_DIM
            tn_kv = c["tn_kv"]
            k = linear(hk, P["w_kv"], 0, name="kv_k" + tag, n_cols=kvd, tm=tm, tn=tn_kv, n_sub=c["n_sub"],
                       epilogue=functools.partial(_ep_rope, scale=1.0),
                       extras=[_row_vec(P["b_kv"][:kvd], tn_kv), (cos, rope_spec), (sin, rope_spec)])
            v = linear(hk, P["w_kv"], 0, name="kv_v" + tag, n_cols=kvd, tm=tm, tn=tn_kv, n_sub=c["n_sub"],
                       col_off=kvd // tn_kv, epilogue=_ep_bias, extras=[_row_vec(P["b_kv"][kvd:], tn_kv)])
            hn = modulate(x, P["g_norm1"][1], sh1, sc1, T=T, tm=tm_mod)
            q = linear(hn, P["w_q"], 0, name="attn_q" + tag, n_cols=D, tm=tm, tn=tn, n_sub=c["n_sub"],
                       epilogue=functools.partial(_ep_rope, scale=HEAD_DIM ** -0.5),
                       extras=[_row_vec(P["b_q"][0], tn), (cos, rope_spec), (sin, rope_spec)],
                       out_dtype=BF if T > 1 else F32)
            if T > 1:
                o = attention_prompt(q, k, v, P["sinks"][0], B=B, T=T)
                last = lambda a: a.reshape(B, T, kvd)[:, T - WINDOW:, :].reshape(B, WINDOW, KV_HEADS, HEAD_DIM)
                out["k_win"], out["v_win"] = last(k), last(v)
            else:
                o, out["k_win"], out["v_win"] = attention_step(q, k, v, state["kbuf"], state["vbuf"], P["sinks"][0])
            x = linear(o, P["w_o"], 0, name="attn_o" + tag, n_cols=D, tm=tm_r, tn=tn_r, n_sub=c["n_sub_r"],
                       epilogue=_ep_resid_bias,
                       extras=[_row_vec(P["b_o"][0], tn_r), _tile(x, tn_r, T, tm_r), _seq_vec(ga1, tn_r, T, tm_r)])

        hn = modulate(x, P["g_norm2"][l], sh2, sc2, T=T, tm=tm_mod)
        if T > 1:
            act, cb = ffn_in_prompt(hn, P["w_ffn_in"], l, P["w_conv"][l], P["b_conv"][l], B=B, T=T, tm=tm,
                                    tn=c["tn_ffi"], n_sub=c["n_sub_ffi"])
        else:
            act, cb = ffn_in_step(hn, P["w_ffn_in"], l, P["w_conv"][l], P["b_conv"][l], state["conv"][l],
                                  tn=c["tn_ffi"])
        convs.append(cb)
        tm_o, tn_o = c["tm_ffo"], c["tn_ffo"]
        x = linear(act, P["w_ffn_out"], l, name="ffn_out" + tag, n_cols=D, tm=tm_o, tn=tn_o, n_sub=c["n_sub_ffo"],
                   epilogue=_ep_resid, extras=[_tile(x, tn_o, T, tm_o), _seq_vec(ga2, tn_o, T, tm_o)])

    out["conv"] = jnp.stack(convs)
    out["y"] = rmsnorm(x, P["g_final"], tm=tm_mod)
    return out


def kernel(x_prompt, x_sample, state_mlstm_C, state_mlstm_n, state_mlstm_m, cache_conv, cache_k_win, cache_v_win, c_prompt, c_sample, w_ada, g_norm1, g_norm2, w_m_in, b_m_i, b_m_f, g_m_head, w_m_out, w_ada_kv, g_kv, w_kv, b_kv, w_q, b_q, sinks, w_o, b_o, w_ffn_in, w_conv, b_conv, w_ffn_out, g_final):
    D = D_MODEL
    Bp, Tp, _ = x_prompt.shape
    Bs, Ts, _ = x_sample.shape
    assert Ts == 1
    P = dict(g_norm1=g_norm1, g_norm2=g_norm2, w_m_in=w_m_in, b_m_i=b_m_i, b_m_f=b_m_f, g_m_head=g_m_head,
             w_m_out=w_m_out, g_kv=g_kv, w_kv=w_kv[None], b_kv=b_kv, w_q=w_q, b_q=b_q, sinks=sinks, w_o=w_o,
             b_o=b_o, w_ffn_in=w_ffn_in, w_conv=w_conv, b_conv=b_conv, w_ffn_out=w_ffn_out, g_final=g_final)

    n_c = Bp + Bs
    pad_c = -n_c % 16
    cs = silu_cast(jnp.concatenate([c_prompt, c_sample, jnp.zeros((pad_c, D), F32)], axis=0))
    rows_c = n_c + pad_c
    ada_all = []
    for l in range(2):
        a = linear(cs, w_ada, l, name="ada", n_cols=6 * D, tm=rows_c, tn=512, epilogue=_ep_plain)
        ada_all.append(a)
    a_kv = linear(cs, w_ada_kv[None], 0, name="ada_kv", n_cols=2 * D, tm=rows_c, tn=512, epilogue=_ep_plain)

    def split(lo, hi):
        per_layer = [tuple(a[lo:hi, i * D:(i + 1) * D] for i in range(6)) for a in ada_all]
        return per_layer + [(a_kv[lo:hi, :D], a_kv[lo:hi, D:])]

    po = _run(x_prompt.reshape(Bp * Tp, D), split(0, Bp), jnp.arange(Tp, dtype=jnp.int32), None, P, B=Bp, T=Tp)
    state = dict(mlstm=(state_mlstm_C, state_mlstm_n, state_mlstm_m), conv=cache_conv,
                 kbuf=cache_k_win, vbuf=cache_v_win)
    so = _run(x_sample.reshape(Bs, D), split(Bp, Bp + Bs), PAST_LEN + jnp.arange(1, dtype=jnp.int32), state, P,
              B=Bs, T=1)
    return (po["y"].reshape(Bp, Tp, D), so["y"].reshape(Bs, 1, D),
            po["C"], po["n"], po["m"], po["conv"], po["k_win"], po["v_win"],
            so["C"], so["n"], so["m"], so["conv"], so["k_win"], so["v_win"])
```

```python
import functools

import jax
import jax.numpy as jnp
from jax import lax
from jax.experimental import pallas as pl
from jax.experimental.pallas import tpu as pltpu

BF = jnp.bfloat16
F32 = jnp.float32

D_MODEL = 4096
M_HEADS = 8
M_QK = 256
M_V = 512
M_CHUNK = 64
GATE_CAP = 15.0
HEAD_DIM = 64
A_HEADS = 64
KV_HEADS = 8
GROUP = 8
WINDOW = 128
ROPE_THETA = 10000.0
D_FF = 11008
PAST_LEN = 16384
EPS = 1e-6

VMEM_LIMIT_V7X = 58 * 1024 * 1024
LANES = 128


def _params(sem):
    return pltpu.CompilerParams(dimension_semantics=sem, vmem_limit_bytes=VMEM_LIMIT_V7X)


def _sigmoid(x):
    return 1.0 / (1.0 + jnp.exp(-x))


def _lag_row(j, i):
    return i * jnp.minimum(j, 1)


def _lag_col(j):
    return jnp.maximum(j - 1, 0)


def _lagged(index_map):
    return lambda j, i: index_map(_lag_col(j), _lag_row(j, i))


def _stage_weight(j, i, nj, w_ref, wb_ref, col0=0):
    ck, cw = w_ref.shape

    @pl.when(j < nj)
    def _():
        wb_ref[j % 2, pl.ds(pl.multiple_of(i * ck, ck), ck), col0:col0 + cw] = w_ref[...].astype(BF)


def _linear_kernel(x_ref, w_ref, *rest, n_extra, epilogue, n_sub, transposed, rows_interleaved):
    extra = rest[:n_extra]
    o_ref = rest[n_extra]
    wb_ref = rest[n_extra + 1]
    j = pl.program_id(0)
    i = pl.program_id(1)
    nj = pl.num_programs(0) - 1
    tm = x_ref.shape[0]
    sub = tm // n_sub
    _stage_weight(j, i, nj, w_ref, wb_ref)

    @pl.when(j > 0)
    def _():
        w = wb_ref[(j + 1) % 2]
        for r in range(n_sub):
            rows = slice(r * sub, (r + 1) * sub)
            xs = x_ref[rows, :].astype(BF)
            if transposed:
                acc = lax.dot_general(xs, w, (((1,), (1,)), ((), ())), preferred_element_type=F32)
            else:
                acc = jnp.dot(xs, w, preferred_element_type=F32)
            if rows_interleaved:
                assert sub == PERM_GROUP
                acc = pltpu.einshape("qsd->sqd", acc.reshape(PERM_Q, SUBLANES, acc.shape[1])).reshape(acc.shape)
            ex = [e[rows, :] if e.shape[0] == tm else e[...] for e in extra]
            o_ref[rows, :] = epilogue(acc, *ex).astype(o_ref.dtype)


def linear(x, w3, lead, *, name, n_cols, tm, tn, epilogue, extras=(), out_dtype=F32,
           col_off=0, n_sub=1, transposed=False, rows_interleaved=False):
    M, K = x.shape
    assert M % tm == 0 and n_cols % tn == 0 and tm % n_sub == 0
    nj, ni = n_cols // tn, M // tm
    col = lambda j: jnp.minimum(j, nj - 1) + col_off
    if transposed:
        cn = tn // ni
        assert cn * ni == tn and cn % 16 == 0
        w_spec = pl.BlockSpec((None, cn, K), lambda j, i: (lead, col(j) * ni + i, 0))
        wb_shape = (2, tn, K)
    else:
        ck = K // ni
        assert ck * ni == K and ck % 16 == 0
        w_spec = pl.BlockSpec((None, ck, tn), lambda j, i: (lead, i, col(j)))
        wb_shape = (2, K, tn)
    in_specs = [pl.BlockSpec((tm, K), lambda j, i: (_lag_row(j, i), 0)), w_spec]
    in_specs += [pl.BlockSpec(s.block_shape, _lagged(s.index_map)) for _, s in extras]
    return pl.pallas_call(
        functools.partial(_linear_kernel, n_extra=len(extras), epilogue=epilogue, n_sub=n_sub,
                          transposed=transposed, rows_interleaved=rows_interleaved),
        grid=(nj + 1, ni),
        in_specs=in_specs,
        out_specs=pl.BlockSpec((tm, tn), lambda j, i: (_lag_row(j, i), _lag_col(j))),
        out_shape=jax.ShapeDtypeStruct((M, n_cols), out_dtype),
        scratch_shapes=[pltpu.VMEM(wb_shape, BF)],
        compiler_params=_params(("arbitrary", "arbitrary")),
        name=name,
    )(x, w3, *[a for a, _ in extras])


def _row_vec(v, tn):
    return v.reshape(1, -1), pl.BlockSpec((1, tn), lambda j, i: (0, j))


def _seq_vec(a, tn, T, tm):
    if T == 1:
        return a, pl.BlockSpec((tm, tn), lambda j, i: (i, j))
    tps = T // tm
    return a.reshape(a.shape[0], 1, a.shape[1]), pl.BlockSpec((None, 1, tn), lambda j, i: (i // tps, 0, j))


def _tile(a, tn, T, tm, col0=0):
    return a, pl.BlockSpec((tm, tn), lambda j, i: (i, j + col0))


def _ep_plain(acc):
    return acc


def _ep_bias(acc, b):
    return acc + b


def _ep_resid(acc, r, g):
    return r + g * acc


def _ep_resid_bias(acc, b, r, g):
    return r + g * (acc + b)


def _ep_rope(acc, b, cos, sin, *, scale):
    y = acc + b
    n = y.shape[1]
    cos = jnp.concatenate([cos] * (n // LANES), axis=1)
    sin = jnp.concatenate([sin] * (n // LANES), axis=1)
    lane = lax.broadcasted_iota(jnp.int32, y.shape, 1)
    first_half = (lane % HEAD_DIM) < (HEAD_DIM // 2)
    partner = jnp.where(first_half, pltpu.roll(y, n - HEAD_DIM // 2, axis=1), pltpu.roll(y, HEAD_DIM // 2, axis=1))
    return (y * cos + partner * sin) * scale


def _ep_gates(acc, b):
    z = GATE_CAP * jnp.tanh((acc + b) / GATE_CAP)
    logsig = jnp.minimum(z, 0.0) - jnp.log(1.0 + jnp.exp(-jnp.abs(z)))
    lane = lax.broadcasted_iota(jnp.int32, z.shape, 1)
    return jnp.where(lane < M_HEADS, z, logsig)


def _silu_kernel(c_ref, o_ref):
    c = c_ref[...]
    o_ref[...] = (c * _sigmoid(c)).astype(o_ref.dtype)


def silu_cast(c):
    return pl.pallas_call(
        _silu_kernel,
        out_shape=jax.ShapeDtypeStruct(c.shape, BF),
        name="silu_c",
    )(c)


def _modulate_kernel(x_ref, g_ref, sh_ref, sc_ref, o_ref):
    x = x_ref[...]
    y = x * lax.rsqrt(jnp.mean(x * x, axis=-1, keepdims=True) + EPS) * g_ref[...]
    o_ref[...] = (y * (1.0 + sc_ref[...]) + sh_ref[...]).astype(o_ref.dtype)


def _modulate_interleave_kernel(x_ref, g_ref, sh_ref, sc_ref, o_ref):
    x = x_ref[...]
    y = x * lax.rsqrt(jnp.mean(x * x, axis=-1, keepdims=True) + EPS) * g_ref[...]
    y = y * (1.0 + sc_ref[...]) + sh_ref[...]
    D = y.shape[1]
    y = pltpu.einshape("sqd->qsd", y.reshape(SUBLANES, PERM_Q, D)).reshape(PERM_GROUP, D)
    o_ref[...] = y.astype(o_ref.dtype)


def modulate(x, g, sh, sc, *, T, tm, interleave=False):
    M, D = x.shape
    assert not interleave or (tm == PERM_GROUP and T > 1)
    sh_a, sh_s = _seq_vec(sh, D, T, tm)
    sc_a, sc_s = _seq_vec(sc, D, T, tm)
    fix = lambda s: pl.BlockSpec(s.block_shape, functools.partial(lambda im, i: im(0, i), s.index_map))
    return pl.pallas_call(
        _modulate_interleave_kernel if interleave else _modulate_kernel,
        grid=(M // tm,),
        in_specs=[pl.BlockSpec((tm, D), lambda i: (i, 0)),
                  pl.BlockSpec((1, D), lambda i: (0, 0)),
                  fix(sh_s), fix(sc_s)],
        out_specs=pl.BlockSpec((tm, D), lambda i: (i, 0)),
        out_shape=jax.ShapeDtypeStruct((M, D), BF),
        compiler_params=_params(("arbitrary",)),
        name="modulate",
    )(x, g.reshape(1, D), sh_a, sc_a)


def _rmsnorm_kernel(x_ref, g_ref, o_ref):
    x = x_ref[...]
    o_ref[...] = x * lax.rsqrt(jnp.mean(x * x, axis=-1, keepdims=True) + EPS) * g_ref[...]


def rmsnorm(x, g, *, tm):
    M, D = x.shape
    return pl.pallas_call(
        _rmsnorm_kernel,
        grid=(M // tm,),
        in_specs=[pl.BlockSpec((tm, D), lambda i: (i, 0)), pl.BlockSpec((1, D), lambda i: (0, 0))],
        out_specs=pl.BlockSpec((tm, D), lambda i: (i, 0)),
        out_shape=jax.ShapeDtypeStruct((M, D), F32),
        compiler_params=_params(("arbitrary",)),
        name="final_norm",
    )(x, g.reshape(1, D))


def _conv_gate(ug, uu, pg1, pg2, pu1, pu2, wcg, wcu, bcg, bcu):
    yg = bcg + pg2 * wcg[0:1] + pg1 * wcg[1:2] + ug * wcg[2:3]
    yu = bcu + pu2 * wcu[0:1] + pu1 * wcu[1:2] + uu * wcu[2:3]
    return yg * _sigmoid(yg) * yu


SUBLANES = 8
PERM_GROUP = 512
PERM_Q = PERM_GROUP // SUBLANES


def _ffn_in_kernel(x_ref, wg_ref, wu_ref, wcg_ref, wcu_ref, bcg_ref, bcu_ref,
                   act_ref, cg_ref, cu_ref, wb_ref, car_ref, *, tps, n_sub):
    j = pl.program_id(0)
    i = pl.program_id(1)
    nj = pl.num_programs(0) - 1
    tm = x_ref.shape[0]
    tn = wg_ref.shape[1]
    G = PERM_GROUP
    assert tm == n_sub * G
    _stage_weight(j, i, nj, wg_ref, wb_ref, 0)
    _stage_weight(j, i, nj, wu_ref, wb_ref, tn)

    @pl.when(j > 0)
    def _():
        @pl.when(i % tps == 0)
        def _():
            car_ref[...] = jnp.zeros_like(car_ref)

        w = wb_ref[(j + 1) % 2]
        wcg, wcu, bcg, bcu = wcg_ref[...], wcu_ref[...], bcg_ref[...], bcu_ref[...]
        car = car_ref[...]
        first = lax.broadcasted_iota(jnp.int32, (SUBLANES, 2 * tn), 0) == 0
        for r in range(n_sub):
            rows = slice(r * G, (r + 1) * G)
            u = jnp.dot(x_ref[rows, :], w, preferred_element_type=F32)
            fix1 = jnp.where(first, car[1:2], pltpu.roll(u[G - 8:G], 1, axis=0))
            fix2 = jnp.where(first, car[0:1], pltpu.roll(u[G - 16:G - 8], 1, axis=0))
            p1 = jnp.concatenate([fix1, u[:G - 8]], axis=0)
            p2 = jnp.concatenate([fix2, fix1, u[:G - 16]], axis=0)
            act_ref[rows, :] = _conv_gate(u[:, :tn], u[:, tn:], p1[:, :tn], p2[:, :tn], p1[:, tn:], p2[:, tn:],
                                          wcg, wcu, bcg, bcu).astype(act_ref.dtype)
            car = jnp.concatenate([u[G - 9:G - 8], u[G - 1:G]], axis=0)
        car_ref[...] = car
        cg_ref[...] = car[:, :tn]
        cu_ref[...] = car[:, tn:]


def ffn_in_prompt(x, w3, lead, w_conv, b_conv, *, B, T, tm, tn, n_sub):
    M, K = x.shape
    F = D_FF
    nj = F // tn
    ni = M // tm
    ck = K // ni
    assert ck * ni == K and ck % 16 == 0
    tps = T // tm
    wc = w_conv
    bc = b_conv.reshape(1, 2 * F)
    nxt = lambda j: jnp.minimum(j, nj - 1)
    act, cg, cu = pl.pallas_call(
        functools.partial(_ffn_in_kernel, tps=tps, n_sub=n_sub),
        grid=(nj + 1, ni),
        in_specs=[
            pl.BlockSpec((tm, K), lambda j, i: (_lag_row(j, i), 0)),
            pl.BlockSpec((None, ck, tn), lambda j, i: (lead, i, nxt(j))),
            pl.BlockSpec((None, ck, tn), lambda j, i: (lead, i, nxt(j) + nj)),
            pl.BlockSpec((3, tn), lambda j, i: (0, _lag_col(j))),
            pl.BlockSpec((3, tn), lambda j, i: (0, _lag_col(j) + nj)),
            pl.BlockSpec((1, tn), lambda j, i: (0, _lag_col(j))),
            pl.BlockSpec((1, tn), lambda j, i: (0, _lag_col(j) + nj)),
        ],
        out_specs=[
            pl.BlockSpec((tm, tn), lambda j, i: (_lag_row(j, i), _lag_col(j))),
            pl.BlockSpec((None, 2, tn), lambda j, i: (_lag_row(j, i) // tps, 0, _lag_col(j))),
            pl.BlockSpec((None, 2, tn), lambda j, i: (_lag_row(j, i) // tps, 0, _lag_col(j))),
        ],
        out_shape=[
            jax.ShapeDtypeStruct((M, F), BF),
            jax.ShapeDtypeStruct((B, 2, F), F32),
            jax.ShapeDtypeStruct((B, 2, F), F32),
        ],
        scratch_shapes=[pltpu.VMEM((2, K, 2 * tn), BF),
                        pltpu.VMEM((2, 2 * tn), F32)],
        compiler_params=_params(("arbitrary", "arbitrary")),
        name="ffn_in_p",
    )(x, w3, w3, wc, wc, bc, bc)
    return act, jnp.concatenate([cg, cu], axis=-1)


def _ffn_in_step_kernel(x_ref, wg_ref, wu_ref, wcg_ref, wcu_ref, bcg_ref, bcu_ref,
                        c0g_ref, c0u_ref, c1g_ref, c1u_ref, act_ref, ug_ref, uu_ref):
    x = x_ref[...]
    ug = jnp.dot(x, wg_ref[...].astype(BF), preferred_element_type=F32)
    uu = jnp.dot(x, wu_ref[...].astype(BF), preferred_element_type=F32)
    act_ref[...] = _conv_gate(ug, uu, c1g_ref[...], c0g_ref[...], c1u_ref[...], c0u_ref[...],
                              wcg_ref[...], wcu_ref[...], bcg_ref[...], bcu_ref[...]).astype(act_ref.dtype)
    ug_ref[...] = ug
    uu_ref[...] = uu


def ffn_in_step(x, w3, lead, w_conv, b_conv, cache, *, tn):
    Bs, K = x.shape
    F = D_FF
    nj = F // tn
    bc = b_conv.reshape(1, 2 * F)
    cflat = cache.reshape(Bs, 4 * F)
    vec = lambda off: pl.BlockSpec((Bs, tn), lambda j: (0, j + off * nj))
    act, ug, uu = pl.pallas_call(
        _ffn_in_step_kernel,
        grid=(nj,),
        in_specs=[
            pl.BlockSpec((Bs, K), lambda j: (0, 0)),
            pl.BlockSpec((None, K, tn), lambda j: (lead, 0, j)),
            pl.BlockSpec((None, K, tn), lambda j: (lead, 0, j + nj)),
            pl.BlockSpec((3, tn), lambda j: (0, j)),
            pl.BlockSpec((3, tn), lambda j: (0, j + nj)),
            pl.BlockSpec((1, tn), lambda j: (0, j)),
            pl.BlockSpec((1, tn), lambda j: (0, j + nj)),
            vec(0), vec(1), vec(2), vec(3),
        ],
        out_specs=[pl.BlockSpec((Bs, tn), lambda j: (0, j))] * 3,
        out_shape=[jax.ShapeDtypeStruct((Bs, F), BF),
                   jax.ShapeDtypeStruct((Bs, F), F32),
                   jax.ShapeDtypeStruct((Bs, F), F32)],
        compiler_params=_params(("arbitrary",)),
        name="ffn_in_s",
    )(x, w3, w3, w_conv, w_conv, bc, bc, cflat, cflat, cflat, cflat)
    new_cache = jnp.stack([cache[:, 1, :], jnp.concatenate([ug, uu], axis=-1)], axis=1)
    return act, new_cache


def _mlstm_chunk_kernel(q_ref, k_ref, v_ref, o_ref, g_ref, gt_ref, gh_ref,
                        hs_ref, C_ref, n_ref, m_ref, *, hb):
    hblk = pl.program_id(1)
    c = pl.program_id(2)
    L = q_ref.shape[0]

    @pl.when(c == 0)
    def _():
        C_ref[...] = jnp.zeros_like(C_ref)
        n_ref[...] = jnp.zeros_like(n_ref)
        m_ref[...] = jnp.zeros_like(m_ref)

    g = g_ref[...]
    gt = gt_ref[...]
    lane = lax.broadcasted_iota(jnp.int32, g.shape, 1)
    sub = lax.broadcasted_iota(jnp.int32, gt.shape, 0)
    t_idx = lax.broadcasted_iota(jnp.int32, (L, L), 0)
    s_idx = lax.broadcasted_iota(jnp.int32, (L, L), 1)
    causal = s_idx <= t_idx
    q_scale = jnp.asarray(M_QK ** -0.5, BF)

    for hh_i in range(hb):
        h = hblk * hb + hh_i
        ic_col = jnp.sum(jnp.where(lane == h, g, 0.0), axis=1, keepdims=True)
        fc_col = jnp.sum(jnp.where(lane == h + M_HEADS, g, 0.0), axis=1, keepdims=True)
        ic_row = jnp.sum(jnp.where(sub == h, gt, 0.0), axis=0, keepdims=True)
        fc_row = jnp.sum(jnp.where(sub == h + M_HEADS, gt, 0.0), axis=0, keepdims=True)
        b_col = jnp.sum(jnp.where(causal, fc_row, 0.0), axis=1, keepdims=True)
        b_row = jnp.sum(jnp.where(t_idx <= s_idx, fc_col, 0.0), axis=0, keepdims=True)

        m_old = m_ref[hh_i][0:1, 0:1]
        d = jnp.where(causal, b_col - b_row + ic_row, -jnp.inf)
        inter = b_col + m_old
        mt = jnp.maximum(inter, jnp.max(d, axis=1, keepdims=True))
        w = jnp.exp(d - mt)
        si = jnp.exp(inter - mt)

        qb = q_ref[:, hh_i * M_QK:(hh_i + 1) * M_QK] * q_scale
        kb = k_ref[:, hh_i * M_QK:(hh_i + 1) * M_QK]
        vb = v_ref[:, hh_i * M_V:(hh_i + 1) * M_V]
        C_old = C_ref[hh_i]
        n_old = n_ref[hh_i]
        qk = lax.dot_general(qb, kb, (((1,), (1,)), ((), ())), preferred_element_type=F32) * w
        num = (jnp.dot(qk.astype(BF), vb, preferred_element_type=F32)
               + si * jnp.dot(qb, C_old.astype(BF), preferred_element_type=F32))
        den = jnp.sum(qk, axis=1, keepdims=True) + si * jnp.sum(qb.astype(F32) * n_old, axis=1, keepdims=True)
        hh = num / jnp.maximum(jnp.abs(den), jnp.exp(-mt))

        b_last = b_col[L - 1:L, :]
        gl = b_last - b_col + ic_col
        m_new = jnp.maximum(b_last + m_old, jnp.max(gl, axis=0, keepdims=True))
        wl = jnp.exp(gl - m_new)
        sd = jnp.exp(b_last + m_old - m_new)
        kw = kb.astype(F32) * wl
        C_ref[hh_i] = sd * C_old + jnp.dot(kw.T.astype(BF), vb, preferred_element_type=F32)
        n_ref[hh_i] = sd * n_old + jnp.sum(kw, axis=0, keepdims=True)
        m_ref[hh_i] = jnp.broadcast_to(m_new, (1, LANES))

        hn = hh * lax.rsqrt(jnp.mean(hh * hh, axis=1, keepdims=True) + EPS)
        o = o_ref[:, hh_i * M_V:(hh_i + 1) * M_V].astype(F32)
        hs_ref[:, hh_i * M_V:(hh_i + 1) * M_V] = (
            hn * gh_ref[:, hh_i * M_V:(hh_i + 1) * M_V] * _sigmoid(o)).astype(hs_ref.dtype)


def mlstm_prompt(z, gates, g_head, *, B, T, L, hb):
    M = B * T
    nc = T // L
    H = M_HEADS
    gt = jnp.swapaxes(gates[:, :2 * H].reshape(B * nc, L, 2 * H), 1, 2)
    row = lambda b, h, c: b * nc + c
    nhb = H // hb
    kq = nhb
    kv = nhb
    ko = 2 * nhb
    hs, C, n, m = pl.pallas_call(
        functools.partial(_mlstm_chunk_kernel, hb=hb),
        grid=(B, nhb, nc),
        in_specs=[
            pl.BlockSpec((L, hb * M_QK), lambda b, h, c: (row(b, h, c), h)),
            pl.BlockSpec((L, hb * M_QK), lambda b, h, c: (row(b, h, c), kq + h)),
            pl.BlockSpec((L, hb * M_V), lambda b, h, c: (row(b, h, c), kv + h)),
            pl.BlockSpec((L, hb * M_V), lambda b, h, c: (row(b, h, c), ko + h)),
            pl.BlockSpec((L, LANES), lambda b, h, c: (row(b, h, c), 0)),
            pl.BlockSpec((None, 2 * H, L), lambda b, h, c: (row(b, h, c), 0, 0)),
            pl.BlockSpec((1, hb * M_V), lambda b, h, c: (0, h)),
        ],
        out_specs=[
            pl.BlockSpec((L, hb * M_V), lambda b, h, c: (row(b, h, c), h)),
            pl.BlockSpec((None, hb, M_QK, M_V), lambda b, h, c: (b, h, 0, 0)),
            pl.BlockSpec((None, hb, 1, M_QK), lambda b, h, c: (b, h, 0, 0)),
            pl.BlockSpec((None, hb, 1, LANES), lambda b, h, c: (b, h, 0, 0)),
        ],
        out_shape=[
            jax.ShapeDtypeStruct((M, D_MODEL), BF),
            jax.ShapeDtypeStruct((B, M_HEADS, M_QK, M_V), F32),
            jax.ShapeDtypeStruct((B, M_HEADS, 1, M_QK), F32),
            jax.ShapeDtypeStruct((B, M_HEADS, 1, LANES), F32),
        ],
        compiler_params=_params(("arbitrary", "arbitrary", "arbitrary")),
        name="mlstm_p",
    )(z, z, z, z, gates, gt, g_head.reshape(1, D_MODEL))
    return hs, C, n[:, :, 0, :], m[:, :, 0, 0]


def _mlstm_step_kernel(q_ref, k_ref, v_ref, o_ref, ig_ref, lf_ref, m0_ref, n0_ref, C0_ref, gh_ref,
                       hs_ref, C_ref, n_ref, m_ref):
    ic = ig_ref[...]
    fc = lf_ref[...]
    m_old = m0_ref[...]
    inter = fc + m_old
    mt = jnp.maximum(inter, ic)
    w = jnp.exp(ic - mt)
    si = jnp.exp(inter - mt)
    q = (q_ref[...] * (M_QK ** -0.5)).astype(BF).astype(F32)
    k = k_ref[...]
    v = v_ref[...]
    n_old = n0_ref[...]
    qk = jnp.sum(q * k.astype(BF).astype(F32), axis=1, keepdims=True) * w
    wl = jnp.exp(ic - mt)
    sd = jnp.exp(inter - mt)
    kw = k * wl
    qT = q.T
    kwT = kw.T
    rows = []
    for h in range(M_HEADS):
        C_old = C0_ref[h]
        rows.append(jnp.sum(qT[:, h:h + 1] * C_old, axis=0, keepdims=True))
        C_ref[h] = sd[h:h + 1, :] * C_old + kwT[:, h:h + 1] * v[h:h + 1, :]
    qC = jnp.concatenate(rows, axis=0)
    num = qk.astype(BF).astype(F32) * v.astype(BF).astype(F32) + si * qC
    den = qk + si * jnp.sum(q * n_old, axis=1, keepdims=True)
    hh = num / jnp.maximum(jnp.abs(den), jnp.exp(-mt))
    n_ref[...] = sd * n_old + kw
    m_ref[...] = mt
    hn = hh * lax.rsqrt(jnp.mean(hh * hh, axis=1, keepdims=True) + EPS)
    hs_ref[...] = (hn * gh_ref[...] * _sigmoid(o_ref[...])).astype(hs_ref.dtype)


def mlstm_step(z, gates, g_head, C0, n0, m0):
    Bs = z.shape[0]
    H = M_HEADS
    qd, vd = H * M_QK, H * M_V
    q = z[:, :qd].reshape(Bs, H, M_QK)
    k = z[:, qd:2 * qd].reshape(Bs, H, M_QK)
    v = z[:, 2 * qd:2 * qd + vd].reshape(Bs, H, M_V)
    o = z[:, 2 * qd + vd:].reshape(Bs, H, M_V)
    ig = gates[:, :H].reshape(Bs, H, 1)
    lf = gates[:, H:2 * H].reshape(Bs, H, 1)
    per_b = lambda *tail: pl.BlockSpec((None,) + tail, lambda b: (b,) + (0,) * len(tail))
    hs, C, n, m = pl.pallas_call(
        _mlstm_step_kernel,
        grid=(Bs,),
        in_specs=[
            per_b(H, M_QK), per_b(H, M_QK), per_b(H, M_V), per_b(H, M_V),
            per_b(H, 1), per_b(H, 1), per_b(H, 1),
            pl.BlockSpec((None, None, H, M_QK), lambda b: (0, b, 0, 0)),
            pl.BlockSpec((None, None, H, M_QK, M_V), lambda b: (0, b, 0, 0, 0)),
            pl.BlockSpec((H, M_V), lambda b: (0, 0)),
        ],
        out_specs=[
            per_b(H, M_V),
            pl.BlockSpec((None, None, H, M_QK, M_V), lambda b: (0, b, 0, 0, 0)),
            pl.BlockSpec((None, None, H, M_QK), lambda b: (0, b, 0, 0)),
            per_b(H, 1),
        ],
        out_shape=[
            jax.ShapeDtypeStruct((Bs, H, M_V), BF),
            jax.ShapeDtypeStruct(C0.shape, F32),
            jax.ShapeDtypeStruct(n0.shape, F32),
            jax.ShapeDtypeStruct((Bs, H, 1), F32),
        ],
        compiler_params=_params(("arbitrary",)),
        name="mlstm_s",
    )(q, k, v, o, ig, lf, m0.reshape(Bs, H, 1), n0, C0, g_head.reshape(H, M_V))
    return hs.reshape(Bs, D_MODEL), C, n, m.reshape(1, Bs, H)


def _attn_kernel(sink_ref, q_ref, kc_ref, kp_ref, vc_ref, vp_ref, o_ref):
    n = pl.program_id(1)
    W = WINDOW
    R = GROUP * W
    kp = kp_ref[...].astype(BF)
    kc = kc_ref[...].astype(BF)
    vpT = vp_ref[...].T.astype(BF)
    vcT = vc_ref[...].T.astype(BF)
    j_idx = lax.broadcasted_iota(jnp.int32, (W, R), 0)
    i_idx = lax.broadcasted_iota(jnp.int32, (W, R), 1) % W
    from_prev = j_idx > i_idx
    lane_g = lax.broadcasted_iota(jnp.int32, (1, R), 1) // W
    no_prev = jnp.where(n > 0, 0.0, -jnp.inf)
    nt = (((1,), (1,)), ((), ()))
    for h in range(KV_HEADS):
        hs = slice(h * HEAD_DIM, (h + 1) * HEAD_DIM)
        qh = jnp.concatenate(
            [q_ref[:, (h * GROUP + g) * HEAD_DIM:(h * GROUP + g + 1) * HEAD_DIM] for g in range(GROUP)], axis=0)
        sp = lax.dot_general(kp[:, hs], qh, nt, preferred_element_type=F32)
        sc = lax.dot_general(kc[:, hs], qh, nt, preferred_element_type=F32)
        s = jnp.where(from_prev, sp + no_prev, sc)
        sk = jnp.zeros((1, R), F32)
        for g in range(GROUP):
            sk = jnp.where(lane_g == g, sink_ref[h * GROUP + g], sk)
        mx = jnp.maximum(jnp.max(s, axis=0, keepdims=True), sk)
        p = jnp.exp(s - mx)
        p = p / (jnp.sum(p, axis=0, keepdims=True) + jnp.exp(sk - mx))
        oT = (jnp.dot(vpT[hs, :], jnp.where(from_prev, p, 0.0).astype(BF), preferred_element_type=F32)
              + jnp.dot(vcT[hs, :], jnp.where(from_prev, 0.0, p).astype(BF), preferred_element_type=F32))
        for g in range(GROUP):
            c0 = (h * GROUP + g) * HEAD_DIM
            o_ref[:, c0:c0 + HEAD_DIM] = oT[:, g * W:(g + 1) * W].T.astype(o_ref.dtype)


def attention_prompt(q, k, v, sinks, *, B, T):
    M = B * T
    W = WINDOW
    nb = T // W
    kvd = KV_HEADS * HEAD_DIM
    cur = lambda b, n: (b * nb + n, 0)
    prev = lambda b, n: (b * nb + jnp.maximum(n - 1, 0), 0)
    return pl.pallas_call(
        _attn_kernel,
        grid=(B, nb),
        in_specs=[
            pl.BlockSpec(memory_space=pltpu.SMEM),
            pl.BlockSpec((W, D_MODEL), cur),
            pl.BlockSpec((W, kvd), cur), pl.BlockSpec((W, kvd), prev),
            pl.BlockSpec((W, kvd), cur), pl.BlockSpec((W, kvd), prev),
        ],
        out_specs=pl.BlockSpec((W, D_MODEL), cur),
        out_shape=jax.ShapeDtypeStruct((M, D_MODEL), BF),
        compiler_params=_params(("arbitrary", "arbitrary")),
        name="attn_p",
    )(sinks, q, k, k, v, v)


def _per_head_rows(row):
    return jnp.concatenate(
        [jnp.broadcast_to(row[:, h * HEAD_DIM:(h + 1) * HEAD_DIM], (GROUP, HEAD_DIM)) for h in range(KV_HEADS)],
        axis=0)


def _attn_step_kernel(q_ref, kc_ref, vc_ref, kn_ref, vn_ref, sk_ref, o_ref, ko_ref, vo_ref, *, bb):
    W = WINDOW
    jl = lax.broadcasted_iota(jnp.int32, (A_HEADS, W), 1)
    sk = sk_ref[...]
    heads = lambda a, h: a[h * GROUP:(h + 1) * GROUP, :]
    scores = []
    for b in range(bb):
        q = q_ref[b]
        kcb = kc_ref[b].astype(BF)
        s = jnp.concatenate(
            [lax.dot_general(heads(q, h).astype(BF), kcb[:, h * HEAD_DIM:(h + 1) * HEAD_DIM],
                             (((1,), (1,)), ((), ())), preferred_element_type=F32) for h in range(KV_HEADS)],
            axis=0)
        kne = _per_head_rows(kn_ref[b]).astype(BF).astype(F32)
        sn = jnp.sum(q.astype(BF).astype(F32) * kne, axis=1, keepdims=True)
        scores.append((s, sn))
    probs = []
    for s, sn in scores:
        s = jnp.where(jl >= 1, s, -jnp.inf)
        mx = jnp.maximum(jnp.maximum(jnp.max(s, axis=1, keepdims=True), sn), sk)
        p = jnp.exp(s - mx)
        pn = jnp.exp(sn - mx)
        den = jnp.sum(p, axis=1, keepdims=True) + pn + jnp.exp(sk - mx)
        probs.append((p / den, pn / den))
    for b in range(bb):
        p, pn = probs[b]
        vcb = vc_ref[b].astype(BF)
        o = jnp.concatenate(
            [jnp.dot(heads(p, h).astype(BF), vcb[:, h * HEAD_DIM:(h + 1) * HEAD_DIM],
                     preferred_element_type=F32) for h in range(KV_HEADS)], axis=0)
        vne = _per_head_rows(vn_ref[b]).astype(BF).astype(F32)
        o_ref[b] = o + pn.astype(BF).astype(F32) * vne
        ko_ref[b, 0:W - 1, :] = kc_ref[b, 1:W, :]
        ko_ref[b, W - 1:W, :] = kn_ref[b]
        vo_ref[b, 0:W - 1, :] = vc_ref[b, 1:W, :]
        vo_ref[b, W - 1:W, :] = vn_ref[b]


def attention_step(q, k_new, v_new, kbuf, vbuf, sinks):
    Bs = q.shape[0]
    W = WINDOW
    kvd = KV_HEADS * HEAD_DIM
    bb = 4
    assert Bs % bb == 0
    per_b = lambda *tail: pl.BlockSpec((bb,) + tail, lambda b: (b,) + (0,) * len(tail))
    o, ko, vo = pl.pallas_call(
        functools.partial(_attn_step_kernel, bb=bb),
        grid=(Bs // bb,),
        in_specs=[per_b(A_HEADS, HEAD_DIM), per_b(W, kvd), per_b(W, kvd), per_b(1, kvd), per_b(1, kvd),
                  pl.BlockSpec((A_HEADS, 1), lambda b: (0, 0))],
        out_specs=[per_b(A_HEADS, HEAD_DIM), per_b(W, kvd), per_b(W, kvd)],
        out_shape=[jax.ShapeDtypeStruct((Bs, A_HEADS, HEAD_DIM), F32),
                   jax.ShapeDtypeStruct((Bs, W, kvd), F32),
                   jax.ShapeDtypeStruct((Bs, W, kvd), F32)],
        compiler_params=_params(("arbitrary",)),
        name="attn_s",
    )(q.reshape(Bs, A_HEADS, HEAD_DIM), kbuf.reshape(Bs, W, kvd), vbuf.reshape(Bs, W, kvd),
      k_new.reshape(Bs, 1, kvd), v_new.reshape(Bs, 1, kvd), sinks.reshape(A_HEADS, 1))
    return (o.reshape(Bs, D_MODEL), ko.reshape(Bs, W, KV_HEADS, HEAD_DIM), vo.reshape(Bs, W, KV_HEADS, HEAD_DIM))


def _rope_tables(pos, width):
    half = HEAD_DIM // 2
    freq = ROPE_THETA ** (-jnp.arange(half, dtype=F32) / half)
    ang = pos.astype(F32)[:, None] * freq[None, :]
    cos = jnp.cos(ang)
    sin = jnp.sin(ang)
    reps = width // HEAD_DIM
    return (jnp.tile(jnp.concatenate([cos, cos], axis=1), (1, reps)),
            jnp.tile(jnp.concatenate([-sin, sin], axis=1), (1, reps)))


def _tiles(T):
    if T > 1:
        return dict(tm=1024, tn=1024, n_sub=2,
                    tm_r=512, tn_r=1024, n_sub_r=1,
                    tn_kv=512, tn_ffi=256, n_sub_ffi=2,
                    tm_ffo=512, tn_ffo=512, n_sub_ffo=1, tm_mod=256)
    return dict(tm=None, tn=512, n_sub=1, tm_r=None, tn_r=512, n_sub_r=1, tn_kv=512,
                tn_ffi=256, n_sub_ffi=1, tm_ffo=None, tn_ffo=256, n_sub_ffo=1, tm_mod=None)


def _run(x, ada, pos, state, P, *, B, T):
    D = D_MODEL
    M = B * T
    c = {k: (M if v is None else v) for k, v in _tiles(T).items()}
    tm, tn, tm_r, tn_r, tm_mod = c["tm"], c["tn"], c["tm_r"], c["tn_r"], c["tm_mod"]
    cos, sin = _rope_tables(pos, LANES)
    if T > 1:
        tps = T // tm
        rope_spec = pl.BlockSpec((tm, LANES), lambda j, i: (i % tps, 0))
    else:
        rope_spec = pl.BlockSpec((1, LANES), lambda j, i: (0, 0))

    out = {}
    tag = "_p" if T > 1 else "_s"

    sh1, sc1, ga1, sh2, sc2, ga2 = ada[0]
    hn = modulate(x, P["g_norm1"][0], sh1, sc1, T=T, tm=tm_mod)
    qd, vd = M_HEADS * M_QK, M_HEADS * M_V
    w_inT = jnp.swapaxes(P["w_m_in"], 1, 2)
    z = linear(hn, w_inT, 0, name="m_in" + tag, n_cols=2 * qd + 2 * vd, tm=tm, tn=tn, n_sub=c["n_sub"],
               epilogue=_ep_plain, transposed=True, out_dtype=BF if T > 1 else F32)
    wgT = jnp.pad(w_inT[:, 2 * qd + 2 * vd:, :], ((0, 0), (0, LANES - 2 * M_HEADS), (0, 0)))
    bg = jnp.pad(jnp.concatenate([P["b_m_i"][0], P["b_m_f"][0]]), (0, LANES - 2 * M_HEADS))
    gates = linear(hn, wgT, 0, name="m_gates" + tag, n_cols=LANES, tm=tm, tn=LANES, epilogue=_ep_gates,
                   extras=[_row_vec(bg, LANES)], transposed=True)
    if T > 1:
        hs, C, n, m = mlstm_prompt(z, gates, P["g_m_head"][0], B=B, T=T, L=256, hb=2)
        out["C"], out["n"], out["m"] = C[None], n[None], m[None]
    else:
        hs, out["C"], out["n"], out["m"] = mlstm_step(z, gates, P["g_m_head"][0], *state["mlstm"])
    x = linear(hs, P["w_m_out"], 0, name="m_out" + tag, n_cols=D, tm=tm_r, tn=tn_r, n_sub=c["n_sub_r"],
               epilogue=_ep_resid, extras=[_tile(x, tn_r, T, tm_r), _seq_vec(ga1, tn_r, T, tm_r)])

    convs = []
    for l in range(2):
        if l == 1:
            sh1, sc1, ga1, sh2, sc2, ga2 = ada[1]
            shk, sck = ada[2]
            hk = modulate(x, P["g_kv"], shk, sck, T=T, tm=tm_mod)
            kvd = KV_HEADS * HEAD_DIM
            tn_kv = c["tn_kv"]
            k = linear(hk, P["w_kv"], 0, name="kv_k" + tag, n_cols=kvd, tm=tm, tn=tn_kv, n_sub=c["n_sub"],
                       epilogue=functools.partial(_ep_rope, scale=1.0),
                       extras=[_row_vec(P["b_kv"][:kvd], tn_kv), (cos, rope_spec), (sin, rope_spec)])
            v = linear(hk, P["w_kv"], 0, name="kv_v" + tag, n_cols=kvd, tm=tm, tn=tn_kv, n_sub=c["n_sub"],
                       col_off=kvd // tn_kv, epilogue=_ep_bias, extras=[_row_vec(P["b_kv"][kvd:], tn_kv)])
            hn = modulate(x, P["g_norm1"][1], sh1, sc1, T=T, tm=tm_mod)
            q = linear(hn, P["w_q"], 0, name="attn_q" + tag, n_cols=D, tm=tm, tn=tn, n_sub=c["n_sub"],
                       epilogue=functools.partial(_ep_rope, scale=HEAD_DIM ** -0.5),
                       extras=[_row_vec(P["b_q"][0], tn), (cos, rope_spec), (sin, rope_spec)],
                       out_dtype=BF if T > 1 else F32)
            if T > 1:
                o = attention_prompt(q, k, v, P["sinks"][0], B=B, T=T)
                last = lambda a: a.reshape(B, T, kvd)[:, T - WINDOW:, :].reshape(B, WINDOW, KV_HEADS, HEAD_DIM)
                out["k_win"], out["v_win"] = last(k), last(v)
            else:
                o, out["k_win"], out["v_win"] = attention_step(q, k, v, state["kbuf"], state["vbuf"], P["sinks"][0])
            x = linear(o, P["w_o"], 0, name="attn_o" + tag, n_cols=D, tm=tm_r, tn=tn_r, n_sub=c["n_sub_r"],
                       epilogue=_ep_resid_bias,
                       extras=[_row_vec(P["b_o"][0], tn_r), _tile(x, tn_r, T, tm_r), _seq_vec(ga1, tn_r, T, tm_r)])

        il = T > 1
        hn = modulate(x, P["g_norm2"][l], sh2, sc2, T=T, tm=PERM_GROUP if il else tm_mod, interleave=il)
        if T > 1:
            act, cb = ffn_in_prompt(hn, P["w_ffn_in"], l, P["w_conv"][l], P["b_conv"][l], B=B, T=T, tm=tm,
                                    tn=c["tn_ffi"], n_sub=c["n_sub_ffi"])
        else:
            act, cb = ffn_in_step(hn, P["w_ffn_in"], l, P["w_conv"][l], P["b_conv"][l], state["conv"][l],
                                  tn=c["tn_ffi"])
        convs.append(cb)
        tm_o, tn_o = c["tm_ffo"], c["tn_ffo"]
        x = linear(act, P["w_ffn_out"], l, name="ffn_out" + tag, n_cols=D, tm=tm_o, tn=tn_o, n_sub=c["n_sub_ffo"],
                   epilogue=_ep_resid, extras=[_tile(x, tn_o, T, tm_o), _seq_vec(ga2, tn_o, T, tm_o)],
                   rows_interleaved=il)

    out["conv"] = jnp.stack(convs)
    out["y"] = rmsnorm(x, P["g_final"], tm=tm_mod)
    return out


def kernel(x_prompt, x_sample, state_mlstm_C, state_mlstm_n, state_mlstm_m, cache_conv, cache_k_win, cache_v_win, c_prompt, c_sample, w_ada, g_norm1, g_norm2, w_m_in, b_m_i, b_m_f, g_m_head, w_m_out, w_ada_kv, g_kv, w_kv, b_kv, w_q, b_q, sinks, w_o, b_o, w_ffn_in, w_conv, b_conv, w_ffn_out, g_final):
    D = D_MODEL
    Bp, Tp, _ = x_prompt.shape
    Bs, Ts, _ = x_sample.shape
    assert Ts == 1
    P = dict(g_norm1=g_norm1, g_norm2=g_norm2, w_m_in=w_m_in, b_m_i=b_m_i, b_m_f=b_m_f, g_m_head=g_m_head,
             w_m_out=w_m_out, g_kv=g_kv, w_kv=w_kv[None], b_kv=b_kv, w_q=w_q, b_q=b_q, sinks=sinks, w_o=w_o,
             b_o=b_o, w_ffn_in=w_ffn_in, w_conv=w_conv, b_conv=b_conv, w_ffn_out=w_ffn_out, g_final=g_final)

    n_c = Bp + Bs
    pad_c = -n_c % 16
    cs = silu_cast(jnp.concatenate([c_prompt, c_sample, jnp.zeros((pad_c, D), F32)], axis=0))
    rows_c = n_c + pad_c
    ada_all = []
    for l in range(2):
        a = linear(cs, w_ada, l, name="ada", n_cols=6 * D, tm=rows_c, tn=512, epilogue=_ep_plain)
        ada_all.append(a)
    a_kv = linear(cs, w_ada_kv[None], 0, name="ada_kv", n_cols=2 * D, tm=rows_c, tn=512, epilogue=_ep_plain)

    def split(lo, hi):
        per_layer = [tuple(a[lo:hi, i * D:(i + 1) * D] for i in range(6)) for a in ada_all]
        return per_layer + [(a_kv[lo:hi, :D], a_kv[lo:hi, D:])]

    po = _run(x_prompt.reshape(Bp * Tp, D), split(0, Bp), jnp.arange(Tp, dtype=jnp.int32), None, P, B=Bp, T=Tp)
    state = dict(mlstm=(state_mlstm_C, state_mlstm_n, state_mlstm_m), conv=cache_conv,
                 kbuf=cache_k_win, vbuf=cache_v_win)
    so = _run(x_sample.reshape(Bs, D), split(Bp, Bp + Bs), PAST_LEN + jnp.arange(1, dtype=jnp.int32), state, P,
              B=Bs, T=1)
    return (po["y"].reshape(Bp, Tp, D), so["y"].reshape(Bs, 1, D),
            po["C"], po["n"], po["m"], po["conv"], po["k_win"], po["v_win"],
            so["C"], so["n"], so["m"], so["conv"], so["k_win"], so["v_win"])
```

```python
import functools

import jax
import jax.numpy as jnp
from jax import lax
from jax.experimental import pallas as pl
from jax.experimental.pallas import tpu as pltpu

BF = jnp.bfloat16
F32 = jnp.float32

D_MODEL = 4096
M_HEADS = 8
M_QK = 256
M_V = 512
M_CHUNK = 64
GATE_CAP = 15.0
HEAD_DIM = 64
A_HEADS = 64
KV_HEADS = 8
GROUP = 8
WINDOW = 128
ROPE_THETA = 10000.0
D_FF = 11008
PAST_LEN = 16384
EPS = 1e-6

VMEM_LIMIT_V7X = 58 * 1024 * 1024
LANES = 128


def _params(sem):
    return pltpu.CompilerParams(dimension_semantics=sem, vmem_limit_bytes=VMEM_LIMIT_V7X)


def _sigmoid(x):
    return 1.0 / (1.0 + jnp.exp(-x))


def _lag_row(j, i):
    return i * jnp.minimum(j, 1)


def _lag_col(j):
    return jnp.maximum(j - 1, 0)


def _lagged(index_map):
    return lambda j, i: index_map(_lag_col(j), _lag_row(j, i))


def _chunk(j, i, nj, ni):
    last = j // nj
    return jnp.minimum(j, nj - 1), i + last * (ni - 1 - i)


def _stage_weight(j, i, nj, w_ref, wb_ref, col0=0, wq_ref=None):
    ck, cw = w_ref.shape

    @pl.when(j < nj)
    def _():
        wq = w_ref[...].astype(BF)
        wb_ref[j % 2, pl.ds(pl.multiple_of(i * ck, ck), ck), col0:col0 + cw] = wq
        if wq_ref is not None:
            wq_ref[...] = wq


def _linear_kernel(x_ref, w_ref, *rest, n_extra, epilogue, n_sub, transposed, rows_interleaved, keep_bf16):
    extra = rest[:n_extra]
    o_ref = rest[n_extra]
    wq_ref = rest[n_extra + 1] if keep_bf16 else None
    wb_ref = rest[-1]
    j = pl.program_id(0)
    i = pl.program_id(1)
    nj = pl.num_programs(0) - 1
    tm = x_ref.shape[0]
    sub = tm // n_sub
    _stage_weight(j, i, nj, w_ref, wb_ref, wq_ref=wq_ref)

    @pl.when(j > 0)
    def _():
        w = wb_ref[(j + 1) % 2]
        for r in range(n_sub):
            rows = slice(r * sub, (r + 1) * sub)
            xs = x_ref[rows, :].astype(BF)
            if transposed:
                acc = lax.dot_general(xs, w, (((1,), (1,)), ((), ())), preferred_element_type=F32)
            else:
                acc = jnp.dot(xs, w, preferred_element_type=F32)
            if rows_interleaved:
                assert sub == PERM_GROUP
                acc = pltpu.einshape("qsd->sqd", acc.reshape(PERM_Q, SUBLANES, acc.shape[1])).reshape(acc.shape)
            ex = [e[rows, :] if e.shape[0] == tm else e[...] for e in extra]
            o_ref[rows, :] = epilogue(acc, *ex).astype(o_ref.dtype)


def linear(x, w3, lead, *, name, n_cols, tm, tn, epilogue, extras=(), out_dtype=F32,
           col_off=0, n_sub=1, transposed=False, rows_interleaved=False, keep_bf16=False):
    M, K = x.shape
    assert M % tm == 0 and n_cols % tn == 0 and tm % n_sub == 0
    nj, ni = n_cols // tn, M // tm
    chunk = lambda j, i: _chunk(j, i, nj, ni)
    if transposed:
        cn = tn // ni
        assert cn * ni == tn and cn % 16 == 0
        w_spec = pl.BlockSpec((None, cn, K),
                              lambda j, i: (lead, (chunk(j, i)[0] + col_off) * ni + chunk(j, i)[1], 0))
        wq_spec = pl.BlockSpec((None, cn, K), lambda j, i: (0, chunk(j, i)[0] * ni + chunk(j, i)[1], 0))
        wq_shape = (1, n_cols, K)
        wb_shape = (2, tn, K)
    else:
        ck = K // ni
        assert ck * ni == K and ck % 16 == 0
        w_spec = pl.BlockSpec((None, ck, tn), lambda j, i: (lead, chunk(j, i)[1], chunk(j, i)[0] + col_off))
        wq_spec = pl.BlockSpec((None, ck, tn), lambda j, i: (0, chunk(j, i)[1], chunk(j, i)[0]))
        wq_shape = (1, K, n_cols)
        wb_shape = (2, K, tn)
    in_specs = [pl.BlockSpec((tm, K), lambda j, i: (_lag_row(j, i), 0)), w_spec]
    in_specs += [pl.BlockSpec(s.block_shape, _lagged(s.index_map)) for _, s in extras]
    out_specs = [pl.BlockSpec((tm, tn), lambda j, i: (_lag_row(j, i), _lag_col(j)))]
    out_shape = [jax.ShapeDtypeStruct((M, n_cols), out_dtype)]
    if keep_bf16:
        out_specs.append(wq_spec)
        out_shape.append(jax.ShapeDtypeStruct(wq_shape, BF))
    res = pl.pallas_call(
        functools.partial(_linear_kernel, n_extra=len(extras), epilogue=epilogue, n_sub=n_sub,
                          transposed=transposed, rows_interleaved=rows_interleaved, keep_bf16=keep_bf16),
        grid=(nj + 1, ni),
        in_specs=in_specs,
        out_specs=out_specs,
        out_shape=out_shape,
        scratch_shapes=[pltpu.VMEM(wb_shape, BF)],
        compiler_params=_params(("arbitrary", "arbitrary")),
        name=name,
    )(x, w3, *[a for a, _ in extras])
    return tuple(res) if keep_bf16 else res[0]


def _row_vec(v, tn):
    return v.reshape(1, -1), pl.BlockSpec((1, tn), lambda j, i: (0, j))


def _seq_vec(a, tn, T, tm):
    if T == 1:
        return a, pl.BlockSpec((tm, tn), lambda j, i: (i, j))
    tps = T // tm
    return a.reshape(a.shape[0], 1, a.shape[1]), pl.BlockSpec((None, 1, tn), lambda j, i: (i // tps, 0, j))


def _tile(a, tn, T, tm, col0=0):
    return a, pl.BlockSpec((tm, tn), lambda j, i: (i, j + col0))


def _ep_plain(acc):
    return acc


def _ep_bias(acc, b):
    return acc + b


def _ep_resid(acc, r, g):
    return r + g * acc


def _ep_resid_bias(acc, b, r, g):
    return r + g * (acc + b)


def _ep_rope(acc, b, cos, sin, *, scale):
    y = acc + b
    n = y.shape[1]
    cos = jnp.concatenate([cos] * (n // LANES), axis=1)
    sin = jnp.concatenate([sin] * (n // LANES), axis=1)
    lane = lax.broadcasted_iota(jnp.int32, y.shape, 1)
    first_half = (lane % HEAD_DIM) < (HEAD_DIM // 2)
    partner = jnp.where(first_half, pltpu.roll(y, n - HEAD_DIM // 2, axis=1), pltpu.roll(y, HEAD_DIM // 2, axis=1))
    return (y * cos + partner * sin) * scale


def _ep_gates(acc, b):
    z = GATE_CAP * jnp.tanh((acc + b) / GATE_CAP)
    logsig = jnp.minimum(z, 0.0) - jnp.log(1.0 + jnp.exp(-jnp.abs(z)))
    lane = lax.broadcasted_iota(jnp.int32, z.shape, 1)
    return jnp.where(lane < M_HEADS, z, logsig)


def _silu_kernel(c_ref, o_ref):
    c = c_ref[...]
    o_ref[...] = (c * _sigmoid(c)).astype(o_ref.dtype)


def silu_cast(c):
    return pl.pallas_call(
        _silu_kernel,
        out_shape=jax.ShapeDtypeStruct(c.shape, BF),
        name="silu_c",
    )(c)


def _modulate_kernel(x_ref, g_ref, sh_ref, sc_ref, o_ref):
    x = x_ref[...]
    y = x * lax.rsqrt(jnp.mean(x * x, axis=-1, keepdims=True) + EPS) * g_ref[...]
    o_ref[...] = (y * (1.0 + sc_ref[...]) + sh_ref[...]).astype(o_ref.dtype)


def _modulate_interleave_kernel(x_ref, g_ref, sh_ref, sc_ref, o_ref):
    x = x_ref[...]
    y = x * lax.rsqrt(jnp.mean(x * x, axis=-1, keepdims=True) + EPS) * g_ref[...]
    y = y * (1.0 + sc_ref[...]) + sh_ref[...]
    D = y.shape[1]
    y = pltpu.einshape("sqd->qsd", y.reshape(SUBLANES, PERM_Q, D)).reshape(PERM_GROUP, D)
    o_ref[...] = y.astype(o_ref.dtype)


def modulate(x, g, sh, sc, *, T, tm, interleave=False):
    M, D = x.shape
    assert not interleave or (tm == PERM_GROUP and T > 1)
    sh_a, sh_s = _seq_vec(sh, D, T, tm)
    sc_a, sc_s = _seq_vec(sc, D, T, tm)
    fix = lambda s: pl.BlockSpec(s.block_shape, functools.partial(lambda im, i: im(0, i), s.index_map))
    return pl.pallas_call(
        _modulate_interleave_kernel if interleave else _modulate_kernel,
        grid=(M // tm,),
        in_specs=[pl.BlockSpec((tm, D), lambda i: (i, 0)),
                  pl.BlockSpec((1, D), lambda i: (0, 0)),
                  fix(sh_s), fix(sc_s)],
        out_specs=pl.BlockSpec((tm, D), lambda i: (i, 0)),
        out_shape=jax.ShapeDtypeStruct((M, D), BF),
        compiler_params=_params(("arbitrary",)),
        name="modulate",
    )(x, g.reshape(1, D), sh_a, sc_a)


def _rmsnorm_kernel(x_ref, g_ref, o_ref):
    x = x_ref[...]
    o_ref[...] = x * lax.rsqrt(jnp.mean(x * x, axis=-1, keepdims=True) + EPS) * g_ref[...]


def rmsnorm(x, g, *, tm):
    M, D = x.shape
    return pl.pallas_call(
        _rmsnorm_kernel,
        grid=(M // tm,),
        in_specs=[pl.BlockSpec((tm, D), lambda i: (i, 0)), pl.BlockSpec((1, D), lambda i: (0, 0))],
        out_specs=pl.BlockSpec((tm, D), lambda i: (i, 0)),
        out_shape=jax.ShapeDtypeStruct((M, D), F32),
        compiler_params=_params(("arbitrary",)),
        name="final_norm",
    )(x, g.reshape(1, D))


def _conv_gate(ug, uu, pg1, pg2, pu1, pu2, wcg, wcu, bcg, bcu):
    yg = bcg + pg2 * wcg[0:1] + pg1 * wcg[1:2] + ug * wcg[2:3]
    yu = bcu + pu2 * wcu[0:1] + pu1 * wcu[1:2] + uu * wcu[2:3]
    return yg * _sigmoid(yg) * yu


SUBLANES = 8
PERM_GROUP = 512
PERM_Q = PERM_GROUP // SUBLANES


def _ffn_in_kernel(x_ref, wg_ref, wu_ref, wcg_ref, wcu_ref, bcg_ref, bcu_ref,
                   act_ref, cg_ref, cu_ref, wqg_ref, wqu_ref, wb_ref, car_ref, *, tps, n_sub):
    j = pl.program_id(0)
    i = pl.program_id(1)
    nj = pl.num_programs(0) - 1
    tm = x_ref.shape[0]
    tn = wg_ref.shape[1]
    G = PERM_GROUP
    assert tm == n_sub * G
    _stage_weight(j, i, nj, wg_ref, wb_ref, 0, wqg_ref)
    _stage_weight(j, i, nj, wu_ref, wb_ref, tn, wqu_ref)

    @pl.when(j > 0)
    def _():
        @pl.when(i % tps == 0)
        def _():
            car_ref[...] = jnp.zeros_like(car_ref)

        w = wb_ref[(j + 1) % 2]
        wcg, wcu, bcg, bcu = wcg_ref[...], wcu_ref[...], bcg_ref[...], bcu_ref[...]
        car = car_ref[...]
        first = lax.broadcasted_iota(jnp.int32, (SUBLANES, 2 * tn), 0) == 0
        for r in range(n_sub):
            rows = slice(r * G, (r + 1) * G)
            u = jnp.dot(x_ref[rows, :], w, preferred_element_type=F32)
            fix1 = jnp.where(first, car[1:2], pltpu.roll(u[G - 8:G], 1, axis=0))
            fix2 = jnp.where(first, car[0:1], pltpu.roll(u[G - 16:G - 8], 1, axis=0))
            p1 = jnp.concatenate([fix1, u[:G - 8]], axis=0)
            p2 = jnp.concatenate([fix2, fix1, u[:G - 16]], axis=0)
            act_ref[rows, :] = _conv_gate(u[:, :tn], u[:, tn:], p1[:, :tn], p2[:, :tn], p1[:, tn:], p2[:, tn:],
                                          wcg, wcu, bcg, bcu).astype(act_ref.dtype)
            car = jnp.concatenate([u[G - 9:G - 8], u[G - 1:G]], axis=0)
        car_ref[...] = car
        cg_ref[...] = car[:, :tn]
        cu_ref[...] = car[:, tn:]


def ffn_in_prompt(x, w3, lead, w_conv, b_conv, *, B, T, tm, tn, n_sub):
    M, K = x.shape
    F = D_FF
    nj = F // tn
    ni = M // tm
    ck = K // ni
    assert ck * ni == K and ck % 16 == 0
    tps = T // tm
    wc = w_conv
    bc = b_conv.reshape(1, 2 * F)
    chunk = lambda j, i: _chunk(j, i, nj, ni)
    wq_spec = pl.BlockSpec((None, ck, tn), lambda j, i: (0, chunk(j, i)[1], chunk(j, i)[0]))
    act, cg, cu, wq_g, wq_u = pl.pallas_call(
        functools.partial(_ffn_in_kernel, tps=tps, n_sub=n_sub),
        grid=(nj + 1, ni),
        in_specs=[
            pl.BlockSpec((tm, K), lambda j, i: (_lag_row(j, i), 0)),
            pl.BlockSpec((None, ck, tn), lambda j, i: (lead, chunk(j, i)[1], chunk(j, i)[0])),
            pl.BlockSpec((None, ck, tn), lambda j, i: (lead, chunk(j, i)[1], chunk(j, i)[0] + nj)),
            pl.BlockSpec((3, tn), lambda j, i: (0, _lag_col(j))),
            pl.BlockSpec((3, tn), lambda j, i: (0, _lag_col(j) + nj)),
            pl.BlockSpec((1, tn), lambda j, i: (0, _lag_col(j))),
            pl.BlockSpec((1, tn), lambda j, i: (0, _lag_col(j) + nj)),
        ],
        out_specs=[
            pl.BlockSpec((tm, tn), lambda j, i: (_lag_row(j, i), _lag_col(j))),
            pl.BlockSpec((None, 2, tn), lambda j, i: (_lag_row(j, i) // tps, 0, _lag_col(j))),
            pl.BlockSpec((None, 2, tn), lambda j, i: (_lag_row(j, i) // tps, 0, _lag_col(j))),
            wq_spec, wq_spec,
        ],
        out_shape=[
            jax.ShapeDtypeStruct((M, F), BF),
            jax.ShapeDtypeStruct((B, 2, F), F32),
            jax.ShapeDtypeStruct((B, 2, F), F32),
            jax.ShapeDtypeStruct((1, K, F), BF),
            jax.ShapeDtypeStruct((1, K, F), BF),
        ],
        scratch_shapes=[pltpu.VMEM((2, K, 2 * tn), BF),
                        pltpu.VMEM((2, 2 * tn), F32)],
        compiler_params=_params(("arbitrary", "arbitrary")),
        name="ffn_in_p",
    )(x, w3, w3, wc, wc, bc, bc)
    return act, jnp.concatenate([cg, cu], axis=-1), (wq_g, wq_u)


def _ffn_in_step_kernel(x_ref, wg_ref, wu_ref, wcg_ref, wcu_ref, bcg_ref, bcu_ref,
                        c0g_ref, c0u_ref, c1g_ref, c1u_ref, act_ref, ug_ref, uu_ref):
    x = x_ref[...]
    ug = jnp.dot(x, wg_ref[...].astype(BF), preferred_element_type=F32)
    uu = jnp.dot(x, wu_ref[...].astype(BF), preferred_element_type=F32)
    act_ref[...] = _conv_gate(ug, uu, c1g_ref[...], c0g_ref[...], c1u_ref[...], c0u_ref[...],
                              wcg_ref[...], wcu_ref[...], bcg_ref[...], bcu_ref[...]).astype(act_ref.dtype)
    ug_ref[...] = ug
    uu_ref[...] = uu


def ffn_in_step(x, wg3, wu3, w_conv, b_conv, cache, *, tn):
    Bs, K = x.shape
    F = D_FF
    nj = F // tn
    bc = b_conv.reshape(1, 2 * F)
    cflat = cache.reshape(Bs, 4 * F)
    vec = lambda off: pl.BlockSpec((Bs, tn), lambda j: (0, j + off * nj))
    act, ug, uu = pl.pallas_call(
        _ffn_in_step_kernel,
        grid=(nj,),
        in_specs=[
            pl.BlockSpec((Bs, K), lambda j: (0, 0)),
            pl.BlockSpec((None, K, tn), lambda j: (0, 0, j)),
            pl.BlockSpec((None, K, tn), lambda j: (0, 0, j)),
            pl.BlockSpec((3, tn), lambda j: (0, j)),
            pl.BlockSpec((3, tn), lambda j: (0, j + nj)),
            pl.BlockSpec((1, tn), lambda j: (0, j)),
            pl.BlockSpec((1, tn), lambda j: (0, j + nj)),
            vec(0), vec(1), vec(2), vec(3),
        ],
        out_specs=[pl.BlockSpec((Bs, tn), lambda j: (0, j))] * 3,
        out_shape=[jax.ShapeDtypeStruct((Bs, F), BF),
                   jax.ShapeDtypeStruct((Bs, F), F32),
                   jax.ShapeDtypeStruct((Bs, F), F32)],
        compiler_params=_params(("arbitrary",)),
        name="ffn_in_s",
    )(x, wg3, wu3, w_conv, w_conv, bc, bc, cflat, cflat, cflat, cflat)
    new_cache = jnp.stack([cache[:, 1, :], jnp.concatenate([ug, uu], axis=-1)], axis=1)
    return act, new_cache


def _mlstm_chunk_kernel(q_ref, k_ref, v_ref, o_ref, g_ref, gt_ref, gh_ref,
                        hs_ref, C_ref, n_ref, m_ref, *, hb):
    hblk = pl.program_id(1)
    c = pl.program_id(2)
    L = q_ref.shape[0]

    @pl.when(c == 0)
    def _():
        C_ref[...] = jnp.zeros_like(C_ref)
        n_ref[...] = jnp.zeros_like(n_ref)
        m_ref[...] = jnp.zeros_like(m_ref)

    g = g_ref[...]
    gt = gt_ref[...]
    lane = lax.broadcasted_iota(jnp.int32, g.shape, 1)
    sub = lax.broadcasted_iota(jnp.int32, gt.shape, 0)
    t_idx = lax.broadcasted_iota(jnp.int32, (L, L), 0)
    s_idx = lax.broadcasted_iota(jnp.int32, (L, L), 1)
    causal = s_idx <= t_idx
    q_scale = jnp.asarray(M_QK ** -0.5, BF)

    for hh_i in range(hb):
        h = hblk * hb + hh_i
        ic_col = jnp.sum(jnp.where(lane == h, g, 0.0), axis=1, keepdims=True)
        fc_col = jnp.sum(jnp.where(lane == h + M_HEADS, g, 0.0), axis=1, keepdims=True)
        ic_row = jnp.sum(jnp.where(sub == h, gt, 0.0), axis=0, keepdims=True)
        fc_row = jnp.sum(jnp.where(sub == h + M_HEADS, gt, 0.0), axis=0, keepdims=True)
        b_col = jnp.sum(jnp.where(causal, fc_row, 0.0), axis=1, keepdims=True)
        b_row = jnp.sum(jnp.where(t_idx <= s_idx, fc_col, 0.0), axis=0, keepdims=True)

        m_old = m_ref[hh_i][0:1, 0:1]
        d = jnp.where(causal, b_col - b_row + ic_row, -jnp.inf)
        inter = b_col + m_old
        mt = jnp.maximum(inter, jnp.max(d, axis=1, keepdims=True))
        w = jnp.exp(d - mt)
        si = jnp.exp(inter - mt)

        qb = q_ref[:, hh_i * M_QK:(hh_i + 1) * M_QK] * q_scale
        kb = k_ref[:, hh_i * M_QK:(hh_i + 1) * M_QK]
        vb = v_ref[:, hh_i * M_V:(hh_i + 1) * M_V]
        C_old = C_ref[hh_i]
        n_old = n_ref[hh_i]
        qk = lax.dot_general(qb, kb, (((1,), (1,)), ((), ())), preferred_element_type=F32) * w
        num = (jnp.dot(qk.astype(BF), vb, preferred_element_type=F32)
               + si * jnp.dot(qb, C_old.astype(BF), preferred_element_type=F32))
        den = jnp.sum(qk, axis=1, keepdims=True) + si * jnp.sum(qb.astype(F32) * n_old, axis=1, keepdims=True)
        hh = num / jnp.maximum(jnp.abs(den), jnp.exp(-mt))

        b_last = b_col[L - 1:L, :]
        gl = b_last - b_col + ic_col
        m_new = jnp.maximum(b_last + m_old, jnp.max(gl, axis=0, keepdims=True))
        wl = jnp.exp(gl - m_new)
        sd = jnp.exp(b_last + m_old - m_new)
        kw = kb.astype(F32) * wl
        C_ref[hh_i] = sd * C_old + jnp.dot(kw.T.astype(BF), vb, preferred_element_type=F32)
        n_ref[hh_i] = sd * n_old + jnp.sum(kw, axis=0, keepdims=True)
        m_ref[hh_i] = jnp.broadcast_to(m_new, (1, LANES))

        hn = hh * lax.rsqrt(jnp.mean(hh * hh, axis=1, keepdims=True) + EPS)
        o = o_ref[:, hh_i * M_V:(hh_i + 1) * M_V].astype(F32)
        hs_ref[:, hh_i * M_V:(hh_i + 1) * M_V] = (
            hn * gh_ref[:, hh_i * M_V:(hh_i + 1) * M_V] * _sigmoid(o)).astype(hs_ref.dtype)


def mlstm_prompt(z, gates, g_head, *, B, T, L, hb):
    M = B * T
    nc = T // L
    H = M_HEADS
    gt = jnp.swapaxes(gates[:, :2 * H].reshape(B * nc, L, 2 * H), 1, 2)
    row = lambda b, h, c: b * nc + c
    nhb = H // hb
    kq = nhb
    kv = nhb
    ko = 2 * nhb
    hs, C, n, m = pl.pallas_call(
        functools.partial(_mlstm_chunk_kernel, hb=hb),
        grid=(B, nhb, nc),
        in_specs=[
            pl.BlockSpec((L, hb * M_QK), lambda b, h, c: (row(b, h, c), h)),
            pl.BlockSpec((L, hb * M_QK), lambda b, h, c: (row(b, h, c), kq + h)),
            pl.BlockSpec((L, hb * M_V), lambda b, h, c: (row(b, h, c), kv + h)),
            pl.BlockSpec((L, hb * M_V), lambda b, h, c: (row(b, h, c), ko + h)),
            pl.BlockSpec((L, LANES), lambda b, h, c: (row(b, h, c), 0)),
            pl.BlockSpec((None, 2 * H, L), lambda b, h, c: (row(b, h, c), 0, 0)),
            pl.BlockSpec((1, hb * M_V), lambda b, h, c: (0, h)),
        ],
        out_specs=[
            pl.BlockSpec((L, hb * M_V), lambda b, h, c: (row(b, h, c), h)),
            pl.BlockSpec((None, hb, M_QK, M_V), lambda b, h, c: (b, h, 0, 0)),
            pl.BlockSpec((None, hb, 1, M_QK), lambda b, h, c: (b, h, 0, 0)),
            pl.BlockSpec((None, hb, 1, LANES), lambda b, h, c: (b, h, 0, 0)),
        ],
        out_shape=[
            jax.ShapeDtypeStruct((M, D_MODEL), BF),
            jax.ShapeDtypeStruct((B, M_HEADS, M_QK, M_V), F32),
            jax.ShapeDtypeStruct((B, M_HEADS, 1, M_QK), F32),
            jax.ShapeDtypeStruct((B, M_HEADS, 1, LANES), F32),
        ],
        compiler_params=_params(("arbitrary", "arbitrary", "arbitrary")),
        name="mlstm_p",
    )(z, z, z, z, gates, gt, g_head.reshape(1, D_MODEL))
    return hs, C, n[:, :, 0, :], m[:, :, 0, 0]


def _mlstm_step_kernel(q_ref, k_ref, v_ref, o_ref, ig_ref, lf_ref, m0_ref, n0_ref, C0_ref, gh_ref,
                       hs_ref, C_ref, n_ref, m_ref):
    ic = ig_ref[...]
    fc = lf_ref[...]
    m_old = m0_ref[...]
    inter = fc + m_old
    mt = jnp.maximum(inter, ic)
    w = jnp.exp(ic - mt)
    si = jnp.exp(inter - mt)
    q = (q_ref[...] * (M_QK ** -0.5)).astype(BF).astype(F32)
    k = k_ref[...]
    v = v_ref[...]
    n_old = n0_ref[...]
    qk = jnp.sum(q * k.astype(BF).astype(F32), axis=1, keepdims=True) * w
    wl = jnp.exp(ic - mt)
    sd = jnp.exp(inter - mt)
    kw = k * wl
    qT = q.T
    kwT = kw.T
    rows = []
    for h in range(M_HEADS):
        C_old = C0_ref[h]
        rows.append(jnp.sum(qT[:, h:h + 1] * C_old, axis=0, keepdims=True))
        C_ref[h] = sd[h:h + 1, :] * C_old + kwT[:, h:h + 1] * v[h:h + 1, :]
    qC = jnp.concatenate(rows, axis=0)
    num = qk.astype(BF).astype(F32) * v.astype(BF).astype(F32) + si * qC
    den = qk + si * jnp.sum(q * n_old, axis=1, keepdims=True)
    hh = num / jnp.maximum(jnp.abs(den), jnp.exp(-mt))
    n_ref[...] = sd * n_old + kw
    m_ref[...] = mt
    hn = hh * lax.rsqrt(jnp.mean(hh * hh, axis=1, keepdims=True) + EPS)
    hs_ref[...] = (hn * gh_ref[...] * _sigmoid(o_ref[...])).astype(hs_ref.dtype)


def mlstm_step(z, gates, g_head, C0, n0, m0):
    Bs = z.shape[0]
    H = M_HEADS
    qd, vd = H * M_QK, H * M_V
    q = z[:, :qd].reshape(Bs, H, M_QK)
    k = z[:, qd:2 * qd].reshape(Bs, H, M_QK)
    v = z[:, 2 * qd:2 * qd + vd].reshape(Bs, H, M_V)
    o = z[:, 2 * qd + vd:].reshape(Bs, H, M_V)
    ig = gates[:, :H].reshape(Bs, H, 1)
    lf = gates[:, H:2 * H].reshape(Bs, H, 1)
    per_b = lambda *tail: pl.BlockSpec((None,) + tail, lambda b: (b,) + (0,) * len(tail))
    hs, C, n, m = pl.pallas_call(
        _mlstm_step_kernel,
        grid=(Bs,),
        in_specs=[
            per_b(H, M_QK), per_b(H, M_QK), per_b(H, M_V), per_b(H, M_V),
            per_b(H, 1), per_b(H, 1), per_b(H, 1),
            pl.BlockSpec((None, None, H, M_QK), lambda b: (0, b, 0, 0)),
            pl.BlockSpec((None, None, H, M_QK, M_V), lambda b: (0, b, 0, 0, 0)),
            pl.BlockSpec((H, M_V), lambda b: (0, 0)),
        ],
        out_specs=[
            per_b(H, M_V),
            pl.BlockSpec((None, None, H, M_QK, M_V), lambda b: (0, b, 0, 0, 0)),
            pl.BlockSpec((None, None, H, M_QK), lambda b: (0, b, 0, 0)),
            per_b(H, 1),
        ],
        out_shape=[
            jax.ShapeDtypeStruct((Bs, H, M_V), BF),
            jax.ShapeDtypeStruct(C0.shape, F32),
            jax.ShapeDtypeStruct(n0.shape, F32),
            jax.ShapeDtypeStruct((Bs, H, 1), F32),
        ],
        compiler_params=_params(("arbitrary",)),
        name="mlstm_s",
    )(q, k, v, o, ig, lf, m0.reshape(Bs, H, 1), n0, C0, g_head.reshape(H, M_V))
    return hs.reshape(Bs, D_MODEL), C, n, m.reshape(1, Bs, H)


def _attn_kernel(sink_ref, q_ref, kc_ref, kp_ref, vc_ref, vp_ref, o_ref):
    n = pl.program_id(1)
    W = WINDOW
    R = GROUP * W
    kp = kp_ref[...].astype(BF)
    kc = kc_ref[...].astype(BF)
    vpT = vp_ref[...].T.astype(BF)
    vcT = vc_ref[...].T.astype(BF)
    j_idx = lax.broadcasted_iota(jnp.int32, (W, R), 0)
    i_idx = lax.broadcasted_iota(jnp.int32, (W, R), 1) % W
    from_prev = j_idx > i_idx
    lane_g = lax.broadcasted_iota(jnp.int32, (1, R), 1) // W
    no_prev = jnp.where(n > 0, 0.0, -jnp.inf)
    nt = (((1,), (1,)), ((), ()))
    for h in range(KV_HEADS):
        hs = slice(h * HEAD_DIM, (h + 1) * HEAD_DIM)
        qh = jnp.concatenate(
            [q_ref[:, (h * GROUP + g) * HEAD_DIM:(h * GROUP + g + 1) * HEAD_DIM] for g in range(GROUP)], axis=0)
        sp = lax.dot_general(kp[:, hs], qh, nt, preferred_element_type=F32)
        sc = lax.dot_general(kc[:, hs], qh, nt, preferred_element_type=F32)
        s = jnp.where(from_prev, sp + no_prev, sc)
        sk = jnp.zeros((1, R), F32)
        for g in range(GROUP):
            sk = jnp.where(lane_g == g, sink_ref[h * GROUP + g], sk)
        mx = jnp.maximum(jnp.max(s, axis=0, keepdims=True), sk)
        p = jnp.exp(s - mx)
        p = p / (jnp.sum(p, axis=0, keepdims=True) + jnp.exp(sk - mx))
        oT = (jnp.dot(vpT[hs, :], jnp.where(from_prev, p, 0.0).astype(BF), preferred_element_type=F32)
              + jnp.dot(vcT[hs, :], jnp.where(from_prev, 0.0, p).astype(BF), preferred_element_type=F32))
        for g in range(GROUP):
            c0 = (h * GROUP + g) * HEAD_DIM
            o_ref[:, c0:c0 + HEAD_DIM] = oT[:, g * W:(g + 1) * W].T.astype(o_ref.dtype)


def attention_prompt(q, k, v, sinks, *, B, T):
    M = B * T
    W = WINDOW
    nb = T // W
    kvd = KV_HEADS * HEAD_DIM
    cur = lambda b, n: (b * nb + n, 0)
    prev = lambda b, n: (b * nb + jnp.maximum(n - 1, 0), 0)
    return pl.pallas_call(
        _attn_kernel,
        grid=(B, nb),
        in_specs=[
            pl.BlockSpec(memory_space=pltpu.SMEM),
            pl.BlockSpec((W, D_MODEL), cur),
            pl.BlockSpec((W, kvd), cur), pl.BlockSpec((W, kvd), prev),
            pl.BlockSpec((W, kvd), cur), pl.BlockSpec((W, kvd), prev),
        ],
        out_specs=pl.BlockSpec((W, D_MODEL), cur),
        out_shape=jax.ShapeDtypeStruct((M, D_MODEL), BF),
        compiler_params=_params(("arbitrary", "arbitrary")),
        name="attn_p",
    )(sinks, q, k, k, v, v)


def _per_head_rows(row):
    return jnp.concatenate(
        [jnp.broadcast_to(row[:, h * HEAD_DIM:(h + 1) * HEAD_DIM], (GROUP, HEAD_DIM)) for h in range(KV_HEADS)],
        axis=0)


def _attn_step_kernel(q_ref, kc_ref, vc_ref, kn_ref, vn_ref, sk_ref, o_ref, ko_ref, vo_ref, *, bb):
    W = WINDOW
    jl = lax.broadcasted_iota(jnp.int32, (A_HEADS, W), 1)
    sk = sk_ref[...]
    heads = lambda a, h: a[h * GROUP:(h + 1) * GROUP, :]
    scores = []
    for b in range(bb):
        q = q_ref[b]
        kcb = kc_ref[b].astype(BF)
        s = jnp.concatenate(
            [lax.dot_general(heads(q, h).astype(BF), kcb[:, h * HEAD_DIM:(h + 1) * HEAD_DIM],
                             (((1,), (1,)), ((), ())), preferred_element_type=F32) for h in range(KV_HEADS)],
            axis=0)
        kne = _per_head_rows(kn_ref[b]).astype(BF).astype(F32)
        sn = jnp.sum(q.astype(BF).astype(F32) * kne, axis=1, keepdims=True)
        scores.append((s, sn))
    probs = []
    for s, sn in scores:
        s = jnp.where(jl >= 1, s, -jnp.inf)
        mx = jnp.maximum(jnp.maximum(jnp.max(s, axis=1, keepdims=True), sn), sk)
        p = jnp.exp(s - mx)
        pn = jnp.exp(sn - mx)
        den = jnp.sum(p, axis=1, keepdims=True) + pn + jnp.exp(sk - mx)
        probs.append((p / den, pn / den))
    for b in range(bb):
        p, pn = probs[b]
        vcb = vc_ref[b].astype(BF)
        o = jnp.concatenate(
            [jnp.dot(heads(p, h).astype(BF), vcb[:, h * HEAD_DIM:(h + 1) * HEAD_DIM],
                     preferred_element_type=F32) for h in range(KV_HEADS)], axis=0)
        vne = _per_head_rows(vn_ref[b]).astype(BF).astype(F32)
        o_ref[b] = o + pn.astype(BF).astype(F32) * vne
        ko_ref[b, 0:W - 1, :] = kc_ref[b, 1:W, :]
        ko_ref[b, W - 1:W, :] = kn_ref[b]
        vo_ref[b, 0:W - 1, :] = vc_ref[b, 1:W, :]
        vo_ref[b, W - 1:W, :] = vn_ref[b]


def attention_step(q, k_new, v_new, kbuf, vbuf, sinks):
    Bs = q.shape[0]
    W = WINDOW
    kvd = KV_HEADS * HEAD_DIM
    bb = 4
    assert Bs % bb == 0
    per_b = lambda *tail: pl.BlockSpec((bb,) + tail, lambda b: (b,) + (0,) * len(tail))
    o, ko, vo = pl.pallas_call(
        functools.partial(_attn_step_kernel, bb=bb),
        grid=(Bs // bb,),
        in_specs=[per_b(A_HEADS, HEAD_DIM), per_b(W, kvd), per_b(W, kvd), per_b(1, kvd), per_b(1, kvd),
                  pl.BlockSpec((A_HEADS, 1), lambda b: (0, 0))],
        out_specs=[per_b(A_HEADS, HEAD_DIM), per_b(W, kvd), per_b(W, kvd)],
        out_shape=[jax.ShapeDtypeStruct((Bs, A_HEADS, HEAD_DIM), F32),
                   jax.ShapeDtypeStruct((Bs, W, kvd), F32),
                   jax.ShapeDtypeStruct((Bs, W, kvd), F32)],
        compiler_params=_params(("arbitrary",)),
        name="attn_s",
    )(q.reshape(Bs, A_HEADS, HEAD_DIM), kbuf.reshape(Bs, W, kvd), vbuf.reshape(Bs, W, kvd),
      k_new.reshape(Bs, 1, kvd), v_new.reshape(Bs, 1, kvd), sinks.reshape(A_HEADS, 1))
    return (o.reshape(Bs, D_MODEL), ko.reshape(Bs, W, KV_HEADS, HEAD_DIM), vo.reshape(Bs, W, KV_HEADS, HEAD_DIM))


def _rope_tables(pos, width):
    half = HEAD_DIM // 2
    freq = ROPE_THETA ** (-jnp.arange(half, dtype=F32) / half)
    ang = pos.astype(F32)[:, None] * freq[None, :]
    cos = jnp.cos(ang)
    sin = jnp.sin(ang)
    reps = width // HEAD_DIM
    return (jnp.tile(jnp.concatenate([cos, cos], axis=1), (1, reps)),
            jnp.tile(jnp.concatenate([-sin, sin], axis=1), (1, reps)))


def _tiles(T):
    if T > 1:
        return dict(tm=1024, tn=1024, n_sub=2,
                    tm_r=512, tn_r=1024, n_sub_r=1,
                    tn_kv=512, tn_ffi=256, n_sub_ffi=2,
                    tm_ffo=512, tn_ffo=512, n_sub_ffo=1, tm_mod=256)
    return dict(tm=None, tn=512, n_sub=1, tm_r=None, tn_r=512, n_sub_r=1, tn_kv=512,
                tn_ffi=256, n_sub_ffi=1, tm_ffo=None, tn_ffo=256, n_sub_ffo=1, tm_mod=None)


def _run(x, ada, pos, state, P, *, B, T, wq_in=None):
    D = D_MODEL
    M = B * T
    c = {k: (M if v is None else v) for k, v in _tiles(T).items()}
    tm, tn, tm_r, tn_r, tm_mod = c["tm"], c["tn"], c["tm_r"], c["tn_r"], c["tm_mod"]
    cos, sin = _rope_tables(pos, LANES)
    if T > 1:
        tps = T // tm
        rope_spec = pl.BlockSpec((tm, LANES), lambda j, i: (i % tps, 0))
    else:
        rope_spec = pl.BlockSpec((1, LANES), lambda j, i: (0, 0))

    out = {}
    tag = "_p" if T > 1 else "_s"
    wq = {}

    def lin(key, x_, w3, lead, **kw):
        if wq_in is None:
            res, wq[key] = linear(x_, w3, lead, keep_bf16=True, **kw)
            return res
        kw.pop("col_off", None)
        return linear(x_, wq_in[key], 0, **kw)

    sh1, sc1, ga1, sh2, sc2, ga2 = ada[0]
    hn = modulate(x, P["g_norm1"][0], sh1, sc1, T=T, tm=tm_mod)
    qd, vd = M_HEADS * M_QK, M_HEADS * M_V
    w_inT = jnp.swapaxes(P["w_m_in"], 1, 2)
    z = lin("m_in", hn, w_inT, 0, name="m_in" + tag, n_cols=2 * qd + 2 * vd, tm=tm, tn=tn, n_sub=c["n_sub"],
               epilogue=_ep_plain, transposed=True, out_dtype=BF if T > 1 else F32)
    wgT = jnp.pad(w_inT[:, 2 * qd + 2 * vd:, :], ((0, 0), (0, LANES - 2 * M_HEADS), (0, 0)))
    bg = jnp.pad(jnp.concatenate([P["b_m_i"][0], P["b_m_f"][0]]), (0, LANES - 2 * M_HEADS))
    gates = linear(hn, wgT, 0, name="m_gates" + tag, n_cols=LANES, tm=tm, tn=LANES, epilogue=_ep_gates,
                   extras=[_row_vec(bg, LANES)], transposed=True)
    if T > 1:
        hs, C, n, m = mlstm_prompt(z, gates, P["g_m_head"][0], B=B, T=T, L=256, hb=4)
        out["C"], out["n"], out["m"] = C[None], n[None], m[None]
    else:
        hs, out["C"], out["n"], out["m"] = mlstm_step(z, gates, P["g_m_head"][0], *state["mlstm"])
    x = lin("m_out", hs, P["w_m_out"], 0, name="m_out" + tag, n_cols=D, tm=tm_r, tn=tn_r, n_sub=c["n_sub_r"],
               epilogue=_ep_resid, extras=[_tile(x, tn_r, T, tm_r), _seq_vec(ga1, tn_r, T, tm_r)])

    convs = []
    for l in range(2):
        if l == 1:
            sh1, sc1, ga1, sh2, sc2, ga2 = ada[1]
            shk, sck = ada[2]
            hk = modulate(x, P["g_kv"], shk, sck, T=T, tm=tm_mod)
            kvd = KV_HEADS * HEAD_DIM
            tn_kv = c["tn_kv"]
            k = lin("kv_k", hk, P["w_kv"], 0, name="kv_k" + tag, n_cols=kvd, tm=tm, tn=tn_kv, n_sub=c["n_sub"],
                       epilogue=functools.partial(_ep_rope, scale=1.0),
                       extras=[_row_vec(P["b_kv"][:kvd], tn_kv), (cos, rope_spec), (sin, rope_spec)])
            v = lin("kv_v", hk, P["w_kv"], 0, name="kv_v" + tag, n_cols=kvd, tm=tm, tn=tn_kv, n_sub=c["n_sub"],
                       col_off=kvd // tn_kv, epilogue=_ep_bias, extras=[_row_vec(P["b_kv"][kvd:], tn_kv)])
            hn = modulate(x, P["g_norm1"][1], sh1, sc1, T=T, tm=tm_mod)
            q = lin("attn_q", hn, P["w_q"], 0, name="attn_q" + tag, n_cols=D, tm=tm, tn=tn, n_sub=c["n_sub"],
                       epilogue=functools.partial(_ep_rope, scale=HEAD_DIM ** -0.5),
                       extras=[_row_vec(P["b_q"][0], tn), (cos, rope_spec), (sin, rope_spec)],
                       out_dtype=BF if T > 1 else F32)
            if T > 1:
                o = attention_prompt(q, k, v, P["sinks"][0], B=B, T=T)
                last = lambda a: a.reshape(B, T, kvd)[:, T - WINDOW:, :].reshape(B, WINDOW, KV_HEADS, HEAD_DIM)
                out["k_win"], out["v_win"] = last(k), last(v)
            else:
                o, out["k_win"], out["v_win"] = attention_step(q, k, v, state["kbuf"], state["vbuf"], P["sinks"][0])
            x = lin("attn_o", o, P["w_o"], 0, name="attn_o" + tag, n_cols=D, tm=tm_r, tn=tn_r, n_sub=c["n_sub_r"],
                       epilogue=_ep_resid_bias,
                       extras=[_row_vec(P["b_o"][0], tn_r), _tile(x, tn_r, T, tm_r), _seq_vec(ga1, tn_r, T, tm_r)])

        il = T > 1
        hn = modulate(x, P["g_norm2"][l], sh2, sc2, T=T, tm=PERM_GROUP if il else tm_mod, interleave=il)
        if T > 1:
            act, cb, wq["ffn_in%d" % l] = ffn_in_prompt(hn, P["w_ffn_in"], l, P["w_conv"][l], P["b_conv"][l], B=B, T=T,
                                                        tm=tm, tn=c["tn_ffi"], n_sub=c["n_sub_ffi"])
        else:
            act, cb = ffn_in_step(hn, *wq_in["ffn_in%d" % l], P["w_conv"][l], P["b_conv"][l], state["conv"][l],
                                  tn=c["tn_ffi"])
        convs.append(cb)
        tm_o, tn_o = c["tm_ffo"], c["tn_ffo"]
        x = lin("ffn_out%d" % l, act, P["w_ffn_out"], l, name="ffn_out" + tag, n_cols=D, tm=tm_o, tn=tn_o, n_sub=c["n_sub_ffo"],
                   epilogue=_ep_resid, extras=[_tile(x, tn_o, T, tm_o), _seq_vec(ga2, tn_o, T, tm_o)],
                   rows_interleaved=il)

    out["conv"] = jnp.stack(convs)
    out["y"] = rmsnorm(x, P["g_final"], tm=tm_mod)
    out["wq"] = wq
    return out


def kernel(x_prompt, x_sample, state_mlstm_C, state_mlstm_n, state_mlstm_m, cache_conv, cache_k_win, cache_v_win, c_prompt, c_sample, w_ada, g_norm1, g_norm2, w_m_in, b_m_i, b_m_f, g_m_head, w_m_out, w_ada_kv, g_kv, w_kv, b_kv, w_q, b_q, sinks, w_o, b_o, w_ffn_in, w_conv, b_conv, w_ffn_out, g_final):
    D = D_MODEL
    Bp, Tp, _ = x_prompt.shape
    Bs, Ts, _ = x_sample.shape
    assert Ts == 1
    P = dict(g_norm1=g_norm1, g_norm2=g_norm2, w_m_in=w_m_in, b_m_i=b_m_i, b_m_f=b_m_f, g_m_head=g_m_head,
             w_m_out=w_m_out, g_kv=g_kv, w_kv=w_kv[None], b_kv=b_kv, w_q=w_q, b_q=b_q, sinks=sinks, w_o=w_o,
             b_o=b_o, w_ffn_in=w_ffn_in, w_conv=w_conv, b_conv=b_conv, w_ffn_out=w_ffn_out, g_final=g_final)

    n_c = Bp + Bs
    pad_c = -n_c % 16
    cs = silu_cast(jnp.concatenate([c_prompt, c_sample, jnp.zeros((pad_c, D), F32)], axis=0))
    rows_c = n_c + pad_c
    ada_all = []
    for l in range(2):
        a = linear(cs, w_ada, l, name="ada", n_cols=6 * D, tm=rows_c, tn=512, epilogue=_ep_plain)
        ada_all.append(a)
    a_kv = linear(cs, w_ada_kv[None], 0, name="ada_kv", n_cols=2 * D, tm=rows_c, tn=512, epilogue=_ep_plain)

    def split(lo, hi):
        per_layer = [tuple(a[lo:hi, i * D:(i + 1) * D] for i in range(6)) for a in ada_all]
        return per_layer + [(a_kv[lo:hi, :D], a_kv[lo:hi, D:])]

    po = _run(x_prompt.reshape(Bp * Tp, D), split(0, Bp), jnp.arange(Tp, dtype=jnp.int32), None, P, B=Bp, T=Tp)
    state = dict(mlstm=(state_mlstm_C, state_mlstm_n, state_mlstm_m), conv=cache_conv,
                 kbuf=cache_k_win, vbuf=cache_v_win)
    so = _run(x_sample.reshape(Bs, D), split(Bp, Bp + Bs), PAST_LEN + jnp.arange(1, dtype=jnp.int32), state, P,
              B=Bs, T=1, wq_in=po["wq"])
    return (po["y"].reshape(Bp, Tp, D), so["y"].reshape(Bs, 1, D),
            po["C"], po["n"], po["m"], po["conv"], po["k_win"], po["v_win"],
            so["C"], so["n"], so["m"], so["conv"], so["k_win"], so["v_win"])
```

```python
import functools

import jax
import jax.numpy as jnp
from jax import lax
from jax.experimental import pallas as pl
from jax.experimental.pallas import tpu as pltpu

BF = jnp.bfloat16
F32 = jnp.float32

D_MODEL = 4096
M_HEADS = 8
M_QK = 256
M_V = 512
M_CHUNK = 64
GATE_CAP = 15.0
HEAD_DIM = 64
A_HEADS = 64
KV_HEADS = 8
GROUP = 8
WINDOW = 128
ROPE_THETA = 10000.0
D_FF = 11008
PAST_LEN = 16384
EPS = 1e-6

VMEM_LIMIT_V7X = 58 * 1024 * 1024
LANES = 128


def _params(sem):
    return pltpu.CompilerParams(dimension_semantics=sem, vmem_limit_bytes=VMEM_LIMIT_V7X)


def _sigmoid(x):
    return 1.0 / (1.0 + jnp.exp(-x))


def _lag_row(j, i):
    return i * jnp.minimum(j, 1)


def _lag_col(j):
    return jnp.maximum(j - 1, 0)


def _lagged(index_map):
    return lambda j, i: index_map(_lag_col(j), _lag_row(j, i))


def _chunk(j, i, nj, ni):
    last = j // nj
    return jnp.minimum(j, nj - 1), i + last * (ni - 1 - i)


def _stage_weight(i, w_ref, wb_ref, col0=0, wq_ref=None):
    ck, cw = w_ref.shape
    wq = w_ref[...].astype(BF)
    wb_ref[pl.ds(pl.multiple_of(i * ck, ck), ck), col0:col0 + cw] = wq
    if wq_ref is not None:
        wq_ref[...] = wq


def _by_slot(j, stage, step, before=None):
    @pl.when(j == 0)
    def _():
        stage(0)

    for parity in (0, 1):
        @pl.when((j > 0) & (j % 2 == parity))
        def _():
            if before is not None:
                before()
            stage(parity)
            step(1 - parity)


def _linear_kernel(x_ref, w_ref, *rest, n_extra, epilogue, n_sub, transposed, rows_interleaved, keep_bf16):
    extra = rest[:n_extra]
    o_ref = rest[n_extra]
    wq_ref = rest[n_extra + 1] if keep_bf16 else None
    wb_refs = rest[-2:]
    j = pl.program_id(0)
    i = pl.program_id(1)
    tm = x_ref.shape[0]
    sub = tm // n_sub

    def stage(slot):
        _stage_weight(i, w_ref, wb_refs[slot], wq_ref=wq_ref)

    def step(slot):
        w = wb_refs[slot][...]
        for r in range(n_sub):
            rows = slice(r * sub, (r + 1) * sub)
            xs = x_ref[rows, :].astype(BF)
            if transposed:
                acc = lax.dot_general(xs, w, (((1,), (1,)), ((), ())), preferred_element_type=F32)
            else:
                acc = jnp.dot(xs, w, preferred_element_type=F32)
            if rows_interleaved:
                assert sub == PERM_GROUP
                acc = pltpu.einshape("qsd->sqd", acc.reshape(PERM_Q, SUBLANES, acc.shape[1])).reshape(acc.shape)
            ex = [e[rows, :] if e.shape[0] == tm else e[...] for e in extra]
            o_ref[rows, :] = epilogue(acc, *ex).astype(o_ref.dtype)

    _by_slot(j, stage, step)


def linear(x, w3, lead, *, name, n_cols, tm, tn, epilogue, extras=(), out_dtype=F32,
           col_off=0, n_sub=1, transposed=False, rows_interleaved=False, keep_bf16=False):
    M, K = x.shape
    assert M % tm == 0 and n_cols % tn == 0 and tm % n_sub == 0
    nj, ni = n_cols // tn, M // tm
    chunk = lambda j, i: _chunk(j, i, nj, ni)
    if transposed:
        cn = tn // ni
        assert cn * ni == tn and cn % 16 == 0
        w_spec = pl.BlockSpec((None, cn, K),
                              lambda j, i: (lead, (chunk(j, i)[0] + col_off) * ni + chunk(j, i)[1], 0))
        wq_spec = pl.BlockSpec((None, cn, K), lambda j, i: (0, chunk(j, i)[0] * ni + chunk(j, i)[1], 0))
        wq_shape = (1, n_cols, K)
        wb_shape = (tn, K)
    else:
        ck = K // ni
        assert ck * ni == K and ck % 16 == 0
        w_spec = pl.BlockSpec((None, ck, tn), lambda j, i: (lead, chunk(j, i)[1], chunk(j, i)[0] + col_off))
        wq_spec = pl.BlockSpec((None, ck, tn), lambda j, i: (0, chunk(j, i)[1], chunk(j, i)[0]))
        wq_shape = (1, K, n_cols)
        wb_shape = (K, tn)
    in_specs = [pl.BlockSpec((tm, K), lambda j, i: (_lag_row(j, i), 0)), w_spec]
    in_specs += [pl.BlockSpec(s.block_shape, _lagged(s.index_map)) for _, s in extras]
    out_specs = [pl.BlockSpec((tm, tn), lambda j, i: (_lag_row(j, i), _lag_col(j)))]
    out_shape = [jax.ShapeDtypeStruct((M, n_cols), out_dtype)]
    if keep_bf16:
        out_specs.append(wq_spec)
        out_shape.append(jax.ShapeDtypeStruct(wq_shape, BF))
    res = pl.pallas_call(
        functools.partial(_linear_kernel, n_extra=len(extras), epilogue=epilogue, n_sub=n_sub,
                          transposed=transposed, rows_interleaved=rows_interleaved, keep_bf16=keep_bf16),
        grid=(nj + 1, ni),
        in_specs=in_specs,
        out_specs=out_specs,
        out_shape=out_shape,
        scratch_shapes=[pltpu.VMEM(wb_shape, BF)] * 2,
        compiler_params=_params(("arbitrary", "arbitrary")),
        name=name,
    )(x, w3, *[a for a, _ in extras])
    return tuple(res) if keep_bf16 else res[0]


def _row_vec(v, tn):
    return v.reshape(1, -1), pl.BlockSpec((1, tn), lambda j, i: (0, j))


def _seq_vec(a, tn, T, tm):
    if T == 1:
        return a, pl.BlockSpec((tm, tn), lambda j, i: (i, j))
    tps = T // tm
    return a.reshape(a.shape[0], 1, a.shape[1]), pl.BlockSpec((None, 1, tn), lambda j, i: (i // tps, 0, j))


def _tile(a, tn, T, tm, col0=0):
    return a, pl.BlockSpec((tm, tn), lambda j, i: (i, j + col0))


def _ep_plain(acc):
    return acc


def _ep_bias(acc, b):
    return acc + b


def _ep_resid(acc, r, g):
    return r + g * acc


def _ep_resid_bias(acc, b, r, g):
    return r + g * (acc + b)


def _ep_rope(acc, b, cos, sin, *, scale):
    y = acc + b
    n = y.shape[1]
    cos = jnp.concatenate([cos] * (n // LANES), axis=1)
    sin = jnp.concatenate([sin] * (n // LANES), axis=1)
    lane = lax.broadcasted_iota(jnp.int32, y.shape, 1)
    first_half = (lane % HEAD_DIM) < (HEAD_DIM // 2)
    partner = jnp.where(first_half, pltpu.roll(y, n - HEAD_DIM // 2, axis=1), pltpu.roll(y, HEAD_DIM // 2, axis=1))
    return (y * cos + partner * sin) * scale


def _ep_gates(acc, b):
    z = GATE_CAP * jnp.tanh((acc + b) / GATE_CAP)
    logsig = jnp.minimum(z, 0.0) - jnp.log(1.0 + jnp.exp(-jnp.abs(z)))
    lane = lax.broadcasted_iota(jnp.int32, z.shape, 1)
    return jnp.where(lane < M_HEADS, z, logsig)


def _silu_kernel(c_ref, o_ref):
    c = c_ref[...]
    o_ref[...] = (c * _sigmoid(c)).astype(o_ref.dtype)


def silu_cast(c):
    return pl.pallas_call(
        _silu_kernel,
        out_shape=jax.ShapeDtypeStruct(c.shape, BF),
        name="silu_c",
    )(c)


ROW_CHUNK = 16


def _modulate_kernel(x_ref, g_ref, sh_ref, sc_ref, o_ref):
    tm = x_ref.shape[0]
    per_row = sh_ref.shape[0] == tm
    g = g_ref[...]
    if not per_row:
        scale_all = 1.0 + sc_ref[...]
        shift_all = sh_ref[...]

    def body(c, carry):
        r = pl.ds(pl.multiple_of(c * ROW_CHUNK, ROW_CHUNK), ROW_CHUNK)
        x = x_ref[r, :]
        y = x * lax.rsqrt(jnp.mean(x * x, axis=-1, keepdims=True) + EPS) * g
        scale = 1.0 + sc_ref[r, :] if per_row else scale_all
        shift = sh_ref[r, :] if per_row else shift_all
        o_ref[r, :] = (y * scale + shift).astype(o_ref.dtype)
        return carry

    n_chunks = tm // ROW_CHUNK
    lax.fori_loop(0, n_chunks, body, 0, unroll=min(4, n_chunks))


def _modulate_interleave_kernel(x_ref, g_ref, sh_ref, sc_ref, o_ref):
    D = x_ref.shape[1]
    x = x_ref[...]
    rstd = lax.rsqrt(jnp.mean(x * x, axis=-1, keepdims=True) + EPS)
    for c in range(D // LANES):
        cols = slice(c * LANES, (c + 1) * LANES)
        y = x_ref[:, cols] * rstd * g_ref[:, cols]
        y = y * (1.0 + sc_ref[:, cols]) + sh_ref[:, cols]
        y = pltpu.einshape("sqd->qsd", y.reshape(SUBLANES, PERM_Q, LANES)).reshape(PERM_GROUP, LANES)
        o_ref[:, cols] = y.astype(o_ref.dtype)


def modulate(x, g, sh, sc, *, T, tm, interleave=False):
    M, D = x.shape
    assert not interleave or (tm == PERM_GROUP and T > 1)
    sh_a, sh_s = _seq_vec(sh, D, T, tm)
    sc_a, sc_s = _seq_vec(sc, D, T, tm)
    fix = lambda s: pl.BlockSpec(s.block_shape, functools.partial(lambda im, i: im(0, i), s.index_map))
    return pl.pallas_call(
        _modulate_interleave_kernel if interleave else _modulate_kernel,
        grid=(M // tm,),
        in_specs=[pl.BlockSpec((tm, D), lambda i: (i, 0)),
                  pl.BlockSpec((1, D), lambda i: (0, 0)),
                  fix(sh_s), fix(sc_s)],
        out_specs=pl.BlockSpec((tm, D), lambda i: (i, 0)),
        out_shape=jax.ShapeDtypeStruct((M, D), BF),
        compiler_params=_params(("arbitrary",)),
        name="modulate",
    )(x, g.reshape(1, D), sh_a, sc_a)


def _rmsnorm_kernel(x_ref, g_ref, o_ref):
    x = x_ref[...]
    o_ref[...] = x * lax.rsqrt(jnp.mean(x * x, axis=-1, keepdims=True) + EPS) * g_ref[...]


def rmsnorm(x, g, *, tm):
    M, D = x.shape
    return pl.pallas_call(
        _rmsnorm_kernel,
        grid=(M // tm,),
        in_specs=[pl.BlockSpec((tm, D), lambda i: (i, 0)), pl.BlockSpec((1, D), lambda i: (0, 0))],
        out_specs=pl.BlockSpec((tm, D), lambda i: (i, 0)),
        out_shape=jax.ShapeDtypeStruct((M, D), F32),
        compiler_params=_params(("arbitrary",)),
        name="final_norm",
    )(x, g.reshape(1, D))


def _conv_gate(ug, uu, pg1, pg2, pu1, pu2, wcg, wcu, bcg, bcu):
    yg = bcg + pg2 * wcg[0:1] + pg1 * wcg[1:2] + ug * wcg[2:3]
    yu = bcu + pu2 * wcu[0:1] + pu1 * wcu[1:2] + uu * wcu[2:3]
    return yg * _sigmoid(yg) * yu


SUBLANES = 8
PERM_GROUP = 512
PERM_Q = PERM_GROUP // SUBLANES


def _ffn_in_kernel(x_ref, wg_ref, wu_ref, wcg_ref, wcu_ref, bcg_ref, bcu_ref,
                   act_ref, cg_ref, cu_ref, wqg_ref, wqu_ref, wb0_ref, wb1_ref, car_ref, *, tps, n_sub):
    j = pl.program_id(0)
    i = pl.program_id(1)
    tm = x_ref.shape[0]
    tn = wg_ref.shape[1]
    G = PERM_GROUP
    assert tm == n_sub * G
    wb_refs = (wb0_ref, wb1_ref)

    def stage(slot):
        _stage_weight(i, wg_ref, wb_refs[slot], 0, wqg_ref)
        _stage_weight(i, wu_ref, wb_refs[slot], tn, wqu_ref)

    def step(slot):
        w = wb_refs[slot][...]
        wcg, wcu, bcg, bcu = wcg_ref[...], wcu_ref[...], bcg_ref[...], bcu_ref[...]
        car = car_ref[...]
        first = lax.broadcasted_iota(jnp.int32, (SUBLANES, 2 * tn), 0) == 0
        for r in range(n_sub):
            rows = slice(r * G, (r + 1) * G)
            u = jnp.dot(x_ref[rows, :], w, preferred_element_type=F32)
            fix1 = jnp.where(first, car[1:2], pltpu.roll(u[G - 8:G], 1, axis=0))
            fix2 = jnp.where(first, car[0:1], pltpu.roll(u[G - 16:G - 8], 1, axis=0))
            p1 = jnp.concatenate([fix1, u[:G - 8]], axis=0)
            p2 = jnp.concatenate([fix2, fix1, u[:G - 16]], axis=0)
            act_ref[rows, :] = _conv_gate(u[:, :tn], u[:, tn:], p1[:, :tn], p2[:, :tn], p1[:, tn:], p2[:, tn:],
                                          wcg, wcu, bcg, bcu).astype(act_ref.dtype)
            car = jnp.concatenate([u[G - 9:G - 8], u[G - 1:G]], axis=0)
        car_ref[...] = car
        cg_ref[...] = car[:, :tn]
        cu_ref[...] = car[:, tn:]

    def reset_history():
        @pl.when(i % tps == 0)
        def _():
            car_ref[...] = jnp.zeros_like(car_ref)

    _by_slot(j, stage, step, before=reset_history)


def ffn_in_prompt(x, w3, lead, w_conv, b_conv, *, B, T, tm, tn, n_sub):
    M, K = x.shape
    F = D_FF
    nj = F // tn
    ni = M // tm
    ck = K // ni
    assert ck * ni == K and ck % 16 == 0
    tps = T // tm
    wc = w_conv
    bc = b_conv.reshape(1, 2 * F)
    chunk = lambda j, i: _chunk(j, i, nj, ni)
    wq_spec = pl.BlockSpec((None, ck, tn), lambda j, i: (0, chunk(j, i)[1], chunk(j, i)[0]))
    act, cg, cu, wq_g, wq_u = pl.pallas_call(
        functools.partial(_ffn_in_kernel, tps=tps, n_sub=n_sub),
        grid=(nj + 1, ni),
        in_specs=[
            pl.BlockSpec((tm, K), lambda j, i: (_lag_row(j, i), 0)),
            pl.BlockSpec((None, ck, tn), lambda j, i: (lead, chunk(j, i)[1], chunk(j, i)[0])),
            pl.BlockSpec((None, ck, tn), lambda j, i: (lead, chunk(j, i)[1], chunk(j, i)[0] + nj)),
            pl.BlockSpec((3, tn), lambda j, i: (0, _lag_col(j))),
            pl.BlockSpec((3, tn), lambda j, i: (0, _lag_col(j) + nj)),
            pl.BlockSpec((1, tn), lambda j, i: (0, _lag_col(j))),
            pl.BlockSpec((1, tn), lambda j, i: (0, _lag_col(j) + nj)),
        ],
        out_specs=[
            pl.BlockSpec((tm, tn), lambda j, i: (_lag_row(j, i), _lag_col(j))),
            pl.BlockSpec((None, 2, tn), lambda j, i: (_lag_row(j, i) // tps, 0, _lag_col(j))),
            pl.BlockSpec((None, 2, tn), lambda j, i: (_lag_row(j, i) // tps, 0, _lag_col(j))),
            wq_spec, wq_spec,
        ],
        out_shape=[
            jax.ShapeDtypeStruct((M, F), BF),
            jax.ShapeDtypeStruct((B, 2, F), F32),
            jax.ShapeDtypeStruct((B, 2, F), F32),
            jax.ShapeDtypeStruct((1, K, F), BF),
            jax.ShapeDtypeStruct((1, K, F), BF),
        ],
        scratch_shapes=[pltpu.VMEM((K, 2 * tn), BF), pltpu.VMEM((K, 2 * tn), BF),
                        pltpu.VMEM((2, 2 * tn), F32)],
        compiler_params=_params(("arbitrary", "arbitrary")),
        name="ffn_in_p",
    )(x, w3, w3, wc, wc, bc, bc)
    return act, jnp.concatenate([cg, cu], axis=-1), (wq_g, wq_u)


def _ffn_in_step_kernel(x_ref, wg_ref, wu_ref, wcg_ref, wcu_ref, bcg_ref, bcu_ref,
                        c0g_ref, c0u_ref, c1g_ref, c1u_ref, act_ref, ug_ref, uu_ref):
    x = x_ref[...]
    ug = jnp.dot(x, wg_ref[...].astype(BF), preferred_element_type=F32)
    uu = jnp.dot(x, wu_ref[...].astype(BF), preferred_element_type=F32)
    act_ref[...] = _conv_gate(ug, uu, c1g_ref[...], c0g_ref[...], c1u_ref[...], c0u_ref[...],
                              wcg_ref[...], wcu_ref[...], bcg_ref[...], bcu_ref[...]).astype(act_ref.dtype)
    ug_ref[...] = ug
    uu_ref[...] = uu


def ffn_in_step(x, wg3, wu3, w_conv, b_conv, cache, *, tn):
    Bs, K = x.shape
    F = D_FF
    nj = F // tn
    bc = b_conv.reshape(1, 2 * F)
    cflat = cache.reshape(Bs, 4 * F)
    vec = lambda off: pl.BlockSpec((Bs, tn), lambda j: (0, j + off * nj))
    act, ug, uu = pl.pallas_call(
        _ffn_in_step_kernel,
        grid=(nj,),
        in_specs=[
            pl.BlockSpec((Bs, K), lambda j: (0, 0)),
            pl.BlockSpec((None, K, tn), lambda j: (0, 0, j)),
            pl.BlockSpec((None, K, tn), lambda j: (0, 0, j)),
            pl.BlockSpec((3, tn), lambda j: (0, j)),
            pl.BlockSpec((3, tn), lambda j: (0, j + nj)),
            pl.BlockSpec((1, tn), lambda j: (0, j)),
            pl.BlockSpec((1, tn), lambda j: (0, j + nj)),
            vec(0), vec(1), vec(2), vec(3),
        ],
        out_specs=[pl.BlockSpec((Bs, tn), lambda j: (0, j))] * 3,
        out_shape=[jax.ShapeDtypeStruct((Bs, F), BF),
                   jax.ShapeDtypeStruct((Bs, F), F32),
                   jax.ShapeDtypeStruct((Bs, F), F32)],
        compiler_params=_params(("arbitrary",)),
        name="ffn_in_s",
    )(x, wg3, wu3, w_conv, w_conv, bc, bc, cflat, cflat, cflat, cflat)
    new_cache = jnp.stack([cache[:, 1, :], jnp.concatenate([ug, uu], axis=-1)], axis=1)
    return act, new_cache


def _mlstm_chunk_kernel(q_ref, k_ref, v_ref, o_ref, g_ref, gt_ref, gh_ref,
                        hs_ref, C_ref, n_ref, m_ref, *, hb):
    hblk = pl.program_id(1)
    c = pl.program_id(2)
    L = q_ref.shape[0]

    @pl.when(c == 0)
    def _():
        C_ref[...] = jnp.zeros_like(C_ref)
        n_ref[...] = jnp.zeros_like(n_ref)
        m_ref[...] = jnp.zeros_like(m_ref)

    g = g_ref[...]
    gt = gt_ref[...]
    lane = lax.broadcasted_iota(jnp.int32, g.shape, 1)
    sub = lax.broadcasted_iota(jnp.int32, gt.shape, 0)
    t_idx = lax.broadcasted_iota(jnp.int32, (L, L), 0)
    s_idx = lax.broadcasted_iota(jnp.int32, (L, L), 1)
    causal = s_idx <= t_idx
    q_scale = jnp.asarray(M_QK ** -0.5, BF)

    for hh_i in range(hb):
        h = hblk * hb + hh_i
        ic_col = jnp.sum(jnp.where(lane == h, g, 0.0), axis=1, keepdims=True)
        fc_col = jnp.sum(jnp.where(lane == h + M_HEADS, g, 0.0), axis=1, keepdims=True)
        ic_row = jnp.sum(jnp.where(sub == h, gt, 0.0), axis=0, keepdims=True)
        fc_row = jnp.sum(jnp.where(sub == h + M_HEADS, gt, 0.0), axis=0, keepdims=True)
        b_col = jnp.sum(jnp.where(causal, fc_row, 0.0), axis=1, keepdims=True)
        b_row = jnp.sum(jnp.where(t_idx <= s_idx, fc_col, 0.0), axis=0, keepdims=True)

        m_old = m_ref[hh_i][0:1, 0:1]
        d = jnp.where(causal, b_col - b_row + ic_row, -jnp.inf)
        inter = b_col + m_old
        mt = jnp.maximum(inter, jnp.max(d, axis=1, keepdims=True))
        w = jnp.exp(d - mt)
        si = jnp.exp(inter - mt)

        qb = q_ref[:, hh_i * M_QK:(hh_i + 1) * M_QK] * q_scale
        kb = k_ref[:, hh_i * M_QK:(hh_i + 1) * M_QK]
        vb = v_ref[:, hh_i * M_V:(hh_i + 1) * M_V]
        C_old = C_ref[hh_i]
        n_old = n_ref[hh_i]
        qk = lax.dot_general(qb, kb, (((1,), (1,)), ((), ())), preferred_element_type=F32) * w
        num = (jnp.dot(qk.astype(BF), vb, preferred_element_type=F32)
               + si * jnp.dot(qb, C_old.astype(BF), preferred_element_type=F32))
        den = jnp.sum(qk, axis=1, keepdims=True) + si * jnp.sum(qb.astype(F32) * n_old, axis=1, keepdims=True)
        hh = num / jnp.maximum(jnp.abs(den), jnp.exp(-mt))

        b_last = b_col[L - 1:L, :]
        gl = b_last - b_col + ic_col
        m_new = jnp.maximum(b_last + m_old, jnp.max(gl, axis=0, keepdims=True))
        wl = jnp.exp(gl - m_new)
        sd = jnp.exp(b_last + m_old - m_new)
        kw = kb.astype(F32) * wl
        C_ref[hh_i] = sd * C_old + jnp.dot(kw.T.astype(BF), vb, preferred_element_type=F32)
        n_ref[hh_i] = sd * n_old + jnp.sum(kw, axis=0, keepdims=True)
        m_ref[hh_i] = jnp.broadcast_to(m_new, (1, LANES))

        hn = hh * lax.rsqrt(jnp.mean(hh * hh, axis=1, keepdims=True) + EPS)
        o = o_ref[:, hh_i * M_V:(hh_i + 1) * M_V].astype(F32)
        hs_ref[:, hh_i * M_V:(hh_i + 1) * M_V] = (
            hn * gh_ref[:, hh_i * M_V:(hh_i + 1) * M_V] * _sigmoid(o)).astype(hs_ref.dtype)


def mlstm_prompt(z, gates, g_head, *, B, T, L, hb):
    M = B * T
    nc = T // L
    H = M_HEADS
    gt = jnp.swapaxes(gates[:, :2 * H].reshape(B * nc, L, 2 * H), 1, 2)
    row = lambda b, h, c: b * nc + c
    nhb = H // hb
    kq = nhb
    kv = nhb
    ko = 2 * nhb
    hs, C, n, m = pl.pallas_call(
        functools.partial(_mlstm_chunk_kernel, hb=hb),
        grid=(B, nhb, nc),
        in_specs=[
            pl.BlockSpec((L, hb * M_QK), lambda b, h, c: (row(b, h, c), h)),
            pl.BlockSpec((L, hb * M_QK), lambda b, h, c: (row(b, h, c), kq + h)),
            pl.BlockSpec((L, hb * M_V), lambda b, h, c: (row(b, h, c), kv + h)),
            pl.BlockSpec((L, hb * M_V), lambda b, h, c: (row(b, h, c), ko + h)),
            pl.BlockSpec((L, LANES), lambda b, h, c: (row(b, h, c), 0)),
            pl.BlockSpec((None, 2 * H, L), lambda b, h, c: (row(b, h, c), 0, 0)),
            pl.BlockSpec((1, hb * M_V), lambda b, h, c: (0, h)),
        ],
        out_specs=[
            pl.BlockSpec((L, hb * M_V), lambda b, h, c: (row(b, h, c), h)),
            pl.BlockSpec((None, hb, M_QK, M_V), lambda b, h, c: (b, h, 0, 0)),
            pl.BlockSpec((None, hb, 1, M_QK), lambda b, h, c: (b, h, 0, 0)),
            pl.BlockSpec((None, hb, 1, LANES), lambda b, h, c: (b, h, 0, 0)),
        ],
        out_shape=[
            jax.ShapeDtypeStruct((M, D_MODEL), BF),
            jax.ShapeDtypeStruct((B, M_HEADS, M_QK, M_V), F32),
            jax.ShapeDtypeStruct((B, M_HEADS, 1, M_QK), F32),
            jax.ShapeDtypeStruct((B, M_HEADS, 1, LANES), F32),
        ],
        compiler_params=_params(("arbitrary", "arbitrary", "arbitrary")),
        name="mlstm_p",
    )(z, z, z, z, gates, gt, g_head.reshape(1, D_MODEL))
    return hs, C, n[:, :, 0, :], m[:, :, 0, 0]


def _mlstm_step_kernel(q_ref, k_ref, v_ref, o_ref, ig_ref, lf_ref, m0_ref, n0_ref, C0_ref, gh_ref,
                       hs_ref, C_ref, n_ref, m_ref):
    ic = ig_ref[...]
    fc = lf_ref[...]
    m_old = m0_ref[...]
    inter = fc + m_old
    mt = jnp.maximum(inter, ic)
    w = jnp.exp(ic - mt)
    si = jnp.exp(inter - mt)
    q = (q_ref[...] * (M_QK ** -0.5)).astype(BF).astype(F32)
    k = k_ref[...]
    v = v_ref[...]
    n_old = n0_ref[...]
    qk = jnp.sum(q * k.astype(BF).astype(F32), axis=1, keepdims=True) * w
    wl = jnp.exp(ic - mt)
    sd = jnp.exp(inter - mt)
    kw = k * wl
    qT = q.T
    kwT = kw.T
    rows = []
    for h in range(M_HEADS):
        C_old = C0_ref[h]
        rows.append(jnp.sum(qT[:, h:h + 1] * C_old, axis=0, keepdims=True))
        C_ref[h] = sd[h:h + 1, :] * C_old + kwT[:, h:h + 1] * v[h:h + 1, :]
    qC = jnp.concatenate(rows, axis=0)
    num = qk.astype(BF).astype(F32) * v.astype(BF).astype(F32) + si * qC
    den = qk + si * jnp.sum(q * n_old, axis=1, keepdims=True)
    hh = num / jnp.maximum(jnp.abs(den), jnp.exp(-mt))
    n_ref[...] = sd * n_old + kw
    m_ref[...] = mt
    hn = hh * lax.rsqrt(jnp.mean(hh * hh, axis=1, keepdims=True) + EPS)
    hs_ref[...] = (hn * gh_ref[...] * _sigmoid(o_ref[...])).astype(hs_ref.dtype)


def mlstm_step(z, gates, g_head, C0, n0, m0):
    Bs = z.shape[0]
    H = M_HEADS
    qd, vd = H * M_QK, H * M_V
    q = z[:, :qd].reshape(Bs, H, M_QK)
    k = z[:, qd:2 * qd].reshape(Bs, H, M_QK)
    v = z[:, 2 * qd:2 * qd + vd].reshape(Bs, H, M_V)
    o = z[:, 2 * qd + vd:].reshape(Bs, H, M_V)
    ig = gates[:, :H].reshape(Bs, H, 1)
    lf = gates[:, H:2 * H].reshape(Bs, H, 1)
    per_b = lambda *tail: pl.BlockSpec((None,) + tail, lambda b: (b,) + (0,) * len(tail))
    hs, C, n, m = pl.pallas_call(
        _mlstm_step_kernel,
        grid=(Bs,),
        in_specs=[
            per_b(H, M_QK), per_b(H, M_QK), per_b(H, M_V), per_b(H, M_V),
            per_b(H, 1), per_b(H, 1), per_b(H, 1),
            pl.BlockSpec((None, None, H, M_QK), lambda b: (0, b, 0, 0)),
            pl.BlockSpec((None, None, H, M_QK, M_V), lambda b: (0, b, 0, 0, 0)),
            pl.BlockSpec((H, M_V), lambda b: (0, 0)),
        ],
        out_specs=[
            per_b(H, M_V),
            pl.BlockSpec((None, None, H, M_QK, M_V), lambda b: (0, b, 0, 0, 0)),
            pl.BlockSpec((None, None, H, M_QK), lambda b: (0, b, 0, 0)),
            per_b(H, 1),
        ],
        out_shape=[
            jax.ShapeDtypeStruct((Bs, H, M_V), BF),
            jax.ShapeDtypeStruct(C0.shape, F32),
            jax.ShapeDtypeStruct(n0.shape, F32),
            jax.ShapeDtypeStruct((Bs, H, 1), F32),
        ],
        compiler_params=_params(("arbitrary",)),
        name="mlstm_s",
    )(q, k, v, o, ig, lf, m0.reshape(Bs, H, 1), n0, C0, g_head.reshape(H, M_V))
    return hs.reshape(Bs, D_MODEL), C, n, m.reshape(1, Bs, H)


def _attn_kernel(sink_ref, q_ref, kc_ref, kp_ref, vc_ref, vp_ref, o_ref):
    n = pl.program_id(1)
    W = WINDOW
    R = GROUP * W
    kp = kp_ref[...].astype(BF)
    kc = kc_ref[...].astype(BF)
    vpT = vp_ref[...].T.astype(BF)
    vcT = vc_ref[...].T.astype(BF)
    j_idx = lax.broadcasted_iota(jnp.int32, (W, R), 0)
    i_idx = lax.broadcasted_iota(jnp.int32, (W, R), 1) % W
    from_prev = j_idx > i_idx
    lane_g = lax.broadcasted_iota(jnp.int32, (1, R), 1) // W
    no_prev = jnp.where(n > 0, 0.0, -jnp.inf)
    nt = (((1,), (1,)), ((), ()))
    for h in range(KV_HEADS):
        hs = slice(h * HEAD_DIM, (h + 1) * HEAD_DIM)
        qh = jnp.concatenate(
            [q_ref[:, (h * GROUP + g) * HEAD_DIM:(h * GROUP + g + 1) * HEAD_DIM] for g in range(GROUP)], axis=0)
        sp = lax.dot_general(kp[:, hs], qh, nt, preferred_element_type=F32)
        sc = lax.dot_general(kc[:, hs], qh, nt, preferred_element_type=F32)
        s = jnp.where(from_prev, sp + no_prev, sc)
        sk = jnp.zeros((1, R), F32)
        for g in range(GROUP):
            sk = jnp.where(lane_g == g, sink_ref[h * GROUP + g], sk)
        mx = jnp.maximum(jnp.max(s, axis=0, keepdims=True), sk)
        p = jnp.exp(s - mx)
        p = p / (jnp.sum(p, axis=0, keepdims=True) + jnp.exp(sk - mx))
        oT = (jnp.dot(vpT[hs, :], jnp.where(from_prev, p, 0.0).astype(BF), preferred_element_type=F32)
              + jnp.dot(vcT[hs, :], jnp.where(from_prev, 0.0, p).astype(BF), preferred_element_type=F32))
        for g in range(GROUP):
            c0 = (h * GROUP + g) * HEAD_DIM
            o_ref[:, c0:c0 + HEAD_DIM] = oT[:, g * W:(g + 1) * W].T.astype(o_ref.dtype)


def attention_prompt(q, k, v, sinks, *, B, T):
    M = B * T
    W = WINDOW
    nb = T // W
    kvd = KV_HEADS * HEAD_DIM
    cur = lambda b, n: (b * nb + n, 0)
    prev = lambda b, n: (b * nb + jnp.maximum(n - 1, 0), 0)
    return pl.pallas_call(
        _attn_kernel,
        grid=(B, nb),
        in_specs=[
            pl.BlockSpec(memory_space=pltpu.SMEM),
            pl.BlockSpec((W, D_MODEL), cur),
            pl.BlockSpec((W, kvd), cur), pl.BlockSpec((W, kvd), prev),
            pl.BlockSpec((W, kvd), cur), pl.BlockSpec((W, kvd), prev),
        ],
        out_specs=pl.BlockSpec((W, D_MODEL), cur),
        out_shape=jax.ShapeDtypeStruct((M, D_MODEL), BF),
        compiler_params=_params(("arbitrary", "arbitrary")),
        name="attn_p",
    )(sinks, q, k, k, v, v)


def _per_head_rows(row):
    return jnp.concatenate(
        [jnp.broadcast_to(row[:, h * HEAD_DIM:(h + 1) * HEAD_DIM], (GROUP, HEAD_DIM)) for h in range(KV_HEADS)],
        axis=0)


def _attn_step_kernel(q_ref, kc_ref, vc_ref, kn_ref, vn_ref, sk_ref, o_ref, ko_ref, vo_ref, *, bb):
    W = WINDOW
    jl = lax.broadcasted_iota(jnp.int32, (A_HEADS, W), 1)
    sk = sk_ref[...]
    heads = lambda a, h: a[h * GROUP:(h + 1) * GROUP, :]
    scores = []
    for b in range(bb):
        q = q_ref[b]
        kcb = kc_ref[b].astype(BF)
        s = jnp.concatenate(
            [lax.dot_general(heads(q, h).astype(BF), kcb[:, h * HEAD_DIM:(h + 1) * HEAD_DIM],
                             (((1,), (1,)), ((), ())), preferred_element_type=F32) for h in range(KV_HEADS)],
            axis=0)
        kne = _per_head_rows(kn_ref[b]).astype(BF).astype(F32)
        sn = jnp.sum(q.astype(BF).astype(F32) * kne, axis=1, keepdims=True)
        scores.append((s, sn))
    probs = []
    for s, sn in scores:
        s = jnp.where(jl >= 1, s, -jnp.inf)
        mx = jnp.maximum(jnp.maximum(jnp.max(s, axis=1, keepdims=True), sn), sk)
        p = jnp.exp(s - mx)
        pn = jnp.exp(sn - mx)
        den = jnp.sum(p, axis=1, keepdims=True) + pn + jnp.exp(sk - mx)
        probs.append((p / den, pn / den))
    for b in range(bb):
        p, pn = probs[b]
        vcb = vc_ref[b].astype(BF)
        o = jnp.concatenate(
            [jnp.dot(heads(p, h).astype(BF), vcb[:, h * HEAD_DIM:(h + 1) * HEAD_DIM],
                     preferred_element_type=F32) for h in range(KV_HEADS)], axis=0)
        vne = _per_head_rows(vn_ref[b]).astype(BF).astype(F32)
        o_ref[b] = o + pn.astype(BF).astype(F32) * vne
        ko_ref[b, 0:W - 1, :] = kc_ref[b, 1:W, :]
        ko_ref[b, W - 1:W, :] = kn_ref[b]
        vo_ref[b, 0:W - 1, :] = vc_ref[b, 1:W, :]
        vo_ref[b, W - 1:W, :] = vn_ref[b]


def attention_step(q, k_new, v_new, kbuf, vbuf, sinks):
    Bs = q.shape[0]
    W = WINDOW
    kvd = KV_HEADS * HEAD_DIM
    bb = 4
    assert Bs % bb == 0
    per_b = lambda *tail: pl.BlockSpec((bb,) + tail, lambda b: (b,) + (0,) * len(tail))
    o, ko, vo = pl.pallas_call(
        functools.partial(_attn_step_kernel, bb=bb),
        grid=(Bs // bb,),
        in_specs=[per_b(A_HEADS, HEAD_DIM), per_b(W, kvd), per_b(W, kvd), per_b(1, kvd), per_b(1, kvd),
                  pl.BlockSpec((A_HEADS, 1), lambda b: (0, 0))],
        out_specs=[per_b(A_HEADS, HEAD_DIM), per_b(W, kvd), per_b(W, kvd)],
        out_shape=[jax.ShapeDtypeStruct((Bs, A_HEADS, HEAD_DIM), F32),
                   jax.ShapeDtypeStruct((Bs, W, kvd), F32),
                   jax.ShapeDtypeStruct((Bs, W, kvd), F32)],
        compiler_params=_params(("arbitrary",)),
        name="attn_s",
    )(q.reshape(Bs, A_HEADS, HEAD_DIM), kbuf.reshape(Bs, W, kvd), vbuf.reshape(Bs, W, kvd),
      k_new.reshape(Bs, 1, kvd), v_new.reshape(Bs, 1, kvd), sinks.reshape(A_HEADS, 1))
    return (o.reshape(Bs, D_MODEL), ko.reshape(Bs, W, KV_HEADS, HEAD_DIM), vo.reshape(Bs, W, KV_HEADS, HEAD_DIM))


def _rope_tables(pos, width):
    half = HEAD_DIM // 2
    freq = ROPE_THETA ** (-jnp.arange(half, dtype=F32) / half)
    ang = pos.astype(F32)[:, None] * freq[None, :]
    cos = jnp.cos(ang)
    sin = jnp.sin(ang)
    reps = width // HEAD_DIM
    return (jnp.tile(jnp.concatenate([cos, cos], axis=1), (1, reps)),
            jnp.tile(jnp.concatenate([-sin, sin], axis=1), (1, reps)))


def _tiles(T):
    if T > 1:
        return dict(tm=1024, tn=1024, n_sub=2,
                    tm_r=512, tn_r=1024, n_sub_r=1,
                    tn_kv=512, tn_ffi=256, n_sub_ffi=2,
                    tm_ffo=512, tn_ffo=512, n_sub_ffo=1, tm_mod=256)
    return dict(tm=None, tn=1024, n_sub=1, tm_r=None, tn_r=1024, n_sub_r=1, tn_kv=512,
                tn_ffi=256, n_sub_ffi=1, tm_ffo=None, tn_ffo=512, n_sub_ffo=1, tm_mod=None)


def _run(x, ada, pos, state, P, *, B, T, wq_in=None):
    D = D_MODEL
    M = B * T
    c = {k: (M if v is None else v) for k, v in _tiles(T).items()}
    tm, tn, tm_r, tn_r, tm_mod = c["tm"], c["tn"], c["tm_r"], c["tn_r"], c["tm_mod"]
    cos, sin = _rope_tables(pos, LANES)
    if T > 1:
        tps = T // tm
        rope_spec = pl.BlockSpec((tm, LANES), lambda j, i: (i % tps, 0))
    else:
        rope_spec = pl.BlockSpec((1, LANES), lambda j, i: (0, 0))

    out = {}
    tag = "_p" if T > 1 else "_s"
    wq = {}

    def lin(key, x_, w3, lead, **kw):
        if wq_in is None:
            res, wq[key] = linear(x_, w3, lead, keep_bf16=True, **kw)
            return res
        kw.pop("col_off", None)
        return linear(x_, wq_in[key], 0, **kw)

    sh1, sc1, ga1, sh2, sc2, ga2 = ada[0]
    hn = modulate(x, P["g_norm1"][0], sh1, sc1, T=T, tm=tm_mod)
    qd, vd = M_HEADS * M_QK, M_HEADS * M_V
    w_inT = jnp.swapaxes(P["w_m_in"], 1, 2)
    z = lin("m_in", hn, w_inT, 0, name="m_in" + tag, n_cols=2 * qd + 2 * vd, tm=tm, tn=tn, n_sub=c["n_sub"],
               epilogue=_ep_plain, transposed=True, out_dtype=BF if T > 1 else F32)
    wgT = jnp.pad(w_inT[:, 2 * qd + 2 * vd:, :], ((0, 0), (0, LANES - 2 * M_HEADS), (0, 0)))
    bg = jnp.pad(jnp.concatenate([P["b_m_i"][0], P["b_m_f"][0]]), (0, LANES - 2 * M_HEADS))
    gates = linear(hn, wgT, 0, name="m_gates" + tag, n_cols=LANES, tm=tm, tn=LANES, epilogue=_ep_gates,
                   extras=[_row_vec(bg, LANES)], transposed=True)
    if T > 1:
        hs, C, n, m = mlstm_prompt(z, gates, P["g_m_head"][0], B=B, T=T, L=256, hb=4)
        out["C"], out["n"], out["m"] = C[None], n[None], m[None]
    else:
        hs, out["C"], out["n"], out["m"] = mlstm_step(z, gates, P["g_m_head"][0], *state["mlstm"])
    x = lin("m_out", hs, P["w_m_out"], 0, name="m_out" + tag, n_cols=D, tm=tm_r, tn=tn_r, n_sub=c["n_sub_r"],
               epilogue=_ep_resid, extras=[_tile(x, tn_r, T, tm_r), _seq_vec(ga1, tn_r, T, tm_r)])

    convs = []
    for l in range(2):
        if l == 1:
            sh1, sc1, ga1, sh2, sc2, ga2 = ada[1]
            shk, sck = ada[2]
            hk = modulate(x, P["g_kv"], shk, sck, T=T, tm=tm_mod)
            kvd = KV_HEADS * HEAD_DIM
            tn_kv = c["tn_kv"]
            k = lin("kv_k", hk, P["w_kv"], 0, name="kv_k" + tag, n_cols=kvd, tm=tm, tn=tn_kv, n_sub=c["n_sub"],
                       epilogue=functools.partial(_ep_rope, scale=1.0),
                       extras=[_row_vec(P["b_kv"][:kvd], tn_kv), (cos, rope_spec), (sin, rope_spec)])
            v = lin("kv_v", hk, P["w_kv"], 0, name="kv_v" + tag, n_cols=kvd, tm=tm, tn=tn_kv, n_sub=c["n_sub"],
                       col_off=kvd // tn_kv, epilogue=_ep_bias, extras=[_row_vec(P["b_kv"][kvd:], tn_kv)])
            hn = modulate(x, P["g_norm1"][1], sh1, sc1, T=T, tm=tm_mod)
            q = lin("attn_q", hn, P["w_q"], 0, name="attn_q" + tag, n_cols=D, tm=tm, tn=tn, n_sub=c["n_sub"],
                       epilogue=functools.partial(_ep_rope, scale=HEAD_DIM ** -0.5),
                       extras=[_row_vec(P["b_q"][0], tn), (cos, rope_spec), (sin, rope_spec)],
                       out_dtype=BF if T > 1 else F32)
            if T > 1:
                o = attention_prompt(q, k, v, P["sinks"][0], B=B, T=T)
                last = lambda a: a.reshape(B, T, kvd)[:, T - WINDOW:, :].reshape(B, WINDOW, KV_HEADS, HEAD_DIM)
                out["k_win"], out["v_win"] = last(k), last(v)
            else:
                o, out["k_win"], out["v_win"] = attention_step(q, k, v, state["kbuf"], state["vbuf"], P["sinks"][0])
            x = lin("attn_o", o, P["w_o"], 0, name="attn_o" + tag, n_cols=D, tm=tm_r, tn=tn_r, n_sub=c["n_sub_r"],
                       epilogue=_ep_resid_bias,
                       extras=[_row_vec(P["b_o"][0], tn_r), _tile(x, tn_r, T, tm_r), _seq_vec(ga1, tn_r, T, tm_r)])

        il = T > 1
        hn = modulate(x, P["g_norm2"][l], sh2, sc2, T=T, tm=PERM_GROUP if il else tm_mod, interleave=il)
        if T > 1:
            act, cb, wq["ffn_in%d" % l] = ffn_in_prompt(hn, P["w_ffn_in"], l, P["w_conv"][l], P["b_conv"][l], B=B, T=T,
                                                        tm=tm, tn=c["tn_ffi"], n_sub=c["n_sub_ffi"])
        else:
            act, cb = ffn_in_step(hn, *wq_in["ffn_in%d" % l], P["w_conv"][l], P["b_conv"][l], state["conv"][l],
                                  tn=c["tn_ffi"])
        convs.append(cb)
        tm_o, tn_o = c["tm_ffo"], c["tn_ffo"]
        x = lin("ffn_out%d" % l, act, P["w_ffn_out"], l, name="ffn_out" + tag, n_cols=D, tm=tm_o, tn=tn_o, n_sub=c["n_sub_ffo"],
                   epilogue=_ep_resid, extras=[_tile(x, tn_o, T, tm_o), _seq_vec(ga2, tn_o, T, tm_o)],
                   rows_interleaved=il)

    out["conv"] = jnp.stack(convs)
    out["y"] = rmsnorm(x, P["g_final"], tm=tm_mod)
    out["wq"] = wq
    return out


def kernel(x_prompt, x_sample, state_mlstm_C, state_mlstm_n, state_mlstm_m, cache_conv, cache_k_win, cache_v_win, c_prompt, c_sample, w_ada, g_norm1, g_norm2, w_m_in, b_m_i, b_m_f, g_m_head, w_m_out, w_ada_kv, g_kv, w_kv, b_kv, w_q, b_q, sinks, w_o, b_o, w_ffn_in, w_conv, b_conv, w_ffn_out, g_final):
    D = D_MODEL
    Bp, Tp, _ = x_prompt.shape
    Bs, Ts, _ = x_sample.shape
    assert Ts == 1
    P = dict(g_norm1=g_norm1, g_norm2=g_norm2, w_m_in=w_m_in, b_m_i=b_m_i, b_m_f=b_m_f, g_m_head=g_m_head,
             w_m_out=w_m_out, g_kv=g_kv, w_kv=w_kv[None], b_kv=b_kv, w_q=w_q, b_q=b_q, sinks=sinks, w_o=w_o,
             b_o=b_o, w_ffn_in=w_ffn_in, w_conv=w_conv, b_conv=b_conv, w_ffn_out=w_ffn_out, g_final=g_final)

    n_c = Bp + Bs
    pad_c = -n_c % 16
    cs = silu_cast(jnp.concatenate([c_prompt, c_sample, jnp.zeros((pad_c, D), F32)], axis=0))
    rows_c = n_c + pad_c
    ada_all = []
    for l in range(2):
        a = linear(cs, w_ada, l, name="ada", n_cols=6 * D, tm=rows_c, tn=512, epilogue=_ep_plain)
        ada_all.append(a)
    a_kv = linear(cs, w_ada_kv[None], 0, name="ada_kv", n_cols=2 * D, tm=rows_c, tn=512, epilogue=_ep_plain)

    def split(lo, hi):
        per_layer = [tuple(a[lo:hi, i * D:(i + 1) * D] for i in range(6)) for a in ada_all]
        return per_layer + [(a_kv[lo:hi, :D], a_kv[lo:hi, D:])]

    po = _run(x_prompt.reshape(Bp * Tp, D), split(0, Bp), jnp.arange(Tp, dtype=jnp.int32), None, P, B=Bp, T=Tp)
    state = dict(mlstm=(state_mlstm_C, state_mlstm_n, state_mlstm_m), conv=cache_conv,
                 kbuf=cache_k_win, vbuf=cache_v_win)
    so = _run(x_sample.reshape(Bs, D), split(Bp, Bp + Bs), PAST_LEN + jnp.arange(1, dtype=jnp.int32), state, P,
              B=Bs, T=1, wq_in=po["wq"])
    return (po["y"].reshape(Bp, Tp, D), so["y"].reshape(Bs, 1, D),
            po["C"], po["n"], po["m"], po["conv"], po["k_win"], po["v_win"],
            so["C"], so["n"], so["m"], so["conv"], so["k_win"], so["v_win"])
```

```python
import functools

import jax
import jax.numpy as jnp
from jax import lax
from jax.experimental import pallas as pl
from jax.experimental.pallas import tpu as pltpu

BF = jnp.bfloat16
F32 = jnp.float32

D_MODEL = 4096
M_HEADS = 8
M_QK = 256
M_V = 512
M_CHUNK = 64
GATE_CAP = 15.0
HEAD_DIM = 64
A_HEADS = 64
KV_HEADS = 8
GROUP = 8
WINDOW = 128
ROPE_THETA = 10000.0
D_FF = 11008
PAST_LEN = 16384
EPS = 1e-6

VMEM_LIMIT_V7X = 58 * 1024 * 1024
LANES = 128


def _params(sem):
    return pltpu.CompilerParams(dimension_semantics=sem, vmem_limit_bytes=VMEM_LIMIT_V7X)


def _sigmoid(x):
    return 1.0 / (1.0 + jnp.exp(-x))


def _lag_row(j, i):
    return i * jnp.minimum(j, 1)


def _lag_col(j):
    return jnp.maximum(j - 1, 0)


def _lagged(index_map):
    return lambda j, i: index_map(_lag_col(j), _lag_row(j, i))


def _chunk(j, i, nj, ni):
    last = j // nj
    return jnp.minimum(j, nj - 1), i + last * (ni - 1 - i)


def _stage_weight(i, w_ref, wb_ref, col0=0, wq_ref=None):
    ck, cw = w_ref.shape
    wq = w_ref[...].astype(BF)
    wb_ref[pl.ds(pl.multiple_of(i * ck, ck), ck), col0:col0 + cw] = wq
    if wq_ref is not None:
        wq_ref[...] = wq


def _by_slot(j, stage, step, before=None):
    @pl.when(j == 0)
    def _():
        stage(0)

    for parity in (0, 1):
        @pl.when((j > 0) & (j % 2 == parity))
        def _():
            if before is not None:
                before()
            stage(parity)
            step(1 - parity)


def _linear_kernel(x_ref, w_ref, *rest, n_extra, epilogue, n_sub, transposed, rows_interleaved, keep_bf16):
    extra = rest[:n_extra]
    o_ref = rest[n_extra]
    wq_ref = rest[n_extra + 1] if keep_bf16 else None
    wb_refs = rest[-2:]
    j = pl.program_id(0)
    i = pl.program_id(1)
    tm = x_ref.shape[0]
    sub = tm // n_sub

    def stage(slot):
        _stage_weight(i, w_ref, wb_refs[slot], wq_ref=wq_ref)

    def step(slot):
        w = wb_refs[slot][...]
        for r in range(n_sub):
            rows = slice(r * sub, (r + 1) * sub)
            xs = x_ref[rows, :].astype(BF)
            if transposed:
                acc = lax.dot_general(xs, w, (((1,), (1,)), ((), ())), preferred_element_type=F32)
            else:
                acc = jnp.dot(xs, w, preferred_element_type=F32)
            if rows_interleaved:
                assert sub == PERM_GROUP
                acc = pltpu.einshape("qsd->sqd", acc.reshape(PERM_Q, SUBLANES, acc.shape[1])).reshape(acc.shape)
            ex = [e[rows, :] if e.shape[0] == tm else e[...] for e in extra]
            o_ref[rows, :] = epilogue(acc, *ex).astype(o_ref.dtype)

    _by_slot(j, stage, step)


def linear(x, w3, lead, *, name, n_cols, tm, tn, epilogue, extras=(), out_dtype=F32,
           col_off=0, n_sub=1, transposed=False, rows_interleaved=False, keep_bf16=False):
    M, K = x.shape
    assert M % tm == 0 and n_cols % tn == 0 and tm % n_sub == 0
    nj, ni = n_cols // tn, M // tm
    chunk = lambda j, i: _chunk(j, i, nj, ni)
    if transposed:
        cn = tn // ni
        assert cn * ni == tn and cn % 16 == 0
        w_spec = pl.BlockSpec((None, cn, K),
                              lambda j, i: (lead, (chunk(j, i)[0] + col_off) * ni + chunk(j, i)[1], 0))
        wq_spec = pl.BlockSpec((None, cn, K), lambda j, i: (0, chunk(j, i)[0] * ni + chunk(j, i)[1], 0))
        wq_shape = (1, n_cols, K)
        wb_shape = (tn, K)
    else:
        ck = K // ni
        assert ck * ni == K and ck % 16 == 0
        w_spec = pl.BlockSpec((None, ck, tn), lambda j, i: (lead, chunk(j, i)[1], chunk(j, i)[0] + col_off))
        wq_spec = pl.BlockSpec((None, ck, tn), lambda j, i: (0, chunk(j, i)[1], chunk(j, i)[0]))
        wq_shape = (1, K, n_cols)
        wb_shape = (K, tn)
    in_specs = [pl.BlockSpec((tm, K), lambda j, i: (_lag_row(j, i), 0)), w_spec]
    in_specs += [pl.BlockSpec(s.block_shape, _lagged(s.index_map)) for _, s in extras]
    out_specs = [pl.BlockSpec((tm, tn), lambda j, i: (_lag_row(j, i), _lag_col(j)))]
    out_shape = [jax.ShapeDtypeStruct((M, n_cols), out_dtype)]
    if keep_bf16:
        out_specs.append(wq_spec)
        out_shape.append(jax.ShapeDtypeStruct(wq_shape, BF))
    res = pl.pallas_call(
        functools.partial(_linear_kernel, n_extra=len(extras), epilogue=epilogue, n_sub=n_sub,
                          transposed=transposed, rows_interleaved=rows_interleaved, keep_bf16=keep_bf16),
        grid=(nj + 1, ni),
        in_specs=in_specs,
        out_specs=out_specs,
        out_shape=out_shape,
        scratch_shapes=[pltpu.VMEM(wb_shape, BF)] * 2,
        compiler_params=_params(("arbitrary", "arbitrary")),
        name=name,
    )(x, w3, *[a for a, _ in extras])
    return tuple(res) if keep_bf16 else res[0]


def _row_vec(v, tn):
    return v.reshape(1, -1), pl.BlockSpec((1, tn), lambda j, i: (0, j))


def _seq_vec(a, tn, T, tm):
    if T == 1:
        return a, pl.BlockSpec((tm, tn), lambda j, i: (i, j))
    tps = T // tm
    return a.reshape(a.shape[0], 1, a.shape[1]), pl.BlockSpec((None, 1, tn), lambda j, i: (i // tps, 0, j))


def _tile(a, tn, T, tm, col0=0):
    return a, pl.BlockSpec((tm, tn), lambda j, i: (i, j + col0))


def _ep_plain(acc):
    return acc


def _ep_bias(acc, b):
    return acc + b


def _ep_resid(acc, r, g):
    return r + g * acc


def _ep_resid_bias(acc, b, r, g):
    return r + g * (acc + b)


def _ep_rope(acc, b, cos, sin, *, scale):
    y = acc + b
    n = y.shape[1]
    cos = jnp.concatenate([cos] * (n // LANES), axis=1)
    sin = jnp.concatenate([sin] * (n // LANES), axis=1)
    lane = lax.broadcasted_iota(jnp.int32, y.shape, 1)
    first_half = (lane % HEAD_DIM) < (HEAD_DIM // 2)
    partner = jnp.where(first_half, pltpu.roll(y, n - HEAD_DIM // 2, axis=1), pltpu.roll(y, HEAD_DIM // 2, axis=1))
    return (y * cos + partner * sin) * scale


def _ep_gates(acc, b):
    z = GATE_CAP * jnp.tanh((acc + b) / GATE_CAP)
    logsig = jnp.minimum(z, 0.0) - jnp.log(1.0 + jnp.exp(-jnp.abs(z)))
    lane = lax.broadcasted_iota(jnp.int32, z.shape, 1)
    return jnp.where(lane < M_HEADS, z, logsig)


def _silu_kernel(c_ref, o_ref):
    c = c_ref[...]
    o_ref[...] = (c * _sigmoid(c)).astype(o_ref.dtype)


def silu_cast(c):
    return pl.pallas_call(
        _silu_kernel,
        out_shape=jax.ShapeDtypeStruct(c.shape, BF),
        name="silu_c",
    )(c)


ROW_CHUNK = 16


def _modulate_kernel(x_ref, g_ref, sh_ref, sc_ref, o_ref):
    tm = x_ref.shape[0]
    per_row = sh_ref.shape[0] == tm
    g = g_ref[...]
    if not per_row:
        scale_all = 1.0 + sc_ref[...]
        shift_all = sh_ref[...]

    def body(c, carry):
        r = pl.ds(pl.multiple_of(c * ROW_CHUNK, ROW_CHUNK), ROW_CHUNK)
        x = x_ref[r, :]
        y = x * lax.rsqrt(jnp.mean(x * x, axis=-1, keepdims=True) + EPS) * g
        scale = 1.0 + sc_ref[r, :] if per_row else scale_all
        shift = sh_ref[r, :] if per_row else shift_all
        o_ref[r, :] = (y * scale + shift).astype(o_ref.dtype)
        return carry

    n_chunks = tm // ROW_CHUNK
    lax.fori_loop(0, n_chunks, body, 0, unroll=min(4, n_chunks))


def _modulate_interleave_kernel(x_ref, g_ref, sh_ref, sc_ref, o_ref):
    D = x_ref.shape[1]
    x = x_ref[...]
    rstd = lax.rsqrt(jnp.mean(x * x, axis=-1, keepdims=True) + EPS)
    for c in range(D // LANES):
        cols = slice(c * LANES, (c + 1) * LANES)
        y = x_ref[:, cols] * rstd * g_ref[:, cols]
        y = y * (1.0 + sc_ref[:, cols]) + sh_ref[:, cols]
        y = pltpu.einshape("sqd->qsd", y.reshape(SUBLANES, PERM_Q, LANES)).reshape(PERM_GROUP, LANES)
        o_ref[:, cols] = y.astype(o_ref.dtype)


def modulate(x, g, sh, sc, *, T, tm, interleave=False):
    M, D = x.shape
    assert not interleave or (tm == PERM_GROUP and T > 1)
    sh_a, sh_s = _seq_vec(sh, D, T, tm)
    sc_a, sc_s = _seq_vec(sc, D, T, tm)
    fix = lambda s: pl.BlockSpec(s.block_shape, functools.partial(lambda im, i: im(0, i), s.index_map))
    return pl.pallas_call(
        _modulate_interleave_kernel if interleave else _modulate_kernel,
        grid=(M // tm,),
        in_specs=[pl.BlockSpec((tm, D), lambda i: (i, 0)),
                  pl.BlockSpec((1, D), lambda i: (0, 0)),
                  fix(sh_s), fix(sc_s)],
        out_specs=pl.BlockSpec((tm, D), lambda i: (i, 0)),
        out_shape=jax.ShapeDtypeStruct((M, D), BF),
        compiler_params=_params(("arbitrary",)),
        name="modulate",
    )(x, g.reshape(1, D), sh_a, sc_a)


def _rmsnorm_kernel(x_ref, g_ref, o_ref):
    x = x_ref[...]
    o_ref[...] = x * lax.rsqrt(jnp.mean(x * x, axis=-1, keepdims=True) + EPS) * g_ref[...]


def rmsnorm(x, g, *, tm):
    M, D = x.shape
    return pl.pallas_call(
        _rmsnorm_kernel,
        grid=(M // tm,),
        in_specs=[pl.BlockSpec((tm, D), lambda i: (i, 0)), pl.BlockSpec((1, D), lambda i: (0, 0))],
        out_specs=pl.BlockSpec((tm, D), lambda i: (i, 0)),
        out_shape=jax.ShapeDtypeStruct((M, D), F32),
        compiler_params=_params(("arbitrary",)),
        name="final_norm",
    )(x, g.reshape(1, D))


def _conv_gate(ug, uu, pg1, pg2, pu1, pu2, wcg, wcu, bcg, bcu):
    yg = bcg + pg2 * wcg[0:1] + pg1 * wcg[1:2] + ug * wcg[2:3]
    yu = bcu + pu2 * wcu[0:1] + pu1 * wcu[1:2] + uu * wcu[2:3]
    return yg * _sigmoid(yg) * yu


SUBLANES = 8
PERM_GROUP = 512
PERM_Q = PERM_GROUP // SUBLANES


def _ffn_in_kernel(x_ref, wg_ref, wu_ref, wcg_ref, wcu_ref, bcg_ref, bcu_ref,
                   act_ref, cg_ref, cu_ref, wqg_ref, wqu_ref, wb0_ref, wb1_ref, car_ref, *, tps, n_sub):
    j = pl.program_id(0)
    i = pl.program_id(1)
    tm = x_ref.shape[0]
    tn = wg_ref.shape[1]
    G = PERM_GROUP
    assert tm == n_sub * G
    wb_refs = (wb0_ref, wb1_ref)

    def stage(slot):
        _stage_weight(i, wg_ref, wb_refs[slot], 0, wqg_ref)
        _stage_weight(i, wu_ref, wb_refs[slot], tn, wqu_ref)

    def step(slot):
        w = wb_refs[slot][...]
        wcg, wcu, bcg, bcu = wcg_ref[...], wcu_ref[...], bcg_ref[...], bcu_ref[...]
        car = car_ref[...]
        first = lax.broadcasted_iota(jnp.int32, (SUBLANES, 2 * tn), 0) == 0
        for r in range(n_sub):
            rows = slice(r * G, (r + 1) * G)
            u = jnp.dot(x_ref[rows, :], w, preferred_element_type=F32)
            fix1 = jnp.where(first, car[1:2], pltpu.roll(u[G - 8:G], 1, axis=0))
            fix2 = jnp.where(first, car[0:1], pltpu.roll(u[G - 16:G - 8], 1, axis=0))
            p1 = jnp.concatenate([fix1, u[:G - 8]], axis=0)
            p2 = jnp.concatenate([fix2, fix1, u[:G - 16]], axis=0)
            act_ref[rows, :] = _conv_gate(u[:, :tn], u[:, tn:], p1[:, :tn], p2[:, :tn], p1[:, tn:], p2[:, tn:],
                                          wcg, wcu, bcg, bcu).astype(act_ref.dtype)
            car = jnp.concatenate([u[G - 9:G - 8], u[G - 1:G]], axis=0)
        car_ref[...] = car
        cg_ref[...] = car[:, :tn]
        cu_ref[...] = car[:, tn:]

    def reset_history():
        @pl.when(i % tps == 0)
        def _():
            car_ref[...] = jnp.zeros_like(car_ref)

    _by_slot(j, stage, step, before=reset_history)


def ffn_in_prompt(x, w3, lead, w_conv, b_conv, *, B, T, tm, tn, n_sub):
    M, K = x.shape
    F = D_FF
    nj = F // tn
    ni = M // tm
    ck = K // ni
    assert ck * ni == K and ck % 16 == 0
    tps = T // tm
    wc = w_conv
    bc = b_conv.reshape(1, 2 * F)
    chunk = lambda j, i: _chunk(j, i, nj, ni)
    wq_spec = pl.BlockSpec((None, ck, tn), lambda j, i: (0, chunk(j, i)[1], chunk(j, i)[0]))
    act, cg, cu, wq_g, wq_u = pl.pallas_call(
        functools.partial(_ffn_in_kernel, tps=tps, n_sub=n_sub),
        grid=(nj + 1, ni),
        in_specs=[
            pl.BlockSpec((tm, K), lambda j, i: (_lag_row(j, i), 0)),
            pl.BlockSpec((None, ck, tn), lambda j, i: (lead, chunk(j, i)[1], chunk(j, i)[0])),
            pl.BlockSpec((None, ck, tn), lambda j, i: (lead, chunk(j, i)[1], chunk(j, i)[0] + nj)),
            pl.BlockSpec((3, tn), lambda j, i: (0, _lag_col(j))),
            pl.BlockSpec((3, tn), lambda j, i: (0, _lag_col(j) + nj)),
            pl.BlockSpec((1, tn), lambda j, i: (0, _lag_col(j))),
            pl.BlockSpec((1, tn), lambda j, i: (0, _lag_col(j) + nj)),
        ],
        out_specs=[
            pl.BlockSpec((tm, tn), lambda j, i: (_lag_row(j, i), _lag_col(j))),
            pl.BlockSpec((None, 2, tn), lambda j, i: (_lag_row(j, i) // tps, 0, _lag_col(j))),
            pl.BlockSpec((None, 2, tn), lambda j, i: (_lag_row(j, i) // tps, 0, _lag_col(j))),
            wq_spec, wq_spec,
        ],
        out_shape=[
            jax.ShapeDtypeStruct((M, F), BF),
            jax.ShapeDtypeStruct((B, 2, F), F32),
            jax.ShapeDtypeStruct((B, 2, F), F32),
            jax.ShapeDtypeStruct((1, K, F), BF),
            jax.ShapeDtypeStruct((1, K, F), BF),
        ],
        scratch_shapes=[pltpu.VMEM((K, 2 * tn), BF), pltpu.VMEM((K, 2 * tn), BF),
                        pltpu.VMEM((2, 2 * tn), F32)],
        compiler_params=_params(("arbitrary", "arbitrary")),
        name="ffn_in_p",
    )(x, w3, w3, wc, wc, bc, bc)
    return act, jnp.concatenate([cg, cu], axis=-1), (wq_g, wq_u)


def _ffn_in_step_kernel(x_ref, wg_ref, wu_ref, wcg_ref, wcu_ref, bcg_ref, bcu_ref,
                        c0g_ref, c0u_ref, c1g_ref, c1u_ref, act_ref, ug_ref, uu_ref):
    x = x_ref[...]
    ug = jnp.dot(x, wg_ref[...].astype(BF), preferred_element_type=F32)
    uu = jnp.dot(x, wu_ref[...].astype(BF), preferred_element_type=F32)
    act_ref[...] = _conv_gate(ug, uu, c1g_ref[...], c0g_ref[...], c1u_ref[...], c0u_ref[...],
                              wcg_ref[...], wcu_ref[...], bcg_ref[...], bcu_ref[...]).astype(act_ref.dtype)
    ug_ref[...] = ug
    uu_ref[...] = uu


def ffn_in_step(x, wg3, wu3, w_conv, b_conv, cache, *, tn):
    Bs, K = x.shape
    F = D_FF
    nj = F // tn
    bc = b_conv.reshape(1, 2 * F)
    cflat = cache.reshape(Bs, 4 * F)
    vec = lambda off: pl.BlockSpec((Bs, tn), lambda j: (0, j + off * nj))
    act, ug, uu = pl.pallas_call(
        _ffn_in_step_kernel,
        grid=(nj,),
        in_specs=[
            pl.BlockSpec((Bs, K), lambda j: (0, 0)),
            pl.BlockSpec((None, K, tn), lambda j: (0, 0, j)),
            pl.BlockSpec((None, K, tn), lambda j: (0, 0, j)),
            pl.BlockSpec((3, tn), lambda j: (0, j)),
            pl.BlockSpec((3, tn), lambda j: (0, j + nj)),
            pl.BlockSpec((1, tn), lambda j: (0, j)),
            pl.BlockSpec((1, tn), lambda j: (0, j + nj)),
            vec(0), vec(1), vec(2), vec(3),
        ],
        out_specs=[pl.BlockSpec((Bs, tn), lambda j: (0, j))] * 3,
        out_shape=[jax.ShapeDtypeStruct((Bs, F), BF),
                   jax.ShapeDtypeStruct((Bs, F), F32),
                   jax.ShapeDtypeStruct((Bs, F), F32)],
        compiler_params=_params(("arbitrary",)),
        name="ffn_in_s",
    )(x, wg3, wu3, w_conv, w_conv, bc, bc, cflat, cflat, cflat, cflat)
    new_cache = jnp.stack([cache[:, 1, :], jnp.concatenate([ug, uu], axis=-1)], axis=1)
    return act, new_cache


def _mlstm_chunk_kernel(q_ref, k_ref, v_ref, o_ref, g_ref, gt_ref, gh_ref,
                        hs_ref, C_ref, n_ref, m_ref, *, hb):
    hblk = pl.program_id(1)
    c = pl.program_id(2)
    L = q_ref.shape[0]

    @pl.when(c == 0)
    def _():
        C_ref[...] = jnp.zeros_like(C_ref)
        n_ref[...] = jnp.zeros_like(n_ref)
        m_ref[...] = jnp.zeros_like(m_ref)

    g = g_ref[...]
    gt = gt_ref[...]
    lane = lax.broadcasted_iota(jnp.int32, g.shape, 1)
    sub = lax.broadcasted_iota(jnp.int32, gt.shape, 0)
    t_idx = lax.broadcasted_iota(jnp.int32, (L, L), 0)
    s_idx = lax.broadcasted_iota(jnp.int32, (L, L), 1)
    causal = s_idx <= t_idx
    q_scale = jnp.asarray(M_QK ** -0.5, BF)

    for hh_i in range(hb):
        h = hblk * hb + hh_i
        ic_col = jnp.sum(jnp.where(lane == h, g, 0.0), axis=1, keepdims=True)
        fc_col = jnp.sum(jnp.where(lane == h + M_HEADS, g, 0.0), axis=1, keepdims=True)
        ic_row = jnp.sum(jnp.where(sub == h, gt, 0.0), axis=0, keepdims=True)
        fc_row = jnp.sum(jnp.where(sub == h + M_HEADS, gt, 0.0), axis=0, keepdims=True)
        b_col = jnp.sum(jnp.where(causal, fc_row, 0.0), axis=1, keepdims=True)
        b_row = jnp.sum(jnp.where(t_idx <= s_idx, fc_col, 0.0), axis=0, keepdims=True)

        m_old = m_ref[hh_i][0:1, 0:1]
        d = jnp.where(causal, b_col - b_row + ic_row, -jnp.inf)
        inter = b_col + m_old
        mt = jnp.maximum(inter, jnp.max(d, axis=1, keepdims=True))
        w = jnp.exp(d - mt)
        si = jnp.exp(inter - mt)

        qb = q_ref[:, hh_i * M_QK:(hh_i + 1) * M_QK] * q_scale
        kb = k_ref[:, hh_i * M_QK:(hh_i + 1) * M_QK]
        vb = v_ref[:, hh_i * M_V:(hh_i + 1) * M_V]
        C_old = C_ref[hh_i]
        n_old = n_ref[hh_i]
        qk = lax.dot_general(qb, kb, (((1,), (1,)), ((), ())), preferred_element_type=F32) * w
        num = (jnp.dot(qk.astype(BF), vb, preferred_element_type=F32)
               + si * jnp.dot(qb, C_old.astype(BF), preferred_element_type=F32))
        den = jnp.sum(qk, axis=1, keepdims=True) + si * jnp.sum(qb.astype(F32) * n_old, axis=1, keepdims=True)
        hh = num / jnp.maximum(jnp.abs(den), jnp.exp(-mt))

        b_last = b_col[L - 1:L, :]
        gl = b_last - b_col + ic_col
        m_new = jnp.maximum(b_last + m_old, jnp.max(gl, axis=0, keepdims=True))
        wl = jnp.exp(gl - m_new)
        sd = jnp.exp(b_last + m_old - m_new)
        kw = kb.astype(F32) * wl
        C_ref[hh_i] = sd * C_old + jnp.dot(kw.T.astype(BF), vb, preferred_element_type=F32)
        n_ref[hh_i] = sd * n_old + jnp.sum(kw, axis=0, keepdims=True)
        m_ref[hh_i] = jnp.broadcast_to(m_new, (1, LANES))

        hn = hh * lax.rsqrt(jnp.mean(hh * hh, axis=1, keepdims=True) + EPS)
        o = o_ref[:, hh_i * M_V:(hh_i + 1) * M_V].astype(F32)
        hs_ref[:, hh_i * M_V:(hh_i + 1) * M_V] = (
            hn * gh_ref[:, hh_i * M_V:(hh_i + 1) * M_V] * _sigmoid(o)).astype(hs_ref.dtype)


def mlstm_prompt(z, gates, g_head, *, B, T, L, hb):
    M = B * T
    nc = T // L
    H = M_HEADS
    gt = jnp.swapaxes(gates[:, :2 * H].reshape(B * nc, L, 2 * H), 1, 2)
    row = lambda b, h, c: b * nc + c
    nhb = H // hb
    kq = nhb
    kv = nhb
    ko = 2 * nhb
    hs, C, n, m = pl.pallas_call(
        functools.partial(_mlstm_chunk_kernel, hb=hb),
        grid=(B, nhb, nc),
        in_specs=[
            pl.BlockSpec((L, hb * M_QK), lambda b, h, c: (row(b, h, c), h)),
            pl.BlockSpec((L, hb * M_QK), lambda b, h, c: (row(b, h, c), kq + h)),
            pl.BlockSpec((L, hb * M_V), lambda b, h, c: (row(b, h, c), kv + h)),
            pl.BlockSpec((L, hb * M_V), lambda b, h, c: (row(b, h, c), ko + h)),
            pl.BlockSpec((L, LANES), lambda b, h, c: (row(b, h, c), 0)),
            pl.BlockSpec((None, 2 * H, L), lambda b, h, c: (row(b, h, c), 0, 0)),
            pl.BlockSpec((1, hb * M_V), lambda b, h, c: (0, h)),
        ],
        out_specs=[
            pl.BlockSpec((L, hb * M_V), lambda b, h, c: (row(b, h, c), h)),
            pl.BlockSpec((None, hb, M_QK, M_V), lambda b, h, c: (b, h, 0, 0)),
            pl.BlockSpec((None, hb, 1, M_QK), lambda b, h, c: (b, h, 0, 0)),
            pl.BlockSpec((None, hb, 1, LANES), lambda b, h, c: (b, h, 0, 0)),
        ],
        out_shape=[
            jax.ShapeDtypeStruct((M, D_MODEL), BF),
            jax.ShapeDtypeStruct((B, M_HEADS, M_QK, M_V), F32),
            jax.ShapeDtypeStruct((B, M_HEADS, 1, M_QK), F32),
            jax.ShapeDtypeStruct((B, M_HEADS, 1, LANES), F32),
        ],
        compiler_params=_params(("arbitrary", "arbitrary", "arbitrary")),
        name="mlstm_p",
    )(z, z, z, z, gates, gt, g_head.reshape(1, D_MODEL))
    return hs, C, n[:, :, 0, :], m[:, :, 0, 0]


def _mlstm_step_kernel(q_ref, k_ref, v_ref, o_ref, ig_ref, lf_ref, m0_ref, n0_ref, C0_ref, gh_ref,
                       hs_ref, C_ref, n_ref, m_ref):
    ic = ig_ref[...]
    fc = lf_ref[...]
    m_old = m0_ref[...]
    inter = fc + m_old
    mt = jnp.maximum(inter, ic)
    w = jnp.exp(ic - mt)
    si = jnp.exp(inter - mt)
    q = (q_ref[...] * (M_QK ** -0.5)).astype(BF).astype(F32)
    k = k_ref[...]
    v = v_ref[...]
    n_old = n0_ref[...]
    qk = jnp.sum(q * k.astype(BF).astype(F32), axis=1, keepdims=True) * w
    wl = jnp.exp(ic - mt)
    sd = jnp.exp(inter - mt)
    kw = k * wl
    qT = q.T
    kwT = kw.T
    rows = []
    for h in range(M_HEADS):
        C_old = C0_ref[h]
        rows.append(jnp.sum(qT[:, h:h + 1] * C_old, axis=0, keepdims=True))
        C_ref[h] = sd[h:h + 1, :] * C_old + kwT[:, h:h + 1] * v[h:h + 1, :]
    qC = jnp.concatenate(rows, axis=0)
    num = qk.astype(BF).astype(F32) * v.astype(BF).astype(F32) + si * qC
    den = qk + si * jnp.sum(q * n_old, axis=1, keepdims=True)
    hh = num / jnp.maximum(jnp.abs(den), jnp.exp(-mt))
    n_ref[...] = sd * n_old + kw
    m_ref[...] = mt
    hn = hh * lax.rsqrt(jnp.mean(hh * hh, axis=1, keepdims=True) + EPS)
    hs_ref[...] = (hn * gh_ref[...] * _sigmoid(o_ref[...])).astype(hs_ref.dtype)


def mlstm_step(z, gates, g_head, C0, n0, m0):
    Bs = z.shape[0]
    H = M_HEADS
    qd, vd = H * M_QK, H * M_V
    q = z[:, :qd].reshape(Bs, H, M_QK)
    k = z[:, qd:2 * qd].reshape(Bs, H, M_QK)
    v = z[:, 2 * qd:2 * qd + vd].reshape(Bs, H, M_V)
    o = z[:, 2 * qd + vd:].reshape(Bs, H, M_V)
    ig = gates[:, :H].reshape(Bs, H, 1)
    lf = gates[:, H:2 * H].reshape(Bs, H, 1)
    per_b = lambda *tail: pl.BlockSpec((None,) + tail, lambda b: (b,) + (0,) * len(tail))
    hs, C, n, m = pl.pallas_call(
        _mlstm_step_kernel,
        grid=(Bs,),
        in_specs=[
            per_b(H, M_QK), per_b(H, M_QK), per_b(H, M_V), per_b(H, M_V),
            per_b(H, 1), per_b(H, 1), per_b(H, 1),
            pl.BlockSpec((None, None, H, M_QK), lambda b: (0, b, 0, 0)),
            pl.BlockSpec((None, None, H, M_QK, M_V), lambda b: (0, b, 0, 0, 0)),
            pl.BlockSpec((H, M_V), lambda b: (0, 0)),
        ],
        out_specs=[
            per_b(H, M_V),
            pl.BlockSpec((None, None, H, M_QK, M_V), lambda b: (0, b, 0, 0, 0)),
            pl.BlockSpec((None, None, H, M_QK), lambda b: (0, b, 0, 0)),
            per_b(H, 1),
        ],
        out_shape=[
            jax.ShapeDtypeStruct((Bs, H, M_V), BF),
            jax.ShapeDtypeStruct(C0.shape, F32),
            jax.ShapeDtypeStruct(n0.shape, F32),
            jax.ShapeDtypeStruct((Bs, H, 1), F32),
        ],
        compiler_params=_params(("arbitrary",)),
        name="mlstm_s",
    )(q, k, v, o, ig, lf, m0.reshape(Bs, H, 1), n0, C0, g_head.reshape(H, M_V))
    return hs.reshape(Bs, D_MODEL), C, n, m.reshape(1, Bs, H)


def _attn_kernel(sink_ref, q_ref, kc_ref, kp_ref, vc_ref, vp_ref, o_ref):
    n = pl.program_id(1)
    W = WINDOW
    R = GROUP * W
    kp = kp_ref[...].astype(BF)
    kc = kc_ref[...].astype(BF)
    vpT = vp_ref[...].T.astype(BF)
    vcT = vc_ref[...].T.astype(BF)
    j_idx = lax.broadcasted_iota(jnp.int32, (W, R), 0)
    i_idx = lax.broadcasted_iota(jnp.int32, (W, R), 1) % W
    from_prev = j_idx > i_idx
    lane_g = lax.broadcasted_iota(jnp.int32, (1, R), 1) // W
    no_prev = jnp.where(n > 0, 0.0, -jnp.inf)
    nt = (((1,), (1,)), ((), ()))
    for h in range(KV_HEADS):
        hs = slice(h * HEAD_DIM, (h + 1) * HEAD_DIM)
        qh = jnp.concatenate(
            [q_ref[:, (h * GROUP + g) * HEAD_DIM:(h * GROUP + g + 1) * HEAD_DIM] for g in range(GROUP)], axis=0)
        sp = lax.dot_general(kp[:, hs], qh, nt, preferred_element_type=F32)
        sc = lax.dot_general(kc[:, hs], qh, nt, preferred_element_type=F32)
        s = jnp.where(from_prev, sp + no_prev, sc)
        sk = jnp.zeros((1, R), F32)
        for g in range(GROUP):
            sk = jnp.where(lane_g == g, sink_ref[h * GROUP + g], sk)
        mx = jnp.maximum(jnp.max(s, axis=0, keepdims=True), sk)
        p = jnp.exp(s - mx)
        p = p / (jnp.sum(p, axis=0, keepdims=True) + jnp.exp(sk - mx))
        oT = (jnp.dot(vpT[hs, :], jnp.where(from_prev, p, 0.0).astype(BF), preferred_element_type=F32)
              + jnp.dot(vcT[hs, :], jnp.where(from_prev, 0.0, p).astype(BF), preferred_element_type=F32))
        for g in range(GROUP):
            c0 = (h * GROUP + g) * HEAD_DIM
            o_ref[:, c0:c0 + HEAD_DIM] = oT[:, g * W:(g + 1) * W].T.astype(o_ref.dtype)


def attention_prompt(q, k, v, sinks, *, B, T):
    M = B * T
    W = WINDOW
    nb = T // W
    kvd = KV_HEADS * HEAD_DIM
    cur = lambda b, n: (b * nb + n, 0)
    prev = lambda b, n: (b * nb + jnp.maximum(n - 1, 0), 0)
    return pl.pallas_call(
        _attn_kernel,
        grid=(B, nb),
        in_specs=[
            pl.BlockSpec(memory_space=pltpu.SMEM),
            pl.BlockSpec((W, D_MODEL), cur),
            pl.BlockSpec((W, kvd), cur), pl.BlockSpec((W, kvd), prev),
            pl.BlockSpec((W, kvd), cur), pl.BlockSpec((W, kvd), prev),
        ],
        out_specs=pl.BlockSpec((W, D_MODEL), cur),
        out_shape=jax.ShapeDtypeStruct((M, D_MODEL), BF),
        compiler_params=_params(("arbitrary", "arbitrary")),
        name="attn_p",
    )(sinks, q, k, k, v, v)


def _per_head_rows(row):
    return jnp.concatenate(
        [jnp.broadcast_to(row[:, h * HEAD_DIM:(h + 1) * HEAD_DIM], (GROUP, HEAD_DIM)) for h in range(KV_HEADS)],
        axis=0)


def _attn_step_kernel(q_ref, kc_ref, vc_ref, kn_ref, vn_ref, sk_ref, o_ref, ko_ref, vo_ref, *, bb):
    W = WINDOW
    jl = lax.broadcasted_iota(jnp.int32, (A_HEADS, W), 1)
    sk = sk_ref[...]
    heads = lambda a, h: a[h * GROUP:(h + 1) * GROUP, :]
    scores = []
    for b in range(bb):
        q = q_ref[b]
        kcb = kc_ref[b].astype(BF)
        s = jnp.concatenate(
            [lax.dot_general(heads(q, h).astype(BF), kcb[:, h * HEAD_DIM:(h + 1) * HEAD_DIM],
                             (((1,), (1,)), ((), ())), preferred_element_type=F32) for h in range(KV_HEADS)],
            axis=0)
        kne = _per_head_rows(kn_ref[b]).astype(BF).astype(F32)
        sn = jnp.sum(q.astype(BF).astype(F32) * kne, axis=1, keepdims=True)
        scores.append((s, sn))
    probs = []
    for s, sn in scores:
        s = jnp.where(jl >= 1, s, -jnp.inf)
        mx = jnp.maximum(jnp.maximum(jnp.max(s, axis=1, keepdims=True), sn), sk)
        p = jnp.exp(s - mx)
        pn = jnp.exp(sn - mx)
        den = jnp.sum(p, axis=1, keepdims=True) + pn + jnp.exp(sk - mx)
        probs.append((p / den, pn / den))
    for b in range(bb):
        p, pn = probs[b]
        vcb = vc_ref[b].astype(BF)
        o = jnp.concatenate(
            [jnp.dot(heads(p, h).astype(BF), vcb[:, h * HEAD_DIM:(h + 1) * HEAD_DIM],
                     preferred_element_type=F32) for h in range(KV_HEADS)], axis=0)
        vne = _per_head_rows(vn_ref[b]).astype(BF).astype(F32)
        o_ref[b] = o + pn.astype(BF).astype(F32) * vne
        ko_ref[b, 0:W - 1, :] = kc_ref[b, 1:W, :]
        ko_ref[b, W - 1:W, :] = kn_ref[b]
        vo_ref[b, 0:W - 1, :] = vc_ref[b, 1:W, :]
        vo_ref[b, W - 1:W, :] = vn_ref[b]


def attention_step(q, k_new, v_new, kbuf, vbuf, sinks):
    Bs = q.shape[0]
    W = WINDOW
    kvd = KV_HEADS * HEAD_DIM
    bb = 4
    assert Bs % bb == 0
    per_b = lambda *tail: pl.BlockSpec((bb,) + tail, lambda b: (b,) + (0,) * len(tail))
    o, ko, vo = pl.pallas_call(
        functools.partial(_attn_step_kernel, bb=bb),
        grid=(Bs // bb,),
        in_specs=[per_b(A_HEADS, HEAD_DIM), per_b(W, kvd), per_b(W, kvd), per_b(1, kvd), per_b(1, kvd),
                  pl.BlockSpec((A_HEADS, 1), lambda b: (0, 0))],
        out_specs=[per_b(A_HEADS, HEAD_DIM), per_b(W, kvd), per_b(W, kvd)],
        out_shape=[jax.ShapeDtypeStruct((Bs, A_HEADS, HEAD_DIM), F32),
                   jax.ShapeDtypeStruct((Bs, W, kvd), F32),
                   jax.ShapeDtypeStruct((Bs, W, kvd), F32)],
        compiler_params=_params(("arbitrary",)),
        name="attn_s",
    )(q.reshape(Bs, A_HEADS, HEAD_DIM), kbuf.reshape(Bs, W, kvd), vbuf.reshape(Bs, W, kvd),
      k_new.reshape(Bs, 1, kvd), v_new.reshape(Bs, 1, kvd), sinks.reshape(A_HEADS, 1))
    return (o.reshape(Bs, D_MODEL), ko.reshape(Bs, W, KV_HEADS, HEAD_DIM), vo.reshape(Bs, W, KV_HEADS, HEAD_DIM))


def _rope_tables(pos, width):
    half = HEAD_DIM // 2
    freq = ROPE_THETA ** (-jnp.arange(half, dtype=F32) / half)
    ang = pos.astype(F32)[:, None] * freq[None, :]
    cos = jnp.cos(ang)
    sin = jnp.sin(ang)
    reps = width // HEAD_DIM
    return (jnp.tile(jnp.concatenate([cos, cos], axis=1), (1, reps)),
            jnp.tile(jnp.concatenate([-sin, sin], axis=1), (1, reps)))


def _tiles(T):
    if T > 1:
        return dict(tm=1024, tn=1024, n_sub=2,
                    tm_r=512, tn_r=1024, n_sub_r=1,
                    tn_kv=512, tm_ffi=2048, tn_ffi=256, n_sub_ffi=4,
                    tm_ffo=512, tn_ffo=512, n_sub_ffo=1, tm_mod=256)
    return dict(tm=None, tn=1024, n_sub=1, tm_r=None, tn_r=1024, n_sub_r=1, tn_kv=512,
                tm_ffi=None, tn_ffi=256, n_sub_ffi=1, tm_ffo=None, tn_ffo=512, n_sub_ffo=1, tm_mod=None)


def _run(x, ada, pos, state, P, *, B, T, wq_in=None):
    D = D_MODEL
    M = B * T
    c = {k: (M if v is None else v) for k, v in _tiles(T).items()}
    tm, tn, tm_r, tn_r, tm_mod = c["tm"], c["tn"], c["tm_r"], c["tn_r"], c["tm_mod"]
    cos, sin = _rope_tables(pos, LANES)
    if T > 1:
        tps = T // tm
        rope_spec = pl.BlockSpec((tm, LANES), lambda j, i: (i % tps, 0))
    else:
        rope_spec = pl.BlockSpec((1, LANES), lambda j, i: (0, 0))

    out = {}
    tag = "_p" if T > 1 else "_s"
    wq = {}

    def lin(key, x_, w3, lead, **kw):
        if wq_in is None:
            res, wq[key] = linear(x_, w3, lead, keep_bf16=True, **kw)
            return res
        kw.pop("col_off", None)
        return linear(x_, wq_in[key], 0, **kw)

    sh1, sc1, ga1, sh2, sc2, ga2 = ada[0]
    hn = modulate(x, P["g_norm1"][0], sh1, sc1, T=T, tm=tm_mod)
    qd, vd = M_HEADS * M_QK, M_HEADS * M_V
    w_inT = jnp.swapaxes(P["w_m_in"], 1, 2)
    z = lin("m_in", hn, w_inT, 0, name="m_in" + tag, n_cols=2 * qd + 2 * vd, tm=tm, tn=tn, n_sub=c["n_sub"],
               epilogue=_ep_plain, transposed=True, out_dtype=BF if T > 1 else F32)
    wgT = jnp.pad(w_inT[:, 2 * qd + 2 * vd:, :], ((0, 0), (0, LANES - 2 * M_HEADS), (0, 0)))
    bg = jnp.pad(jnp.concatenate([P["b_m_i"][0], P["b_m_f"][0]]), (0, LANES - 2 * M_HEADS))
    gates = linear(hn, wgT, 0, name="m_gates" + tag, n_cols=LANES, tm=tm, tn=LANES, epilogue=_ep_gates,
                   extras=[_row_vec(bg, LANES)], transposed=True)
    if T > 1:
        hs, C, n, m = mlstm_prompt(z, gates, P["g_m_head"][0], B=B, T=T, L=256, hb=4)
        out["C"], out["n"], out["m"] = C[None], n[None], m[None]
    else:
        hs, out["C"], out["n"], out["m"] = mlstm_step(z, gates, P["g_m_head"][0], *state["mlstm"])
    x = lin("m_out", hs, P["w_m_out"], 0, name="m_out" + tag, n_cols=D, tm=tm_r, tn=tn_r, n_sub=c["n_sub_r"],
               epilogue=_ep_resid, extras=[_tile(x, tn_r, T, tm_r), _seq_vec(ga1, tn_r, T, tm_r)])

    convs = []
    for l in range(2):
        if l == 1:
            sh1, sc1, ga1, sh2, sc2, ga2 = ada[1]
            shk, sck = ada[2]
            hk = modulate(x, P["g_kv"], shk, sck, T=T, tm=tm_mod)
            kvd = KV_HEADS * HEAD_DIM
            tn_kv = c["tn_kv"]
            k = lin("kv_k", hk, P["w_kv"], 0, name="kv_k" + tag, n_cols=kvd, tm=tm, tn=tn_kv, n_sub=c["n_sub"],
                       epilogue=functools.partial(_ep_rope, scale=1.0),
                       extras=[_row_vec(P["b_kv"][:kvd], tn_kv), (cos, rope_spec), (sin, rope_spec)])
            v = lin("kv_v", hk, P["w_kv"], 0, name="kv_v" + tag, n_cols=kvd, tm=tm, tn=tn_kv, n_sub=c["n_sub"],
                       col_off=kvd // tn_kv, epilogue=_ep_bias, extras=[_row_vec(P["b_kv"][kvd:], tn_kv)])
            hn = modulate(x, P["g_norm1"][1], sh1, sc1, T=T, tm=tm_mod)
            q = lin("attn_q", hn, P["w_q"], 0, name="attn_q" + tag, n_cols=D, tm=tm, tn=tn, n_sub=c["n_sub"],
                       epilogue=functools.partial(_ep_rope, scale=HEAD_DIM ** -0.5),
                       extras=[_row_vec(P["b_q"][0], tn), (cos, rope_spec), (sin, rope_spec)],
                       out_dtype=BF if T > 1 else F32)
            if T > 1:
                o = attention_prompt(q, k, v, P["sinks"][0], B=B, T=T)
                last = lambda a: a.reshape(B, T, kvd)[:, T - WINDOW:, :].reshape(B, WINDOW, KV_HEADS, HEAD_DIM)
                out["k_win"], out["v_win"] = last(k), last(v)
            else:
                o, out["k_win"], out["v_win"] = attention_step(q, k, v, state["kbuf"], state["vbuf"], P["sinks"][0])
            x = lin("attn_o", o, P["w_o"], 0, name="attn_o" + tag, n_cols=D, tm=tm_r, tn=tn_r, n_sub=c["n_sub_r"],
                       epilogue=_ep_resid_bias,
                       extras=[_row_vec(P["b_o"][0], tn_r), _tile(x, tn_r, T, tm_r), _seq_vec(ga1, tn_r, T, tm_r)])

        il = T > 1
        hn = modulate(x, P["g_norm2"][l], sh2, sc2, T=T, tm=PERM_GROUP if il else tm_mod, interleave=il)
        if T > 1:
            act, cb, wq["ffn_in%d" % l] = ffn_in_prompt(hn, P["w_ffn_in"], l, P["w_conv"][l], P["b_conv"][l], B=B, T=T,
                                                        tm=c["tm_ffi"], tn=c["tn_ffi"], n_sub=c["n_sub_ffi"])
        else:
            act, cb = ffn_in_step(hn, *wq_in["ffn_in%d" % l], P["w_conv"][l], P["b_conv"][l], state["conv"][l],
                                  tn=c["tn_ffi"])
        convs.append(cb)
        tm_o, tn_o = c["tm_ffo"], c["tn_ffo"]
        x = lin("ffn_out%d" % l, act, P["w_ffn_out"], l, name="ffn_out" + tag, n_cols=D, tm=tm_o, tn=tn_o, n_sub=c["n_sub_ffo"],
                   epilogue=_ep_resid, extras=[_tile(x, tn_o, T, tm_o), _seq_vec(ga2, tn_o, T, tm_o)],
                   rows_interleaved=il)

    out["conv"] = jnp.stack(convs)
    out["y"] = rmsnorm(x, P["g_final"], tm=tm_mod)
    out["wq"] = wq
    return out


def kernel(x_prompt, x_sample, state_mlstm_C, state_mlstm_n, state_mlstm_m, cache_conv, cache_k_win, cache_v_win, c_prompt, c_sample, w_ada, g_norm1, g_norm2, w_m_in, b_m_i, b_m_f, g_m_head, w_m_out, w_ada_kv, g_kv, w_kv, b_kv, w_q, b_q, sinks, w_o, b_o, w_ffn_in, w_conv, b_conv, w_ffn_out, g_final):
    D = D_MODEL
    Bp, Tp, _ = x_prompt.shape
    Bs, Ts, _ = x_sample.shape
    assert Ts == 1
    P = dict(g_norm1=g_norm1, g_norm2=g_norm2, w_m_in=w_m_in, b_m_i=b_m_i, b_m_f=b_m_f, g_m_head=g_m_head,
             w_m_out=w_m_out, g_kv=g_kv, w_kv=w_kv[None], b_kv=b_kv, w_q=w_q, b_q=b_q, sinks=sinks, w_o=w_o,
             b_o=b_o, w_ffn_in=w_ffn_in, w_conv=w_conv, b_conv=b_conv, w_ffn_out=w_ffn_out, g_final=g_final)

    n_c = Bp + Bs
    pad_c = -n_c % 16
    cs = silu_cast(jnp.concatenate([c_prompt, c_sample, jnp.zeros((pad_c, D), F32)], axis=0))
    rows_c = n_c + pad_c
    ada_all = []
    for l in range(2):
        a = linear(cs, w_ada, l, name="ada", n_cols=6 * D, tm=rows_c, tn=512, epilogue=_ep_plain)
        ada_all.append(a)
    a_kv = linear(cs, w_ada_kv[None], 0, name="ada_kv", n_cols=2 * D, tm=rows_c, tn=512, epilogue=_ep_plain)

    def split(lo, hi):
        per_layer = [tuple(a[lo:hi, i * D:(i + 1) * D] for i in range(6)) for a in ada_all]
        return per_layer + [(a_kv[lo:hi, :D], a_kv[lo:hi, D:])]

    po = _run(x_prompt.reshape(Bp * Tp, D), split(0, Bp), jnp.arange(Tp, dtype=jnp.int32), None, P, B=Bp, T=Tp)
    state = dict(mlstm=(state_mlstm_C, state_mlstm_n, state_mlstm_m), conv=cache_conv,
                 kbuf=cache_k_win, vbuf=cache_v_win)
    so = _run(x_sample.reshape(Bs, D), split(Bp, Bp + Bs), PAST_LEN + jnp.arange(1, dtype=jnp.int32), state, P,
              B=Bs, T=1, wq_in=po["wq"])
    return (po["y"].reshape(Bp, Tp, D), so["y"].reshape(Bs, 1, D),
            po["C"], po["n"], po["m"], po["conv"], po["k_win"], po["v_win"],
            so["C"], so["n"], so["m"], so["conv"], so["k_win"], so["v_win"])
```

```python
import functools

import jax
import jax.numpy as jnp
from jax import lax
from jax.experimental import pallas as pl
from jax.experimental.pallas import tpu as pltpu

BF = jnp.bfloat16
F32 = jnp.float32

D_MODEL = 4096
M_HEADS = 8
M_QK = 256
M_V = 512
M_CHUNK = 64
GATE_CAP = 15.0
HEAD_DIM = 64
A_HEADS = 64
KV_HEADS = 8
GROUP = 8
WINDOW = 128
ROPE_THETA = 10000.0
D_FF = 11008
PAST_LEN = 16384
EPS = 1e-6

VMEM_LIMIT_V7X = 58 * 1024 * 1024
LANES = 128


def _params(sem):
    return pltpu.CompilerParams(dimension_semantics=sem, vmem_limit_bytes=VMEM_LIMIT_V7X)


def _sigmoid(x):
    return 1.0 / (1.0 + jnp.exp(-x))


def _lag_row(j, i):
    return i * jnp.minimum(j, 1)


def _lag_col(j):
    return jnp.maximum(j - 1, 0)


def _lagged(index_map):
    return lambda j, i: index_map(_lag_col(j), _lag_row(j, i))


def _chunk(j, i, nj, ni):
    last = j // nj
    return jnp.minimum(j, nj - 1), i + last * (ni - 1 - i)


def _stage_weight(i, w_ref, wb_ref, col0=0, wq_ref=None):
    ck, cw = w_ref.shape
    wq = w_ref[...].astype(BF)
    wb_ref[pl.ds(pl.multiple_of(i * ck, ck), ck), col0:col0 + cw] = wq
    if wq_ref is not None:
        wq_ref[...] = wq


def _by_slot(j, stage, step, before=None):
    @pl.when(j == 0)
    def _():
        stage(0)

    for parity in (0, 1):
        @pl.when((j > 0) & (j % 2 == parity))
        def _():
            if before is not None:
                before()
            stage(parity)
            step(1 - parity)


def _linear_kernel(x_ref, w_ref, *rest, n_extra, epilogue, n_sub, transposed, rows_interleaved, keep_bf16):
    extra = rest[:n_extra]
    o_ref = rest[n_extra]
    wq_ref = rest[n_extra + 1] if keep_bf16 else None
    wb_refs = rest[-2:]
    j = pl.program_id(0)
    i = pl.program_id(1)
    tm = x_ref.shape[0]
    sub = tm // n_sub

    def stage(slot):
        _stage_weight(i, w_ref, wb_refs[slot], wq_ref=wq_ref)

    def step(slot):
        w = wb_refs[slot][...]
        for r in range(n_sub):
            rows = slice(r * sub, (r + 1) * sub)
            xs = x_ref[rows, :].astype(BF)
            if transposed:
                acc = lax.dot_general(xs, w, (((1,), (1,)), ((), ())), preferred_element_type=F32)
            else:
                acc = jnp.dot(xs, w, preferred_element_type=F32)
            if rows_interleaved:
                assert sub == PERM_GROUP
                acc = pltpu.einshape("qsd->sqd", acc.reshape(PERM_Q, SUBLANES, acc.shape[1])).reshape(acc.shape)
            ex = [e[rows, :] if e.shape[0] == tm else e[...] for e in extra]
            o_ref[rows, :] = epilogue(acc, *ex).astype(o_ref.dtype)

    _by_slot(j, stage, step)


def linear(x, w3, lead, *, name, n_cols, tm, tn, epilogue, extras=(), out_dtype=F32,
           col_off=0, n_sub=1, transposed=False, rows_interleaved=False, keep_bf16=False):
    M, K = x.shape
    assert M % tm == 0 and n_cols % tn == 0 and tm % n_sub == 0
    nj, ni = n_cols // tn, M // tm
    chunk = lambda j, i: _chunk(j, i, nj, ni)
    if transposed:
        cn = tn // ni
        assert cn * ni == tn and cn % 16 == 0
        w_spec = pl.BlockSpec((None, cn, K),
                              lambda j, i: (lead, (chunk(j, i)[0] + col_off) * ni + chunk(j, i)[1], 0))
        wq_spec = pl.BlockSpec((None, cn, K), lambda j, i: (0, chunk(j, i)[0] * ni + chunk(j, i)[1], 0))
        wq_shape = (1, n_cols, K)
        wb_shape = (tn, K)
    else:
        ck = K // ni
        assert ck * ni == K and ck % 16 == 0
        w_spec = pl.BlockSpec((None, ck, tn), lambda j, i: (lead, chunk(j, i)[1], chunk(j, i)[0] + col_off))
        wq_spec = pl.BlockSpec((None, ck, tn), lambda j, i: (0, chunk(j, i)[1], chunk(j, i)[0]))
        wq_shape = (1, K, n_cols)
        wb_shape = (K, tn)
    in_specs = [pl.BlockSpec((tm, K), lambda j, i: (_lag_row(j, i), 0)), w_spec]
    in_specs += [pl.BlockSpec(s.block_shape, _lagged(s.index_map)) for _, s in extras]
    out_specs = [pl.BlockSpec((tm, tn), lambda j, i: (_lag_row(j, i), _lag_col(j)))]
    out_shape = [jax.ShapeDtypeStruct((M, n_cols), out_dtype)]
    if keep_bf16:
        out_specs.append(wq_spec)
        out_shape.append(jax.ShapeDtypeStruct(wq_shape, BF))
    res = pl.pallas_call(
        functools.partial(_linear_kernel, n_extra=len(extras), epilogue=epilogue, n_sub=n_sub,
                          transposed=transposed, rows_interleaved=rows_interleaved, keep_bf16=keep_bf16),
        grid=(nj + 1, ni),
        in_specs=in_specs,
        out_specs=out_specs,
        out_shape=out_shape,
        scratch_shapes=[pltpu.VMEM(wb_shape, BF)] * 2,
        compiler_params=_params(("arbitrary", "arbitrary")),
        name=name,
    )(x, w3, *[a for a, _ in extras])
    return tuple(res) if keep_bf16 else res[0]


def _row_vec(v, tn):
    return v.reshape(1, -1), pl.BlockSpec((1, tn), lambda j, i: (0, j))


def _seq_vec(a, tn, T, tm):
    if T == 1:
        return a, pl.BlockSpec((tm, tn), lambda j, i: (i, j))
    tps = T // tm
    return a.reshape(a.shape[0], 1, a.shape[1]), pl.BlockSpec((None, 1, tn), lambda j, i: (i // tps, 0, j))


def _tile(a, tn, T, tm, col0=0):
    return a, pl.BlockSpec((tm, tn), lambda j, i: (i, j + col0))


def _ep_plain(acc):
    return acc


def _ep_bias(acc, b):
    return acc + b


def _ep_resid(acc, r, g):
    return r + g * acc


def _ep_resid_bias(acc, b, r, g):
    return r + g * (acc + b)


def _ep_rope(acc, b, cos, sin, *, scale):
    y = acc + b
    n = y.shape[1]
    cos = jnp.concatenate([cos] * (n // LANES), axis=1)
    sin = jnp.concatenate([sin] * (n // LANES), axis=1)
    lane = lax.broadcasted_iota(jnp.int32, y.shape, 1)
    first_half = (lane % HEAD_DIM) < (HEAD_DIM // 2)
    partner = jnp.where(first_half, pltpu.roll(y, n - HEAD_DIM // 2, axis=1), pltpu.roll(y, HEAD_DIM // 2, axis=1))
    return (y * cos + partner * sin) * scale


def _ep_gates(acc, b):
    z = GATE_CAP * jnp.tanh((acc + b) / GATE_CAP)
    logsig = jnp.minimum(z, 0.0) - jnp.log(1.0 + jnp.exp(-jnp.abs(z)))
    lane = lax.broadcasted_iota(jnp.int32, z.shape, 1)
    return jnp.where(lane < M_HEADS, z, logsig)


def _silu_kernel(c_ref, o_ref):
    c = c_ref[...]
    o_ref[...] = (c * _sigmoid(c)).astype(o_ref.dtype)


def silu_cast(c):
    return pl.pallas_call(
        _silu_kernel,
        out_shape=jax.ShapeDtypeStruct(c.shape, BF),
        name="silu_c",
    )(c)


ROW_CHUNK = 16


def _modulate_kernel(x_ref, g_ref, sh_ref, sc_ref, o_ref):
    tm = x_ref.shape[0]
    per_row = sh_ref.shape[0] == tm
    g = g_ref[...]
    if not per_row:
        scale_all = 1.0 + sc_ref[...]
        shift_all = sh_ref[...]

    def body(c, carry):
        r = pl.ds(pl.multiple_of(c * ROW_CHUNK, ROW_CHUNK), ROW_CHUNK)
        x = x_ref[r, :]
        y = x * lax.rsqrt(jnp.mean(x * x, axis=-1, keepdims=True) + EPS) * g
        scale = 1.0 + sc_ref[r, :] if per_row else scale_all
        shift = sh_ref[r, :] if per_row else shift_all
        o_ref[r, :] = (y * scale + shift).astype(o_ref.dtype)
        return carry

    n_chunks = tm // ROW_CHUNK
    lax.fori_loop(0, n_chunks, body, 0, unroll=min(4, n_chunks))


def _modulate_interleave_kernel(x_ref, g_ref, sh_ref, sc_ref, o_ref):
    D = x_ref.shape[1]
    x = x_ref[...]
    rstd = lax.rsqrt(jnp.mean(x * x, axis=-1, keepdims=True) + EPS)
    for c in range(D // LANES):
        cols = slice(c * LANES, (c + 1) * LANES)
        y = x_ref[:, cols] * rstd * g_ref[:, cols]
        y = y * (1.0 + sc_ref[:, cols]) + sh_ref[:, cols]
        y = pltpu.einshape("sqd->qsd", y.reshape(SUBLANES, PERM_Q, LANES)).reshape(PERM_GROUP, LANES)
        o_ref[:, cols] = y.astype(o_ref.dtype)


def modulate(x, g, sh, sc, *, T, tm, interleave=False):
    M, D = x.shape
    assert not interleave or (tm == PERM_GROUP and T > 1)
    sh_a, sh_s = _seq_vec(sh, D, T, tm)
    sc_a, sc_s = _seq_vec(sc, D, T, tm)
    fix = lambda s: pl.BlockSpec(s.block_shape, functools.partial(lambda im, i: im(0, i), s.index_map))
    return pl.pallas_call(
        _modulate_interleave_kernel if interleave else _modulate_kernel,
        grid=(M // tm,),
        in_specs=[pl.BlockSpec((tm, D), lambda i: (i, 0)),
                  pl.BlockSpec((1, D), lambda i: (0, 0)),
                  fix(sh_s), fix(sc_s)],
        out_specs=pl.BlockSpec((tm, D), lambda i: (i, 0)),
        out_shape=jax.ShapeDtypeStruct((M, D), BF),
        compiler_params=_params(("arbitrary",)),
        name="modulate",
    )(x, g.reshape(1, D), sh_a, sc_a)


def _rmsnorm_kernel(x_ref, g_ref, o_ref):
    x = x_ref[...]
    o_ref[...] = x * lax.rsqrt(jnp.mean(x * x, axis=-1, keepdims=True) + EPS) * g_ref[...]


def rmsnorm(x, g, *, tm):
    M, D = x.shape
    return pl.pallas_call(
        _rmsnorm_kernel,
        grid=(M // tm,),
        in_specs=[pl.BlockSpec((tm, D), lambda i: (i, 0)), pl.BlockSpec((1, D), lambda i: (0, 0))],
        out_specs=pl.BlockSpec((tm, D), lambda i: (i, 0)),
        out_shape=jax.ShapeDtypeStruct((M, D), F32),
        compiler_params=_params(("arbitrary",)),
        name="final_norm",
    )(x, g.reshape(1, D))


def _conv_gate(ug, uu, pg1, pg2, pu1, pu2, wcg, wcu, bcg, bcu):
    yg = bcg + pg2 * wcg[0:1] + pg1 * wcg[1:2] + ug * wcg[2:3]
    yu = bcu + pu2 * wcu[0:1] + pu1 * wcu[1:2] + uu * wcu[2:3]
    return yg * _sigmoid(yg) * yu


SUBLANES = 8
PERM_GROUP = 512
PERM_Q = PERM_GROUP // SUBLANES


def _ffn_in_kernel(x_ref, wg_ref, wu_ref, wcg_ref, wcu_ref, bcg_ref, bcu_ref,
                   act_ref, cg_ref, cu_ref, wqg_ref, wqu_ref, wb0_ref, wb1_ref, car_ref, *, tps, n_sub):
    j = pl.program_id(0)
    i = pl.program_id(1)
    tm = x_ref.shape[0]
    tn = wg_ref.shape[1]
    G = PERM_GROUP
    assert tm == n_sub * G
    wb_refs = (wb0_ref, wb1_ref)

    def stage(slot):
        _stage_weight(i, wg_ref, wb_refs[slot], 0, wqg_ref)
        _stage_weight(i, wu_ref, wb_refs[slot], tn, wqu_ref)

    def step(slot):
        w = wb_refs[slot][...]
        wcg, wcu, bcg, bcu = wcg_ref[...], wcu_ref[...], bcg_ref[...], bcu_ref[...]
        car = car_ref[...]
        first = lax.broadcasted_iota(jnp.int32, (SUBLANES, 2 * tn), 0) == 0
        for r in range(n_sub):
            rows = slice(r * G, (r + 1) * G)
            u = jnp.dot(x_ref[rows, :], w, preferred_element_type=F32)
            fix1 = jnp.where(first, car[1:2], pltpu.roll(u[G - 8:G], 1, axis=0))
            fix2 = jnp.where(first, car[0:1], pltpu.roll(u[G - 16:G - 8], 1, axis=0))
            p1 = jnp.concatenate([fix1, u[:G - 8]], axis=0)
            p2 = jnp.concatenate([fix2, fix1, u[:G - 16]], axis=0)
            act_ref[rows, :] = _conv_gate(u[:, :tn], u[:, tn:], p1[:, :tn], p2[:, :tn], p1[:, tn:], p2[:, tn:],
                                          wcg, wcu, bcg, bcu).astype(act_ref.dtype)
            car = jnp.concatenate([u[G - 9:G - 8], u[G - 1:G]], axis=0)
        car_ref[...] = car
        cg_ref[...] = car[:, :tn]
        cu_ref[...] = car[:, tn:]

    def reset_history():
        @pl.when(i % tps == 0)
        def _():
            car_ref[...] = jnp.zeros_like(car_ref)

    _by_slot(j, stage, step, before=reset_history)


def ffn_in_prompt(x, w3, lead, w_conv, b_conv, *, B, T, tm, tn, n_sub):
    M, K = x.shape
    F = D_FF
    nj = F // tn
    ni = M // tm
    ck = K // ni
    assert ck * ni == K and ck % 16 == 0
    tps = T // tm
    wc = w_conv
    bc = b_conv.reshape(1, 2 * F)
    chunk = lambda j, i: _chunk(j, i, nj, ni)
    wq_spec = pl.BlockSpec((None, ck, tn), lambda j, i: (0, chunk(j, i)[1], chunk(j, i)[0]))
    act, cg, cu, wq_g, wq_u = pl.pallas_call(
        functools.partial(_ffn_in_kernel, tps=tps, n_sub=n_sub),
        grid=(nj + 1, ni),
        in_specs=[
            pl.BlockSpec((tm, K), lambda j, i: (_lag_row(j, i), 0)),
            pl.BlockSpec((None, ck, tn), lambda j, i: (lead, chunk(j, i)[1], chunk(j, i)[0])),
            pl.BlockSpec((None, ck, tn), lambda j, i: (lead, chunk(j, i)[1], chunk(j, i)[0] + nj)),
            pl.BlockSpec((3, tn), lambda j, i: (0, _lag_col(j))),
            pl.BlockSpec((3, tn), lambda j, i: (0, _lag_col(j) + nj)),
            pl.BlockSpec((1, tn), lambda j, i: (0, _lag_col(j))),
            pl.BlockSpec((1, tn), lambda j, i: (0, _lag_col(j) + nj)),
        ],
        out_specs=[
            pl.BlockSpec((tm, tn), lambda j, i: (_lag_row(j, i), _lag_col(j))),
            pl.BlockSpec((None, 2, tn), lambda j, i: (_lag_row(j, i) // tps, 0, _lag_col(j))),
            pl.BlockSpec((None, 2, tn), lambda j, i: (_lag_row(j, i) // tps, 0, _lag_col(j))),
            wq_spec, wq_spec,
        ],
        out_shape=[
            jax.ShapeDtypeStruct((M, F), BF),
            jax.ShapeDtypeStruct((B, 2, F), F32),
            jax.ShapeDtypeStruct((B, 2, F), F32),
            jax.ShapeDtypeStruct((1, K, F), BF),
            jax.ShapeDtypeStruct((1, K, F), BF),
        ],
        scratch_shapes=[pltpu.VMEM((K, 2 * tn), BF), pltpu.VMEM((K, 2 * tn), BF),
                        pltpu.VMEM((2, 2 * tn), F32)],
        compiler_params=_params(("arbitrary", "arbitrary")),
        name="ffn_in_p",
    )(x, w3, w3, wc, wc, bc, bc)
    return act, jnp.concatenate([cg, cu], axis=-1), (wq_g, wq_u)


def _ffn_in_step_kernel(x_ref, wg_ref, wu_ref, wcg_ref, wcu_ref, bcg_ref, bcu_ref,
                        c0g_ref, c0u_ref, c1g_ref, c1u_ref, act_ref, ug_ref, uu_ref):
    x = x_ref[...]
    ug = jnp.dot(x, wg_ref[...].astype(BF), preferred_element_type=F32)
    uu = jnp.dot(x, wu_ref[...].astype(BF), preferred_element_type=F32)
    act_ref[...] = _conv_gate(ug, uu, c1g_ref[...], c0g_ref[...], c1u_ref[...], c0u_ref[...],
                              wcg_ref[...], wcu_ref[...], bcg_ref[...], bcu_ref[...]).astype(act_ref.dtype)
    ug_ref[...] = ug
    uu_ref[...] = uu


def ffn_in_step(x, wg3, wu3, w_conv, b_conv, cache, *, tn):
    Bs, K = x.shape
    F = D_FF
    nj = F // tn
    bc = b_conv.reshape(1, 2 * F)
    cflat = cache.reshape(Bs, 4 * F)
    vec = lambda off: pl.BlockSpec((Bs, tn), lambda j: (0, j + off * nj))
    act, ug, uu = pl.pallas_call(
        _ffn_in_step_kernel,
        grid=(nj,),
        in_specs=[
            pl.BlockSpec((Bs, K), lambda j: (0, 0)),
            pl.BlockSpec((None, K, tn), lambda j: (0, 0, j)),
            pl.BlockSpec((None, K, tn), lambda j: (0, 0, j)),
            pl.BlockSpec((3, tn), lambda j: (0, j)),
            pl.BlockSpec((3, tn), lambda j: (0, j + nj)),
            pl.BlockSpec((1, tn), lambda j: (0, j)),
            pl.BlockSpec((1, tn), lambda j: (0, j + nj)),
            vec(0), vec(1), vec(2), vec(3),
        ],
        out_specs=[pl.BlockSpec((Bs, tn), lambda j: (0, j))] * 3,
        out_shape=[jax.ShapeDtypeStruct((Bs, F), BF),
                   jax.ShapeDtypeStruct((Bs, F), F32),
                   jax.ShapeDtypeStruct((Bs, F), F32)],
        compiler_params=_params(("arbitrary",)),
        name="ffn_in_s",
    )(x, wg3, wu3, w_conv, w_conv, bc, bc, cflat, cflat, cflat, cflat)
    new_cache = jnp.stack([cache[:, 1, :], jnp.concatenate([ug, uu], axis=-1)], axis=1)
    return act, new_cache


def _mlstm_chunk_kernel(q_ref, k_ref, v_ref, o_ref, g_ref, gt_ref, gh_ref,
                        hs_ref, C_ref, n_ref, m_ref, *, hb):
    hblk = pl.program_id(1)
    c = pl.program_id(2)
    L = q_ref.shape[0]

    @pl.when(c == 0)
    def _():
        C_ref[...] = jnp.zeros_like(C_ref)
        n_ref[...] = jnp.zeros_like(n_ref)
        m_ref[...] = jnp.zeros_like(m_ref)

    g = g_ref[...]
    gt = gt_ref[...]
    lane = lax.broadcasted_iota(jnp.int32, g.shape, 1)
    sub = lax.broadcasted_iota(jnp.int32, gt.shape, 0)
    t_idx = lax.broadcasted_iota(jnp.int32, (L, L), 0)
    s_idx = lax.broadcasted_iota(jnp.int32, (L, L), 1)
    causal = s_idx <= t_idx
    q_scale = jnp.asarray(M_QK ** -0.5, BF)

    for hh_i in range(hb):
        h = hblk * hb + hh_i
        ic_col = jnp.sum(jnp.where(lane == h, g, 0.0), axis=1, keepdims=True)
        fc_col = jnp.sum(jnp.where(lane == h + M_HEADS, g, 0.0), axis=1, keepdims=True)
        ic_row = jnp.sum(jnp.where(sub == h, gt, 0.0), axis=0, keepdims=True)
        fc_row = jnp.sum(jnp.where(sub == h + M_HEADS, gt, 0.0), axis=0, keepdims=True)
        b_col = jnp.sum(jnp.where(causal, fc_row, 0.0), axis=1, keepdims=True)
        b_row = jnp.sum(jnp.where(t_idx <= s_idx, fc_col, 0.0), axis=0, keepdims=True)

        m_old = m_ref[hh_i][0:1, 0:1]
        d = jnp.where(causal, b_col - b_row + ic_row, -jnp.inf)
        inter = b_col + m_old
        mt = jnp.maximum(inter, jnp.max(d, axis=1, keepdims=True))
        w = jnp.exp(d - mt)
        si = jnp.exp(inter - mt)

        qb = q_ref[:, hh_i * M_QK:(hh_i + 1) * M_QK] * q_scale
        kb = k_ref[:, hh_i * M_QK:(hh_i + 1) * M_QK]
        vb = v_ref[:, hh_i * M_V:(hh_i + 1) * M_V]
        C_old = C_ref[hh_i]
        n_old = n_ref[hh_i]
        qk = lax.dot_general(qb, kb, (((1,), (1,)), ((), ())), preferred_element_type=F32) * w
        num = (jnp.dot(qk.astype(BF), vb, preferred_element_type=F32)
               + si * jnp.dot(qb, C_old.astype(BF), preferred_element_type=F32))
        den = jnp.sum(qk, axis=1, keepdims=True) + si * jnp.sum(qb.astype(F32) * n_old, axis=1, keepdims=True)
        hh = num / jnp.maximum(jnp.abs(den), jnp.exp(-mt))

        b_last = b_col[L - 1:L, :]
        gl = b_last - b_col + ic_col
        m_new = jnp.maximum(b_last + m_old, jnp.max(gl, axis=0, keepdims=True))
        wl = jnp.exp(gl - m_new)
        sd = jnp.exp(b_last + m_old - m_new)
        kw = kb.astype(F32) * wl
        C_ref[hh_i] = sd * C_old + jnp.dot(kw.T.astype(BF), vb, preferred_element_type=F32)
        n_ref[hh_i] = sd * n_old + jnp.sum(kw, axis=0, keepdims=True)
        m_ref[hh_i] = jnp.broadcast_to(m_new, (1, LANES))

        hn = hh * lax.rsqrt(jnp.mean(hh * hh, axis=1, keepdims=True) + EPS)
        o = o_ref[:, hh_i * M_V:(hh_i + 1) * M_V].astype(F32)
        hs_ref[:, hh_i * M_V:(hh_i + 1) * M_V] = (
            hn * gh_ref[:, hh_i * M_V:(hh_i + 1) * M_V] * _sigmoid(o)).astype(hs_ref.dtype)


def mlstm_prompt(z, gates, g_head, *, B, T, L, hb):
    M = B * T
    nc = T // L
    H = M_HEADS
    gt = jnp.swapaxes(gates[:, :2 * H].reshape(B * nc, L, 2 * H), 1, 2)
    row = lambda b, h, c: b * nc + c
    nhb = H // hb
    kq = nhb
    kv = nhb
    ko = 2 * nhb
    hs, C, n, m = pl.pallas_call(
        functools.partial(_mlstm_chunk_kernel, hb=hb),
        grid=(B, nhb, nc),
        in_specs=[
            pl.BlockSpec((L, hb * M_QK), lambda b, h, c: (row(b, h, c), h)),
            pl.BlockSpec((L, hb * M_QK), lambda b, h, c: (row(b, h, c), kq + h)),
            pl.BlockSpec((L, hb * M_V), lambda b, h, c: (row(b, h, c), kv + h)),
            pl.BlockSpec((L, hb * M_V), lambda b, h, c: (row(b, h, c), ko + h)),
            pl.BlockSpec((L, LANES), lambda b, h, c: (row(b, h, c), 0)),
            pl.BlockSpec((None, 2 * H, L), lambda b, h, c: (row(b, h, c), 0, 0)),
            pl.BlockSpec((1, hb * M_V), lambda b, h, c: (0, h)),
        ],
        out_specs=[
            pl.BlockSpec((L, hb * M_V), lambda b, h, c: (row(b, h, c), h)),
            pl.BlockSpec((None, hb, M_QK, M_V), lambda b, h, c: (b, h, 0, 0)),
            pl.BlockSpec((None, hb, 1, M_QK), lambda b, h, c: (b, h, 0, 0)),
            pl.BlockSpec((None, hb, 1, LANES), lambda b, h, c: (b, h, 0, 0)),
        ],
        out_shape=[
            jax.ShapeDtypeStruct((M, D_MODEL), BF),
            jax.ShapeDtypeStruct((B, M_HEADS, M_QK, M_V), F32),
            jax.ShapeDtypeStruct((B, M_HEADS, 1, M_QK), F32),
            jax.ShapeDtypeStruct((B, M_HEADS, 1, LANES), F32),
        ],
        compiler_params=_params(("arbitrary", "arbitrary", "arbitrary")),
        name="mlstm_p",
    )(z, z, z, z, gates, gt, g_head.reshape(1, D_MODEL))
    return hs, C, n[:, :, 0, :], m[:, :, 0, 0]


def _mlstm_step_kernel(q_ref, k_ref, v_ref, o_ref, ig_ref, lf_ref, m0_ref, n0_ref, C0_ref, gh_ref,
                       hs_ref, C_ref, n_ref, m_ref):
    ic = ig_ref[...]
    fc = lf_ref[...]
    m_old = m0_ref[...]
    inter = fc + m_old
    mt = jnp.maximum(inter, ic)
    w = jnp.exp(ic - mt)
    si = jnp.exp(inter - mt)
    q = (q_ref[...] * (M_QK ** -0.5)).astype(BF).astype(F32)
    k = k_ref[...]
    v = v_ref[...]
    n_old = n0_ref[...]
    qk = jnp.sum(q * k.astype(BF).astype(F32), axis=1, keepdims=True) * w
    wl = jnp.exp(ic - mt)
    sd = jnp.exp(inter - mt)
    kw = k * wl
    qT = q.T
    kwT = kw.T
    rows = []
    for h in range(M_HEADS):
        C_old = C0_ref[h]
        rows.append(jnp.sum(qT[:, h:h + 1] * C_old, axis=0, keepdims=True))
        C_ref[h] = sd[h:h + 1, :] * C_old + kwT[:, h:h + 1] * v[h:h + 1, :]
    qC = jnp.concatenate(rows, axis=0)
    num = qk.astype(BF).astype(F32) * v.astype(BF).astype(F32) + si * qC
    den = qk + si * jnp.sum(q * n_old, axis=1, keepdims=True)
    hh = num / jnp.maximum(jnp.abs(den), jnp.exp(-mt))
    n_ref[...] = sd * n_old + kw
    m_ref[...] = mt
    hn = hh * lax.rsqrt(jnp.mean(hh * hh, axis=1, keepdims=True) + EPS)
    hs_ref[...] = (hn * gh_ref[...] * _sigmoid(o_ref[...])).astype(hs_ref.dtype)


def mlstm_step(z, gates, g_head, C0, n0, m0):
    Bs = z.shape[0]
    H = M_HEADS
    qd, vd = H * M_QK, H * M_V
    q = z[:, :qd].reshape(Bs, H, M_QK)
    k = z[:, qd:2 * qd].reshape(Bs, H, M_QK)
    v = z[:, 2 * qd:2 * qd + vd].reshape(Bs, H, M_V)
    o = z[:, 2 * qd + vd:].reshape(Bs, H, M_V)
    ig = gates[:, :H].reshape(Bs, H, 1)
    lf = gates[:, H:2 * H].reshape(Bs, H, 1)
    per_b = lambda *tail: pl.BlockSpec((None,) + tail, lambda b: (b,) + (0,) * len(tail))
    hs, C, n, m = pl.pallas_call(
        _mlstm_step_kernel,
        grid=(Bs,),
        in_specs=[
            per_b(H, M_QK), per_b(H, M_QK), per_b(H, M_V), per_b(H, M_V),
            per_b(H, 1), per_b(H, 1), per_b(H, 1),
            pl.BlockSpec((None, None, H, M_QK), lambda b: (0, b, 0, 0)),
            pl.BlockSpec((None, None, H, M_QK, M_V), lambda b: (0, b, 0, 0, 0)),
            pl.BlockSpec((H, M_V), lambda b: (0, 0)),
        ],
        out_specs=[
            per_b(H, M_V),
            pl.BlockSpec((None, None, H, M_QK, M_V), lambda b: (0, b, 0, 0, 0)),
            pl.BlockSpec((None, None, H, M_QK), lambda b: (0, b, 0, 0)),
            per_b(H, 1),
        ],
        out_shape=[
            jax.ShapeDtypeStruct((Bs, H, M_V), BF),
            jax.ShapeDtypeStruct(C0.shape, F32),
            jax.ShapeDtypeStruct(n0.shape, F32),
            jax.ShapeDtypeStruct((Bs, H, 1), F32),
        ],
        compiler_params=_params(("arbitrary",)),
        name="mlstm_s",
    )(q, k, v, o, ig, lf, m0.reshape(Bs, H, 1), n0, C0, g_head.reshape(H, M_V))
    return hs.reshape(Bs, D_MODEL), C, n, m.reshape(1, Bs, H)


def _attn_kernel(sink_ref, q_ref, kc_ref, kp_ref, vc_ref, vp_ref, o_ref):
    n = pl.program_id(1)
    W = WINDOW
    R = GROUP * W
    kp = kp_ref[...].astype(BF)
    kc = kc_ref[...].astype(BF)
    vpT = vp_ref[...].T.astype(BF)
    vcT = vc_ref[...].T.astype(BF)
    j_idx = lax.broadcasted_iota(jnp.int32, (W, R), 0)
    i_idx = lax.broadcasted_iota(jnp.int32, (W, R), 1) % W
    from_prev = j_idx > i_idx
    lane_g = lax.broadcasted_iota(jnp.int32, (1, R), 1) // W
    no_prev = jnp.where(n > 0, 0.0, -jnp.inf)
    nt = (((1,), (1,)), ((), ()))
    for h in range(KV_HEADS):
        hs = slice(h * HEAD_DIM, (h + 1) * HEAD_DIM)
        qh = jnp.concatenate(
            [q_ref[:, (h * GROUP + g) * HEAD_DIM:(h * GROUP + g + 1) * HEAD_DIM] for g in range(GROUP)], axis=0)
        sp = lax.dot_general(kp[:, hs], qh, nt, preferred_element_type=F32)
        sc = lax.dot_general(kc[:, hs], qh, nt, preferred_element_type=F32)
        s = jnp.where(from_prev, sp + no_prev, sc)
        sk = jnp.zeros((1, R), F32)
        for g in range(GROUP):
            sk = jnp.where(lane_g == g, sink_ref[h * GROUP + g], sk)
        mx = jnp.maximum(jnp.max(s, axis=0, keepdims=True), sk)
        p = jnp.exp(s - mx)
        p = p / (jnp.sum(p, axis=0, keepdims=True) + jnp.exp(sk - mx))
        oT = (jnp.dot(vpT[hs, :], jnp.where(from_prev, p, 0.0).astype(BF), preferred_element_type=F32)
              + jnp.dot(vcT[hs, :], jnp.where(from_prev, 0.0, p).astype(BF), preferred_element_type=F32))
        for g in range(GROUP):
            c0 = (h * GROUP + g) * HEAD_DIM
            o_ref[:, c0:c0 + HEAD_DIM] = oT[:, g * W:(g + 1) * W].T.astype(o_ref.dtype)


def attention_prompt(q, k, v, sinks, *, B, T):
    M = B * T
    W = WINDOW
    nb = T // W
    kvd = KV_HEADS * HEAD_DIM
    cur = lambda b, n: (b * nb + n, 0)
    prev = lambda b, n: (b * nb + jnp.maximum(n - 1, 0), 0)
    return pl.pallas_call(
        _attn_kernel,
        grid=(B, nb),
        in_specs=[
            pl.BlockSpec(memory_space=pltpu.SMEM),
            pl.BlockSpec((W, D_MODEL), cur),
            pl.BlockSpec((W, kvd), cur), pl.BlockSpec((W, kvd), prev),
            pl.BlockSpec((W, kvd), cur), pl.BlockSpec((W, kvd), prev),
        ],
        out_specs=pl.BlockSpec((W, D_MODEL), cur),
        out_shape=jax.ShapeDtypeStruct((M, D_MODEL), BF),
        compiler_params=_params(("arbitrary", "arbitrary")),
        name="attn_p",
    )(sinks, q, k, k, v, v)


def _per_head_rows(row):
    return jnp.concatenate(
        [jnp.broadcast_to(row[:, h * HEAD_DIM:(h + 1) * HEAD_DIM], (GROUP, HEAD_DIM)) for h in range(KV_HEADS)],
        axis=0)


def _attn_step_kernel(q_ref, kc_ref, vc_ref, kn_ref, vn_ref, sk_ref, o_ref, ko_ref, vo_ref, *, bb):
    W = WINDOW
    jl = lax.broadcasted_iota(jnp.int32, (A_HEADS, W), 1)
    sk = sk_ref[...]
    heads = lambda a, h: a[h * GROUP:(h + 1) * GROUP, :]
    scores = []
    for b in range(bb):
        q = q_ref[b]
        kcb = kc_ref[b].astype(BF)
        s = jnp.concatenate(
            [lax.dot_general(heads(q, h).astype(BF), kcb[:, h * HEAD_DIM:(h + 1) * HEAD_DIM],
                             (((1,), (1,)), ((), ())), preferred_element_type=F32) for h in range(KV_HEADS)],
            axis=0)
        kne = _per_head_rows(kn_ref[b]).astype(BF).astype(F32)
        sn = jnp.sum(q.astype(BF).astype(F32) * kne, axis=1, keepdims=True)
        scores.append((s, sn))
    probs = []
    for s, sn in scores:
        s = jnp.where(jl >= 1, s, -jnp.inf)
        mx = jnp.maximum(jnp.maximum(jnp.max(s, axis=1, keepdims=True), sn), sk)
        p = jnp.exp(s - mx)
        pn = jnp.exp(sn - mx)
        den = jnp.sum(p, axis=1, keepdims=True) + pn + jnp.exp(sk - mx)
        probs.append((p / den, pn / den))
    for b in range(bb):
        p, pn = probs[b]
        vcb = vc_ref[b].astype(BF)
        o = jnp.concatenate(
            [jnp.dot(heads(p, h).astype(BF), vcb[:, h * HEAD_DIM:(h + 1) * HEAD_DIM],
                     preferred_element_type=F32) for h in range(KV_HEADS)], axis=0)
        vne = _per_head_rows(vn_ref[b]).astype(BF).astype(F32)
        o_ref[b] = o + pn.astype(BF).astype(F32) * vne
        ko_ref[b, 0:W - 1, :] = kc_ref[b, 1:W, :]
        ko_ref[b, W - 1:W, :] = kn_ref[b]
        vo_ref[b, 0:W - 1, :] = vc_ref[b, 1:W, :]
        vo_ref[b, W - 1:W, :] = vn_ref[b]


def attention_step(q, k_new, v_new, kbuf, vbuf, sinks):
    Bs = q.shape[0]
    W = WINDOW
    kvd = KV_HEADS * HEAD_DIM
    bb = 4
    assert Bs % bb == 0
    per_b = lambda *tail: pl.BlockSpec((bb,) + tail, lambda b: (b,) + (0,) * len(tail))
    o, ko, vo = pl.pallas_call(
        functools.partial(_attn_step_kernel, bb=bb),
        grid=(Bs // bb,),
        in_specs=[per_b(A_HEADS, HEAD_DIM), per_b(W, kvd), per_b(W, kvd), per_b(1, kvd), per_b(1, kvd),
                  pl.BlockSpec((A_HEADS, 1), lambda b: (0, 0))],
        out_specs=[per_b(A_HEADS, HEAD_DIM), per_b(W, kvd), per_b(W, kvd)],
        out_shape=[jax.ShapeDtypeStruct((Bs, A_HEADS, HEAD_DIM), F32),
                   jax.ShapeDtypeStruct((Bs, W, kvd), F32),
                   jax.ShapeDtypeStruct((Bs, W, kvd), F32)],
        compiler_params=_params(("arbitrary",)),
        name="attn_s",
    )(q.reshape(Bs, A_HEADS, HEAD_DIM), kbuf.reshape(Bs, W, kvd), vbuf.reshape(Bs, W, kvd),
      k_new.reshape(Bs, 1, kvd), v_new.reshape(Bs, 1, kvd), sinks.reshape(A_HEADS, 1))
    return (o.reshape(Bs, D_MODEL), ko.reshape(Bs, W, KV_HEADS, HEAD_DIM), vo.reshape(Bs, W, KV_HEADS, HEAD_DIM))


def _rope_tables(pos, width):
    half = HEAD_DIM // 2
    freq = ROPE_THETA ** (-jnp.arange(half, dtype=F32) / half)
    ang = pos.astype(F32)[:, None] * freq[None, :]
    cos = jnp.cos(ang)
    sin = jnp.sin(ang)
    reps = width // HEAD_DIM
    return (jnp.tile(jnp.concatenate([cos, cos], axis=1), (1, reps)),
            jnp.tile(jnp.concatenate([-sin, sin], axis=1), (1, reps)))


def _tiles(T):
    if T > 1:
        return dict(tm=1024, tn=1024, n_sub=2,
                    tm_r=512, tn_r=1024, n_sub_r=1,
                    tn_kv=512, tm_ffi=2048, tn_ffi=256, n_sub_ffi=4,
                    tm_ffo=512, tn_ffo=512, n_sub_ffo=1, tm_mod=512)
    return dict(tm=None, tn=1024, n_sub=1, tm_r=None, tn_r=1024, n_sub_r=1, tn_kv=512,
                tm_ffi=None, tn_ffi=256, n_sub_ffi=1, tm_ffo=None, tn_ffo=512, n_sub_ffo=1, tm_mod=None)


def _run(x, ada, pos, state, P, *, B, T, wq_in=None):
    D = D_MODEL
    M = B * T
    c = {k: (M if v is None else v) for k, v in _tiles(T).items()}
    tm, tn, tm_r, tn_r, tm_mod = c["tm"], c["tn"], c["tm_r"], c["tn_r"], c["tm_mod"]
    cos, sin = _rope_tables(pos, LANES)
    if T > 1:
        tps = T // tm
        rope_spec = pl.BlockSpec((tm, LANES), lambda j, i: (i % tps, 0))
    else:
        rope_spec = pl.BlockSpec((1, LANES), lambda j, i: (0, 0))

    out = {}
    tag = "_p" if T > 1 else "_s"
    wq = {}

    def lin(key, x_, w3, lead, **kw):
        if wq_in is None:
            res, wq[key] = linear(x_, w3, lead, keep_bf16=True, **kw)
            return res
        kw.pop("col_off", None)
        return linear(x_, wq_in[key], 0, **kw)

    sh1, sc1, ga1, sh2, sc2, ga2 = ada[0]
    hn = modulate(x, P["g_norm1"][0], sh1, sc1, T=T, tm=tm_mod)
    qd, vd = M_HEADS * M_QK, M_HEADS * M_V
    w_inT = jnp.swapaxes(P["w_m_in"], 1, 2)
    z = lin("m_in", hn, w_inT, 0, name="m_in" + tag, n_cols=2 * qd + 2 * vd, tm=tm, tn=tn, n_sub=c["n_sub"],
               epilogue=_ep_plain, transposed=True, out_dtype=BF if T > 1 else F32)
    wgT = jnp.pad(w_inT[:, 2 * qd + 2 * vd:, :], ((0, 0), (0, LANES - 2 * M_HEADS), (0, 0)))
    bg = jnp.pad(jnp.concatenate([P["b_m_i"][0], P["b_m_f"][0]]), (0, LANES - 2 * M_HEADS))
    gates = linear(hn, wgT, 0, name="m_gates" + tag, n_cols=LANES, tm=tm, tn=LANES, epilogue=_ep_gates,
                   extras=[_row_vec(bg, LANES)], transposed=True)
    if T > 1:
        hs, C, n, m = mlstm_prompt(z, gates, P["g_m_head"][0], B=B, T=T, L=256, hb=8)
        out["C"], out["n"], out["m"] = C[None], n[None], m[None]
    else:
        hs, out["C"], out["n"], out["m"] = mlstm_step(z, gates, P["g_m_head"][0], *state["mlstm"])
    x = lin("m_out", hs, P["w_m_out"], 0, name="m_out" + tag, n_cols=D, tm=tm_r, tn=tn_r, n_sub=c["n_sub_r"],
               epilogue=_ep_resid, extras=[_tile(x, tn_r, T, tm_r), _seq_vec(ga1, tn_r, T, tm_r)])

    convs = []
    for l in range(2):
        if l == 1:
            sh1, sc1, ga1, sh2, sc2, ga2 = ada[1]
            shk, sck = ada[2]
            hk = modulate(x, P["g_kv"], shk, sck, T=T, tm=tm_mod)
            kvd = KV_HEADS * HEAD_DIM
            tn_kv = c["tn_kv"]
            k = lin("kv_k", hk, P["w_kv"], 0, name="kv_k" + tag, n_cols=kvd, tm=tm, tn=tn_kv, n_sub=c["n_sub"],
                       epilogue=functools.partial(_ep_rope, scale=1.0),
                       extras=[_row_vec(P["b_kv"][:kvd], tn_kv), (cos, rope_spec), (sin, rope_spec)])
            v = lin("kv_v", hk, P["w_kv"], 0, name="kv_v" + tag, n_cols=kvd, tm=tm, tn=tn_kv, n_sub=c["n_sub"],
                       col_off=kvd // tn_kv, epilogue=_ep_bias, extras=[_row_vec(P["b_kv"][kvd:], tn_kv)])
            hn = modulate(x, P["g_norm1"][1], sh1, sc1, T=T, tm=tm_mod)
            q = lin("attn_q", hn, P["w_q"], 0, name="attn_q" + tag, n_cols=D, tm=tm, tn=tn, n_sub=c["n_sub"],
                       epilogue=functools.partial(_ep_rope, scale=HEAD_DIM ** -0.5),
                       extras=[_row_vec(P["b_q"][0], tn), (cos, rope_spec), (sin, rope_spec)],
                       out_dtype=BF if T > 1 else F32)
            if T > 1:
                o = attention_prompt(q, k, v, P["sinks"][0], B=B, T=T)
                last = lambda a: a.reshape(B, T, kvd)[:, T - WINDOW:, :].reshape(B, WINDOW, KV_HEADS, HEAD_DIM)
                out["k_win"], out["v_win"] = last(k), last(v)
            else:
                o, out["k_win"], out["v_win"] = attention_step(q, k, v, state["kbuf"], state["vbuf"], P["sinks"][0])
            x = lin("attn_o", o, P["w_o"], 0, name="attn_o" + tag, n_cols=D, tm=tm_r, tn=tn_r, n_sub=c["n_sub_r"],
                       epilogue=_ep_resid_bias,
                       extras=[_row_vec(P["b_o"][0], tn_r), _tile(x, tn_r, T, tm_r), _seq_vec(ga1, tn_r, T, tm_r)])

        il = T > 1
        hn = modulate(x, P["g_norm2"][l], sh2, sc2, T=T, tm=PERM_GROUP if il else tm_mod, interleave=il)
        if T > 1:
            act, cb, wq["ffn_in%d" % l] = ffn_in_prompt(hn, P["w_ffn_in"], l, P["w_conv"][l], P["b_conv"][l], B=B, T=T,
                                                        tm=c["tm_ffi"], tn=c["tn_ffi"], n_sub=c["n_sub_ffi"])
        else:
            act, cb = ffn_in_step(hn, *wq_in["ffn_in%d" % l], P["w_conv"][l], P["b_conv"][l], state["conv"][l],
                                  tn=c["tn_ffi"])
        convs.append(cb)
        tm_o, tn_o = c["tm_ffo"], c["tn_ffo"]
        x = lin("ffn_out%d" % l, act, P["w_ffn_out"], l, name="ffn_out" + tag, n_cols=D, tm=tm_o, tn=tn_o, n_sub=c["n_sub_ffo"],
                   epilogue=_ep_resid, extras=[_tile(x, tn_o, T, tm_o), _seq_vec(ga2, tn_o, T, tm_o)],
                   rows_interleaved=il)

    out["conv"] = jnp.stack(convs)
    out["y"] = rmsnorm(x, P["g_final"], tm=tm_mod)
    out["wq"] = wq
    return out


def kernel(x_prompt, x_sample, state_mlstm_C, state_mlstm_n, state_mlstm_m, cache_conv, cache_k_win, cache_v_win, c_prompt, c_sample, w_ada, g_norm1, g_norm2, w_m_in, b_m_i, b_m_f, g_m_head, w_m_out, w_ada_kv, g_kv, w_kv, b_kv, w_q, b_q, sinks, w_o, b_o, w_ffn_in, w_conv, b_conv, w_ffn_out, g_final):
    D = D_MODEL
    Bp, Tp, _ = x_prompt.shape
    Bs, Ts, _ = x_sample.shape
    assert Ts == 1
    P = dict(g_norm1=g_norm1, g_norm2=g_norm2, w_m_in=w_m_in, b_m_i=b_m_i, b_m_f=b_m_f, g_m_head=g_m_head,
             w_m_out=w_m_out, g_kv=g_kv, w_kv=w_kv[None], b_kv=b_kv, w_q=w_q, b_q=b_q, sinks=sinks, w_o=w_o,
             b_o=b_o, w_ffn_in=w_ffn_in, w_conv=w_conv, b_conv=b_conv, w_ffn_out=w_ffn_out, g_final=g_final)

    n_c = Bp + Bs
    pad_c = -n_c % 16
    cs = silu_cast(jnp.concatenate([c_prompt, c_sample, jnp.zeros((pad_c, D), F32)], axis=0))
    rows_c = n_c + pad_c
    ada_all = []
    for l in range(2):
        a = linear(cs, w_ada, l, name="ada", n_cols=6 * D, tm=rows_c, tn=512, epilogue=_ep_plain)
        ada_all.append(a)
    a_kv = linear(cs, w_ada_kv[None], 0, name="ada_kv", n_cols=2 * D, tm=rows_c, tn=512, epilogue=_ep_plain)

    def split(lo, hi):
        per_layer = [tuple(a[lo:hi, i * D:(i + 1) * D] for i in range(6)) for a in ada_all]
        return per_layer + [(a_kv[lo:hi, :D], a_kv[lo:hi, D:])]

    po = _run(x_prompt.reshape(Bp * Tp, D), split(0, Bp), jnp.arange(Tp, dtype=jnp.int32), None, P, B=Bp, T=Tp)
    state = dict(mlstm=(state_mlstm_C, state_mlstm_n, state_mlstm_m), conv=cache_conv,
                 kbuf=cache_k_win, vbuf=cache_v_win)
    so = _run(x_sample.reshape(Bs, D), split(Bp, Bp + Bs), PAST_LEN + jnp.arange(1, dtype=jnp.int32), state, P,
              B=Bs, T=1, wq_in=po["wq"])
    return (po["y"].reshape(Bp, Tp, D), so["y"].reshape(Bs, 1, D),
            po["C"], po["n"], po["m"], po["conv"], po["k_win"], po["v_win"],
            so["C"], so["n"], so["m"], so["conv"], so["k_win"], so["v_win"])
```

```python
import functools

import jax
import jax.numpy as jnp
from jax import lax
from jax.experimental import pallas as pl
from jax.experimental.pallas import tpu as pltpu

BF = jnp.bfloat16
F32 = jnp.float32

D_MODEL = 4096
M_HEADS = 8
M_QK = 256
M_V = 512
M_CHUNK = 64
GATE_CAP = 15.0
HEAD_DIM = 64
A_HEADS = 64
KV_HEADS = 8
GROUP = 8
WINDOW = 128
ROPE_THETA = 10000.0
D_FF = 11008
PAST_LEN = 16384
EPS = 1e-6

VMEM_LIMIT_V7X = 58 * 1024 * 1024
LANES = 128


def _params(sem):
    return pltpu.CompilerParams(dimension_semantics=sem, vmem_limit_bytes=VMEM_LIMIT_V7X)


def _sigmoid(x):
    return 1.0 / (1.0 + jnp.exp(-x))


def _lag_row(j, i):
    return i * jnp.minimum(j, 1)


def _lag_col(j):
    return jnp.maximum(j - 1, 0)


def _lagged(index_map):
    return lambda j, i: index_map(_lag_col(j), _lag_row(j, i))


def _chunk(j, i, nj, ni):
    last = j // nj
    return jnp.minimum(j, nj - 1), i + last * (ni - 1 - i)


def _stage_weight(i, w_ref, wb_ref, col0=0, wq_ref=None):
    ck, cw = w_ref.shape
    wq = w_ref[...].astype(BF)
    wb_ref[pl.ds(pl.multiple_of(i * ck, ck), ck), col0:col0 + cw] = wq
    if wq_ref is not None:
        wq_ref[...] = wq


def _by_slot(j, stage, step, before=None):
    @pl.when(j == 0)
    def _():
        stage(0)

    for parity in (0, 1):
        @pl.when((j > 0) & (j % 2 == parity))
        def _():
            if before is not None:
                before()
            stage(parity)
            step(1 - parity)


def _linear_kernel(x_ref, w_ref, *rest, n_extra, epilogue, n_sub, transposed, rows_interleaved, keep_bf16):
    extra = rest[:n_extra]
    o_ref = rest[n_extra]
    wq_ref = rest[n_extra + 1] if keep_bf16 else None
    wb_refs = rest[-2:]
    j = pl.program_id(0)
    i = pl.program_id(1)
    tm = x_ref.shape[0]
    sub = tm // n_sub

    def stage(slot):
        _stage_weight(i, w_ref, wb_refs[slot], wq_ref=wq_ref)

    def step(slot):
        w = wb_refs[slot][...]
        for r in range(n_sub):
            rows = slice(r * sub, (r + 1) * sub)
            xs = x_ref[rows, :].astype(BF)
            if transposed:
                acc = lax.dot_general(xs, w, (((1,), (1,)), ((), ())), preferred_element_type=F32)
            else:
                acc = jnp.dot(xs, w, preferred_element_type=F32)
            if rows_interleaved:
                assert sub == PERM_GROUP
                acc = pltpu.einshape("qsd->sqd", acc.reshape(PERM_Q, SUBLANES, acc.shape[1])).reshape(acc.shape)
            ex = [e[rows, :] if e.shape[0] == tm else e[...] for e in extra]
            o_ref[rows, :] = epilogue(acc, *ex).astype(o_ref.dtype)

    _by_slot(j, stage, step)


def linear(x, w3, lead, *, name, n_cols, tm, tn, epilogue, extras=(), out_dtype=F32,
           col_off=0, n_sub=1, transposed=False, rows_interleaved=False, keep_bf16=False):
    M, K = x.shape
    assert M % tm == 0 and n_cols % tn == 0 and tm % n_sub == 0
    nj, ni = n_cols // tn, M // tm
    chunk = lambda j, i: _chunk(j, i, nj, ni)
    if transposed:
        cn = tn // ni
        assert cn * ni == tn and cn % 16 == 0
        w_spec = pl.BlockSpec((None, cn, K),
                              lambda j, i: (lead, (chunk(j, i)[0] + col_off) * ni + chunk(j, i)[1], 0))
        wq_spec = pl.BlockSpec((None, cn, K), lambda j, i: (0, chunk(j, i)[0] * ni + chunk(j, i)[1], 0))
        wq_shape = (1, n_cols, K)
        wb_shape = (tn, K)
    else:
        ck = K // ni
        assert ck * ni == K and ck % 16 == 0
        w_spec = pl.BlockSpec((None, ck, tn), lambda j, i: (lead, chunk(j, i)[1], chunk(j, i)[0] + col_off))
        wq_spec = pl.BlockSpec((None, ck, tn), lambda j, i: (0, chunk(j, i)[1], chunk(j, i)[0]))
        wq_shape = (1, K, n_cols)
        wb_shape = (K, tn)
    in_specs = [pl.BlockSpec((tm, K), lambda j, i: (_lag_row(j, i), 0)), w_spec]
    in_specs += [pl.BlockSpec(s.block_shape, _lagged(s.index_map)) for _, s in extras]
    out_specs = [pl.BlockSpec((tm, tn), lambda j, i: (_lag_row(j, i), _lag_col(j)))]
    out_shape = [jax.ShapeDtypeStruct((M, n_cols), out_dtype)]
    if keep_bf16:
        out_specs.append(wq_spec)
        out_shape.append(jax.ShapeDtypeStruct(wq_shape, BF))
    res = pl.pallas_call(
        functools.partial(_linear_kernel, n_extra=len(extras), epilogue=epilogue, n_sub=n_sub,
                          transposed=transposed, rows_interleaved=rows_interleaved, keep_bf16=keep_bf16),
        grid=(nj + 1, ni),
        in_specs=in_specs,
        out_specs=out_specs,
        out_shape=out_shape,
        scratch_shapes=[pltpu.VMEM(wb_shape, BF)] * 2,
        compiler_params=_params(("arbitrary", "arbitrary")),
        name=name,
    )(x, w3, *[a for a, _ in extras])
    return tuple(res) if keep_bf16 else res[0]


def _row_vec(v, tn):
    return v.reshape(1, -1), pl.BlockSpec((1, tn), lambda j, i: (0, j))


def _seq_vec(a, tn, T, tm):
    if T == 1:
        return a, pl.BlockSpec((tm, tn), lambda j, i: (i, j))
    tps = T // tm
    return a.reshape(a.shape[0], 1, a.shape[1]), pl.BlockSpec((None, 1, tn), lambda j, i: (i // tps, 0, j))


def _tile(a, tn, T, tm, col0=0):
    return a, pl.BlockSpec((tm, tn), lambda j, i: (i, j + col0))


def _ep_plain(acc):
    return acc


def _ep_bias(acc, b):
    return acc + b


def _ep_resid(acc, r, g):
    return r + g * acc


def _ep_resid_bias(acc, b, r, g):
    return r + g * (acc + b)


def _ep_rope(acc, b, cos, sin, *, scale):
    y = acc + b
    n = y.shape[1]
    cos = jnp.concatenate([cos] * (n // LANES), axis=1)
    sin = jnp.concatenate([sin] * (n // LANES), axis=1)
    lane = lax.broadcasted_iota(jnp.int32, y.shape, 1)
    first_half = (lane % HEAD_DIM) < (HEAD_DIM // 2)
    partner = jnp.where(first_half, pltpu.roll(y, n - HEAD_DIM // 2, axis=1), pltpu.roll(y, HEAD_DIM // 2, axis=1))
    return (y * cos + partner * sin) * scale


def _ep_gates(acc, b):
    z = GATE_CAP * jnp.tanh((acc + b) / GATE_CAP)
    logsig = jnp.minimum(z, 0.0) - jnp.log(1.0 + jnp.exp(-jnp.abs(z)))
    lane = lax.broadcasted_iota(jnp.int32, z.shape, 1)
    return jnp.where(lane < M_HEADS, z, logsig)


def _silu_kernel(c_ref, o_ref):
    c = c_ref[...]
    o_ref[...] = (c * _sigmoid(c)).astype(o_ref.dtype)


def silu_cast(c):
    return pl.pallas_call(
        _silu_kernel,
        out_shape=jax.ShapeDtypeStruct(c.shape, BF),
        name="silu_c",
    )(c)


ROW_CHUNK = 16


def _modulate_kernel(x_ref, g_ref, sh_ref, sc_ref, o_ref):
    tm = x_ref.shape[0]
    per_row = sh_ref.shape[0] == tm
    g = g_ref[...]
    if not per_row:
        scale_all = 1.0 + sc_ref[...]
        shift_all = sh_ref[...]

    def body(c, carry):
        r = pl.ds(pl.multiple_of(c * ROW_CHUNK, ROW_CHUNK), ROW_CHUNK)
        x = x_ref[r, :]
        y = x * lax.rsqrt(jnp.mean(x * x, axis=-1, keepdims=True) + EPS) * g
        scale = 1.0 + sc_ref[r, :] if per_row else scale_all
        shift = sh_ref[r, :] if per_row else shift_all
        o_ref[r, :] = (y * scale + shift).astype(o_ref.dtype)
        return carry

    n_chunks = tm // ROW_CHUNK
    lax.fori_loop(0, n_chunks, body, 0, unroll=min(4, n_chunks))


def _modulate_pair_kernel(x_ref, ga_ref, sha_ref, sca_ref, gb_ref, shb_ref, scb_ref, oa_ref, ob_ref):
    tm = x_ref.shape[0]
    per_row = sha_ref.shape[0] == tm
    sets = ((ga_ref[...], sha_ref, sca_ref, oa_ref), (gb_ref[...], shb_ref, scb_ref, ob_ref))
    if not per_row:
        hoisted = [(1.0 + sc_ref[...], sh_ref[...]) for _, sh_ref, sc_ref, _ in sets]

    def body(c, carry):
        r = pl.ds(pl.multiple_of(c * ROW_CHUNK, ROW_CHUNK), ROW_CHUNK)
        x = x_ref[r, :]
        xn = x * lax.rsqrt(jnp.mean(x * x, axis=-1, keepdims=True) + EPS)
        for k, (g, sh_ref, sc_ref, o_ref) in enumerate(sets):
            scale, shift = (1.0 + sc_ref[r, :], sh_ref[r, :]) if per_row else hoisted[k]
            o_ref[r, :] = (xn * g * scale + shift).astype(o_ref.dtype)
        return carry

    n_chunks = tm // ROW_CHUNK
    lax.fori_loop(0, n_chunks, body, 0, unroll=min(4, n_chunks))


def modulate_pair(x, a, b, *, T, tm):
    M, D = x.shape
    fix = lambda s: pl.BlockSpec(s.block_shape, functools.partial(lambda im, i: im(0, i), s.index_map))
    args, specs = [], []
    for g, sh, sc in (a, b):
        sh_a, sh_s = _seq_vec(sh, D, T, tm)
        sc_a, sc_s = _seq_vec(sc, D, T, tm)
        args += [g.reshape(1, D), sh_a, sc_a]
        specs += [pl.BlockSpec((1, D), lambda i: (0, 0)), fix(sh_s), fix(sc_s)]
    row_spec = pl.BlockSpec((tm, D), lambda i: (i, 0))
    return pl.pallas_call(
        _modulate_pair_kernel,
        grid=(M // tm,),
        in_specs=[row_spec] + specs,
        out_specs=[row_spec, row_spec],
        out_shape=[jax.ShapeDtypeStruct((M, D), BF)] * 2,
        compiler_params=_params(("arbitrary",)),
        name="modulate_pair",
    )(x, *args)


def _modulate_interleave_kernel(x_ref, g_ref, sh_ref, sc_ref, o_ref):
    D = x_ref.shape[1]
    x = x_ref[...]
    rstd = lax.rsqrt(jnp.mean(x * x, axis=-1, keepdims=True) + EPS)
    for c in range(D // LANES):
        cols = slice(c * LANES, (c + 1) * LANES)
        y = x_ref[:, cols] * rstd * g_ref[:, cols]
        y = y * (1.0 + sc_ref[:, cols]) + sh_ref[:, cols]
        y = pltpu.einshape("sqd->qsd", y.reshape(SUBLANES, PERM_Q, LANES)).reshape(PERM_GROUP, LANES)
        o_ref[:, cols] = y.astype(o_ref.dtype)


def modulate(x, g, sh, sc, *, T, tm, interleave=False):
    M, D = x.shape
    assert not interleave or (tm == PERM_GROUP and T > 1)
    sh_a, sh_s = _seq_vec(sh, D, T, tm)
    sc_a, sc_s = _seq_vec(sc, D, T, tm)
    fix = lambda s: pl.BlockSpec(s.block_shape, functools.partial(lambda im, i: im(0, i), s.index_map))
    return pl.pallas_call(
        _modulate_interleave_kernel if interleave else _modulate_kernel,
        grid=(M // tm,),
        in_specs=[pl.BlockSpec((tm, D), lambda i: (i, 0)),
                  pl.BlockSpec((1, D), lambda i: (0, 0)),
                  fix(sh_s), fix(sc_s)],
        out_specs=pl.BlockSpec((tm, D), lambda i: (i, 0)),
        out_shape=jax.ShapeDtypeStruct((M, D), BF),
        compiler_params=_params(("arbitrary",)),
        name="modulate",
    )(x, g.reshape(1, D), sh_a, sc_a)


def _rmsnorm_kernel(x_ref, g_ref, o_ref):
    x = x_ref[...]
    o_ref[...] = x * lax.rsqrt(jnp.mean(x * x, axis=-1, keepdims=True) + EPS) * g_ref[...]


def rmsnorm(x, g, *, tm):
    M, D = x.shape
    return pl.pallas_call(
        _rmsnorm_kernel,
        grid=(M // tm,),
        in_specs=[pl.BlockSpec((tm, D), lambda i: (i, 0)), pl.BlockSpec((1, D), lambda i: (0, 0))],
        out_specs=pl.BlockSpec((tm, D), lambda i: (i, 0)),
        out_shape=jax.ShapeDtypeStruct((M, D), F32),
        compiler_params=_params(("arbitrary",)),
        name="final_norm",
    )(x, g.reshape(1, D))


def _conv_gate(ug, uu, pg1, pg2, pu1, pu2, wcg, wcu, bcg, bcu):
    yg = bcg + pg2 * wcg[0:1] + pg1 * wcg[1:2] + ug * wcg[2:3]
    yu = bcu + pu2 * wcu[0:1] + pu1 * wcu[1:2] + uu * wcu[2:3]
    return yg * _sigmoid(yg) * yu


SUBLANES = 8
PERM_GROUP = 512
PERM_Q = PERM_GROUP // SUBLANES


def _ffn_in_kernel(x_ref, wg_ref, wu_ref, wcg_ref, wcu_ref, bcg_ref, bcu_ref,
                   act_ref, cg_ref, cu_ref, wqg_ref, wqu_ref, wb0_ref, wb1_ref, car_ref, *, tps, n_sub):
    j = pl.program_id(0)
    i = pl.program_id(1)
    tm = x_ref.shape[0]
    tn = wg_ref.shape[1]
    G = PERM_GROUP
    assert tm == n_sub * G
    wb_refs = (wb0_ref, wb1_ref)

    def stage(slot):
        _stage_weight(i, wg_ref, wb_refs[slot], 0, wqg_ref)
        _stage_weight(i, wu_ref, wb_refs[slot], tn, wqu_ref)

    def step(slot):
        w = wb_refs[slot][...]
        wcg, wcu, bcg, bcu = wcg_ref[...], wcu_ref[...], bcg_ref[...], bcu_ref[...]
        car = car_ref[...]
        first = lax.broadcasted_iota(jnp.int32, (SUBLANES, 2 * tn), 0) == 0
        for r in range(n_sub):
            rows = slice(r * G, (r + 1) * G)
            u = jnp.dot(x_ref[rows, :], w, preferred_element_type=F32)
            fix1 = jnp.where(first, car[1:2], pltpu.roll(u[G - 8:G], 1, axis=0))
            fix2 = jnp.where(first, car[0:1], pltpu.roll(u[G - 16:G - 8], 1, axis=0))
            p1 = jnp.concatenate([fix1, u[:G - 8]], axis=0)
            p2 = jnp.concatenate([fix2, fix1, u[:G - 16]], axis=0)
            act_ref[rows, :] = _conv_gate(u[:, :tn], u[:, tn:], p1[:, :tn], p2[:, :tn], p1[:, tn:], p2[:, tn:],
                                          wcg, wcu, bcg, bcu).astype(act_ref.dtype)
            car = jnp.concatenate([u[G - 9:G - 8], u[G - 1:G]], axis=0)
        car_ref[...] = car
        cg_ref[...] = car[:, :tn]
        cu_ref[...] = car[:, tn:]

    def reset_history():
        @pl.when(i % tps == 0)
        def _():
            car_ref[...] = jnp.zeros_like(car_ref)

    _by_slot(j, stage, step, before=reset_history)


def ffn_in_prompt(x, w3, lead, w_conv, b_conv, *, B, T, tm, tn, n_sub):
    M, K = x.shape
    F = D_FF
    nj = F // tn
    ni = M // tm
    ck = K // ni
    assert ck * ni == K and ck % 16 == 0
    tps = T // tm
    wc = w_conv
    bc = b_conv.reshape(1, 2 * F)
    chunk = lambda j, i: _chunk(j, i, nj, ni)
    wq_spec = pl.BlockSpec((None, ck, tn), lambda j, i: (0, chunk(j, i)[1], chunk(j, i)[0]))
    act, cg, cu, wq_g, wq_u = pl.pallas_call(
        functools.partial(_ffn_in_kernel, tps=tps, n_sub=n_sub),
        grid=(nj + 1, ni),
        in_specs=[
            pl.BlockSpec((tm, K), lambda j, i: (_lag_row(j, i), 0)),
            pl.BlockSpec((None, ck, tn), lambda j, i: (lead, chunk(j, i)[1], chunk(j, i)[0])),
            pl.BlockSpec((None, ck, tn), lambda j, i: (lead, chunk(j, i)[1], chunk(j, i)[0] + nj)),
            pl.BlockSpec((3, tn), lambda j, i: (0, _lag_col(j))),
            pl.BlockSpec((3, tn), lambda j, i: (0, _lag_col(j) + nj)),
            pl.BlockSpec((1, tn), lambda j, i: (0, _lag_col(j))),
            pl.BlockSpec((1, tn), lambda j, i: (0, _lag_col(j) + nj)),
        ],
        out_specs=[
            pl.BlockSpec((tm, tn), lambda j, i: (_lag_row(j, i), _lag_col(j))),
            pl.BlockSpec((None, 2, tn), lambda j, i: (_lag_row(j, i) // tps, 0, _lag_col(j))),
            pl.BlockSpec((None, 2, tn), lambda j, i: (_lag_row(j, i) // tps, 0, _lag_col(j))),
            wq_spec, wq_spec,
        ],
        out_shape=[
            jax.ShapeDtypeStruct((M, F), BF),
            jax.ShapeDtypeStruct((B, 2, F), F32),
            jax.ShapeDtypeStruct((B, 2, F), F32),
            jax.ShapeDtypeStruct((1, K, F), BF),
            jax.ShapeDtypeStruct((1, K, F), BF),
        ],
        scratch_shapes=[pltpu.VMEM((K, 2 * tn), BF), pltpu.VMEM((K, 2 * tn), BF),
                        pltpu.VMEM((2, 2 * tn), F32)],
        compiler_params=_params(("arbitrary", "arbitrary")),
        name="ffn_in_p",
    )(x, w3, w3, wc, wc, bc, bc)
    return act, jnp.concatenate([cg, cu], axis=-1), (wq_g, wq_u)


def _ffn_in_step_kernel(x_ref, wg_ref, wu_ref, wcg_ref, wcu_ref, bcg_ref, bcu_ref,
                        c0g_ref, c0u_ref, c1g_ref, c1u_ref, act_ref, ug_ref, uu_ref):
    x = x_ref[...]
    ug = jnp.dot(x, wg_ref[...].astype(BF), preferred_element_type=F32)
    uu = jnp.dot(x, wu_ref[...].astype(BF), preferred_element_type=F32)
    act_ref[...] = _conv_gate(ug, uu, c1g_ref[...], c0g_ref[...], c1u_ref[...], c0u_ref[...],
                              wcg_ref[...], wcu_ref[...], bcg_ref[...], bcu_ref[...]).astype(act_ref.dtype)
    ug_ref[...] = ug
    uu_ref[...] = uu


def ffn_in_step(x, wg3, wu3, w_conv, b_conv, cache, *, tn):
    Bs, K = x.shape
    F = D_FF
    nj = F // tn
    bc = b_conv.reshape(1, 2 * F)
    cflat = cache.reshape(Bs, 4 * F)
    vec = lambda off: pl.BlockSpec((Bs, tn), lambda j: (0, j + off * nj))
    act, ug, uu = pl.pallas_call(
        _ffn_in_step_kernel,
        grid=(nj,),
        in_specs=[
            pl.BlockSpec((Bs, K), lambda j: (0, 0)),
            pl.BlockSpec((None, K, tn), lambda j: (0, 0, j)),
            pl.BlockSpec((None, K, tn), lambda j: (0, 0, j)),
            pl.BlockSpec((3, tn), lambda j: (0, j)),
            pl.BlockSpec((3, tn), lambda j: (0, j + nj)),
            pl.BlockSpec((1, tn), lambda j: (0, j)),
            pl.BlockSpec((1, tn), lambda j: (0, j + nj)),
            vec(0), vec(1), vec(2), vec(3),
        ],
        out_specs=[pl.BlockSpec((Bs, tn), lambda j: (0, j))] * 3,
        out_shape=[jax.ShapeDtypeStruct((Bs, F), BF),
                   jax.ShapeDtypeStruct((Bs, F), F32),
                   jax.ShapeDtypeStruct((Bs, F), F32)],
        compiler_params=_params(("arbitrary",)),
        name="ffn_in_s",
    )(x, wg3, wu3, w_conv, w_conv, bc, bc, cflat, cflat, cflat, cflat)
    new_cache = jnp.stack([cache[:, 1, :], jnp.concatenate([ug, uu], axis=-1)], axis=1)
    return act, new_cache


def _mlstm_chunk_kernel(q_ref, k_ref, v_ref, o_ref, g_ref, gt_ref, gh_ref,
                        hs_ref, C_ref, n_ref, m_ref, *, hb):
    hblk = pl.program_id(1)
    c = pl.program_id(2)
    L = q_ref.shape[0]

    @pl.when(c == 0)
    def _():
        C_ref[...] = jnp.zeros_like(C_ref)
        n_ref[...] = jnp.zeros_like(n_ref)
        m_ref[...] = jnp.zeros_like(m_ref)

    g = g_ref[...]
    gt = gt_ref[...]
    lane = lax.broadcasted_iota(jnp.int32, g.shape, 1)
    sub = lax.broadcasted_iota(jnp.int32, gt.shape, 0)
    t_idx = lax.broadcasted_iota(jnp.int32, (L, L), 0)
    s_idx = lax.broadcasted_iota(jnp.int32, (L, L), 1)
    causal = s_idx <= t_idx
    q_scale = jnp.asarray(M_QK ** -0.5, BF)

    for hh_i in range(hb):
        h = hblk * hb + hh_i
        ic_col = jnp.sum(jnp.where(lane == h, g, 0.0), axis=1, keepdims=True)
        fc_col = jnp.sum(jnp.where(lane == h + M_HEADS, g, 0.0), axis=1, keepdims=True)
        ic_row = jnp.sum(jnp.where(sub == h, gt, 0.0), axis=0, keepdims=True)
        fc_row = jnp.sum(jnp.where(sub == h + M_HEADS, gt, 0.0), axis=0, keepdims=True)
        b_col = jnp.sum(jnp.where(causal, fc_row, 0.0), axis=1, keepdims=True)
        b_row = jnp.sum(jnp.where(t_idx <= s_idx, fc_col, 0.0), axis=0, keepdims=True)

        m_old = m_ref[hh_i][0:1, 0:1]
        d = jnp.where(causal, b_col - b_row + ic_row, -jnp.inf)
        inter = b_col + m_old
        mt = jnp.maximum(inter, jnp.max(d, axis=1, keepdims=True))
        w = jnp.exp(d - mt)
        si = jnp.exp(inter - mt)

        qb = q_ref[:, hh_i * M_QK:(hh_i + 1) * M_QK] * q_scale
        kb = k_ref[:, hh_i * M_QK:(hh_i + 1) * M_QK]
        vb = v_ref[:, hh_i * M_V:(hh_i + 1) * M_V]
        C_old = C_ref[hh_i]
        n_old = n_ref[hh_i]
        qk = lax.dot_general(qb, kb, (((1,), (1,)), ((), ())), preferred_element_type=F32) * w
        num = (jnp.dot(qk.astype(BF), vb, preferred_element_type=F32)
               + si * jnp.dot(qb, C_old.astype(BF), preferred_element_type=F32))
        den = jnp.sum(qk, axis=1, keepdims=True) + si * jnp.sum(qb.astype(F32) * n_old, axis=1, keepdims=True)
        hh = num / jnp.maximum(jnp.abs(den), jnp.exp(-mt))

        b_last = b_col[L - 1:L, :]
        gl = b_last - b_col + ic_col
        m_new = jnp.maximum(b_last + m_old, jnp.max(gl, axis=0, keepdims=True))
        wl = jnp.exp(gl - m_new)
        sd = jnp.exp(b_last + m_old - m_new)
        kw = kb.astype(F32) * wl
        C_ref[hh_i] = sd * C_old + jnp.dot(kw.T.astype(BF), vb, preferred_element_type=F32)
        n_ref[hh_i] = sd * n_old + jnp.sum(kw, axis=0, keepdims=True)
        m_ref[hh_i] = jnp.broadcast_to(m_new, (1, LANES))

        hn = hh * lax.rsqrt(jnp.mean(hh * hh, axis=1, keepdims=True) + EPS)
        o = o_ref[:, hh_i * M_V:(hh_i + 1) * M_V].astype(F32)
        hs_ref[:, hh_i * M_V:(hh_i + 1) * M_V] = (
            hn * gh_ref[:, hh_i * M_V:(hh_i + 1) * M_V] * _sigmoid(o)).astype(hs_ref.dtype)


def mlstm_prompt(z, gates, g_head, *, B, T, L, hb):
    M = B * T
    nc = T // L
    H = M_HEADS
    gt = jnp.swapaxes(gates[:, :2 * H].reshape(B * nc, L, 2 * H), 1, 2)
    row = lambda b, h, c: b * nc + c
    nhb = H // hb
    kq = nhb
    kv = nhb
    ko = 2 * nhb
    hs, C, n, m = pl.pallas_call(
        functools.partial(_mlstm_chunk_kernel, hb=hb),
        grid=(B, nhb, nc),
        in_specs=[
            pl.BlockSpec((L, hb * M_QK), lambda b, h, c: (row(b, h, c), h)),
            pl.BlockSpec((L, hb * M_QK), lambda b, h, c: (row(b, h, c), kq + h)),
            pl.BlockSpec((L, hb * M_V), lambda b, h, c: (row(b, h, c), kv + h)),
            pl.BlockSpec((L, hb * M_V), lambda b, h, c: (row(b, h, c), ko + h)),
            pl.BlockSpec((L, LANES), lambda b, h, c: (row(b, h, c), 0)),
            pl.BlockSpec((None, 2 * H, L), lambda b, h, c: (row(b, h, c), 0, 0)),
            pl.BlockSpec((1, hb * M_V), lambda b, h, c: (0, h)),
        ],
        out_specs=[
            pl.BlockSpec((L, hb * M_V), lambda b, h, c: (row(b, h, c), h)),
            pl.BlockSpec((None, hb, M_QK, M_V), lambda b, h, c: (b, h, 0, 0)),
            pl.BlockSpec((None, hb, 1, M_QK), lambda b, h, c: (b, h, 0, 0)),
            pl.BlockSpec((None, hb, 1, LANES), lambda b, h, c: (b, h, 0, 0)),
        ],
        out_shape=[
            jax.ShapeDtypeStruct((M, D_MODEL), BF),
            jax.ShapeDtypeStruct((B, M_HEADS, M_QK, M_V), F32),
            jax.ShapeDtypeStruct((B, M_HEADS, 1, M_QK), F32),
            jax.ShapeDtypeStruct((B, M_HEADS, 1, LANES), F32),
        ],
        compiler_params=_params(("arbitrary", "arbitrary", "arbitrary")),
        name="mlstm_p",
    )(z, z, z, z, gates, gt, g_head.reshape(1, D_MODEL))
    return hs, C, n[:, :, 0, :], m[:, :, 0, 0]


def _mlstm_step_kernel(q_ref, k_ref, v_ref, o_ref, ig_ref, lf_ref, m0_ref, n0_ref, C0_ref, gh_ref,
                       hs_ref, C_ref, n_ref, m_ref):
    ic = ig_ref[...]
    fc = lf_ref[...]
    m_old = m0_ref[...]
    inter = fc + m_old
    mt = jnp.maximum(inter, ic)
    w = jnp.exp(ic - mt)
    si = jnp.exp(inter - mt)
    q = (q_ref[...] * (M_QK ** -0.5)).astype(BF).astype(F32)
    k = k_ref[...]
    v = v_ref[...]
    n_old = n0_ref[...]
    qk = jnp.sum(q * k.astype(BF).astype(F32), axis=1, keepdims=True) * w
    wl = jnp.exp(ic - mt)
    sd = jnp.exp(inter - mt)
    kw = k * wl
    qT = q.T
    kwT = kw.T
    rows = []
    for h in range(M_HEADS):
        C_old = C0_ref[h]
        rows.append(jnp.sum(qT[:, h:h + 1] * C_old, axis=0, keepdims=True))
        C_ref[h] = sd[h:h + 1, :] * C_old + kwT[:, h:h + 1] * v[h:h + 1, :]
    qC = jnp.concatenate(rows, axis=0)
    num = qk.astype(BF).astype(F32) * v.astype(BF).astype(F32) + si * qC
    den = qk + si * jnp.sum(q * n_old, axis=1, keepdims=True)
    hh = num / jnp.maximum(jnp.abs(den), jnp.exp(-mt))
    n_ref[...] = sd * n_old + kw
    m_ref[...] = mt
    hn = hh * lax.rsqrt(jnp.mean(hh * hh, axis=1, keepdims=True) + EPS)
    hs_ref[...] = (hn * gh_ref[...] * _sigmoid(o_ref[...])).astype(hs_ref.dtype)


def mlstm_step(z, gates, g_head, C0, n0, m0):
    Bs = z.shape[0]
    H = M_HEADS
    qd, vd = H * M_QK, H * M_V
    q = z[:, :qd].reshape(Bs, H, M_QK)
    k = z[:, qd:2 * qd].reshape(Bs, H, M_QK)
    v = z[:, 2 * qd:2 * qd + vd].reshape(Bs, H, M_V)
    o = z[:, 2 * qd + vd:].reshape(Bs, H, M_V)
    ig = gates[:, :H].reshape(Bs, H, 1)
    lf = gates[:, H:2 * H].reshape(Bs, H, 1)
    per_b = lambda *tail: pl.BlockSpec((None,) + tail, lambda b: (b,) + (0,) * len(tail))
    hs, C, n, m = pl.pallas_call(
        _mlstm_step_kernel,
        grid=(Bs,),
        in_specs=[
            per_b(H, M_QK), per_b(H, M_QK), per_b(H, M_V), per_b(H, M_V),
            per_b(H, 1), per_b(H, 1), per_b(H, 1),
            pl.BlockSpec((None, None, H, M_QK), lambda b: (0, b, 0, 0)),
            pl.BlockSpec((None, None, H, M_QK, M_V), lambda b: (0, b, 0, 0, 0)),
            pl.BlockSpec((H, M_V), lambda b: (0, 0)),
        ],
        out_specs=[
            per_b(H, M_V),
            pl.BlockSpec((None, None, H, M_QK, M_V), lambda b: (0, b, 0, 0, 0)),
            pl.BlockSpec((None, None, H, M_QK), lambda b: (0, b, 0, 0)),
            per_b(H, 1),
        ],
        out_shape=[
            jax.ShapeDtypeStruct((Bs, H, M_V), BF),
            jax.ShapeDtypeStruct(C0.shape, F32),
            jax.ShapeDtypeStruct(n0.shape, F32),
            jax.ShapeDtypeStruct((Bs, H, 1), F32),
        ],
        compiler_params=_params(("arbitrary",)),
        name="mlstm_s",
    )(q, k, v, o, ig, lf, m0.reshape(Bs, H, 1), n0, C0, g_head.reshape(H, M_V))
    return hs.reshape(Bs, D_MODEL), C, n, m.reshape(1, Bs, H)


def _attn_kernel(sink_ref, q_ref, kc_ref, kp_ref, vc_ref, vp_ref, o_ref):
    n = pl.program_id(1)
    W = WINDOW
    R = GROUP * W
    kp = kp_ref[...].astype(BF)
    kc = kc_ref[...].astype(BF)
    vpT = vp_ref[...].T.astype(BF)
    vcT = vc_ref[...].T.astype(BF)
    j_idx = lax.broadcasted_iota(jnp.int32, (W, R), 0)
    i_idx = lax.broadcasted_iota(jnp.int32, (W, R), 1) % W
    from_prev = j_idx > i_idx
    lane_g = lax.broadcasted_iota(jnp.int32, (1, R), 1) // W
    no_prev = jnp.where(n > 0, 0.0, -jnp.inf)
    nt = (((1,), (1,)), ((), ()))
    for h in range(KV_HEADS):
        hs = slice(h * HEAD_DIM, (h + 1) * HEAD_DIM)
        qh = jnp.concatenate(
            [q_ref[:, (h * GROUP + g) * HEAD_DIM:(h * GROUP + g + 1) * HEAD_DIM] for g in range(GROUP)], axis=0)
        sp = lax.dot_general(kp[:, hs], qh, nt, preferred_element_type=F32)
        sc = lax.dot_general(kc[:, hs], qh, nt, preferred_element_type=F32)
        s = jnp.where(from_prev, sp + no_prev, sc)
        sk = jnp.zeros((1, R), F32)
        for g in range(GROUP):
            sk = jnp.where(lane_g == g, sink_ref[h * GROUP + g], sk)
        mx = jnp.maximum(jnp.max(s, axis=0, keepdims=True), sk)
        p = jnp.exp(s - mx)
        p = p / (jnp.sum(p, axis=0, keepdims=True) + jnp.exp(sk - mx))
        oT = (jnp.dot(vpT[hs, :], jnp.where(from_prev, p, 0.0).astype(BF), preferred_element_type=F32)
              + jnp.dot(vcT[hs, :], jnp.where(from_prev, 0.0, p).astype(BF), preferred_element_type=F32))
        for g in range(GROUP):
            c0 = (h * GROUP + g) * HEAD_DIM
            o_ref[:, c0:c0 + HEAD_DIM] = oT[:, g * W:(g + 1) * W].T.astype(o_ref.dtype)


def attention_prompt(q, k, v, sinks, *, B, T):
    M = B * T
    W = WINDOW
    nb = T // W
    kvd = KV_HEADS * HEAD_DIM
    cur = lambda b, n: (b * nb + n, 0)
    prev = lambda b, n: (b * nb + jnp.maximum(n - 1, 0), 0)
    return pl.pallas_call(
        _attn_kernel,
        grid=(B, nb),
        in_specs=[
            pl.BlockSpec(memory_space=pltpu.SMEM),
            pl.BlockSpec((W, D_MODEL), cur),
            pl.BlockSpec((W, kvd), cur), pl.BlockSpec((W, kvd), prev),
            pl.BlockSpec((W, kvd), cur), pl.BlockSpec((W, kvd), prev),
        ],
        out_specs=pl.BlockSpec((W, D_MODEL), cur),
        out_shape=jax.ShapeDtypeStruct((M, D_MODEL), BF),
        compiler_params=_params(("arbitrary", "arbitrary")),
        name="attn_p",
    )(sinks, q, k, k, v, v)


def _per_head_rows(row):
    return jnp.concatenate(
        [jnp.broadcast_to(row[:, h * HEAD_DIM:(h + 1) * HEAD_DIM], (GROUP, HEAD_DIM)) for h in range(KV_HEADS)],
        axis=0)


def _attn_step_kernel(q_ref, kc_ref, vc_ref, kn_ref, vn_ref, sk_ref, o_ref, ko_ref, vo_ref, *, bb):
    W = WINDOW
    jl = lax.broadcasted_iota(jnp.int32, (A_HEADS, W), 1)
    sk = sk_ref[...]
    heads = lambda a, h: a[h * GROUP:(h + 1) * GROUP, :]
    scores = []
    for b in range(bb):
        q = q_ref[b]
        kcb = kc_ref[b].astype(BF)
        s = jnp.concatenate(
            [lax.dot_general(heads(q, h).astype(BF), kcb[:, h * HEAD_DIM:(h + 1) * HEAD_DIM],
                             (((1,), (1,)), ((), ())), preferred_element_type=F32) for h in range(KV_HEADS)],
            axis=0)
        kne = _per_head_rows(kn_ref[b]).astype(BF).astype(F32)
        sn = jnp.sum(q.astype(BF).astype(F32) * kne, axis=1, keepdims=True)
        scores.append((s, sn))
    probs = []
    for s, sn in scores:
        s = jnp.where(jl >= 1, s, -jnp.inf)
        mx = jnp.maximum(jnp.maximum(jnp.max(s, axis=1, keepdims=True), sn), sk)
        p = jnp.exp(s - mx)
        pn = jnp.exp(sn - mx)
        den = jnp.sum(p, axis=1, keepdims=True) + pn + jnp.exp(sk - mx)
        probs.append((p / den, pn / den))
    for b in range(bb):
        p, pn = probs[b]
        vcb = vc_ref[b].astype(BF)
        o = jnp.concatenate(
            [jnp.dot(heads(p, h).astype(BF), vcb[:, h * HEAD_DIM:(h + 1) * HEAD_DIM],
                     preferred_element_type=F32) for h in range(KV_HEADS)], axis=0)
        vne = _per_head_rows(vn_ref[b]).astype(BF).astype(F32)
        o_ref[b] = o + pn.astype(BF).astype(F32) * vne
        ko_ref[b, 0:W - 1, :] = kc_ref[b, 1:W, :]
        ko_ref[b, W - 1:W, :] = kn_ref[b]
        vo_ref[b, 0:W - 1, :] = vc_ref[b, 1:W, :]
        vo_ref[b, W - 1:W, :] = vn_ref[b]


def attention_step(q, k_new, v_new, kbuf, vbuf, sinks):
    Bs = q.shape[0]
    W = WINDOW
    kvd = KV_HEADS * HEAD_DIM
    bb = 4
    assert Bs % bb == 0
    per_b = lambda *tail: pl.BlockSpec((bb,) + tail, lambda b: (b,) + (0,) * len(tail))
    o, ko, vo = pl.pallas_call(
        functools.partial(_attn_step_kernel, bb=bb),
        grid=(Bs // bb,),
        in_specs=[per_b(A_HEADS, HEAD_DIM), per_b(W, kvd), per_b(W, kvd), per_b(1, kvd), per_b(1, kvd),
                  pl.BlockSpec((A_HEADS, 1), lambda b: (0, 0))],
        out_specs=[per_b(A_HEADS, HEAD_DIM), per_b(W, kvd), per_b(W, kvd)],
        out_shape=[jax.ShapeDtypeStruct((Bs, A_HEADS, HEAD_DIM), F32),
                   jax.ShapeDtypeStruct((Bs, W, kvd), F32),
                   jax.ShapeDtypeStruct((Bs, W, kvd), F32)],
        compiler_params=_params(("arbitrary",)),
        name="attn_s",
    )(q.reshape(Bs, A_HEADS, HEAD_DIM), kbuf.reshape(Bs, W, kvd), vbuf.reshape(Bs, W, kvd),
      k_new.reshape(Bs, 1, kvd), v_new.reshape(Bs, 1, kvd), sinks.reshape(A_HEADS, 1))
    return (o.reshape(Bs, D_MODEL), ko.reshape(Bs, W, KV_HEADS, HEAD_DIM), vo.reshape(Bs, W, KV_HEADS, HEAD_DIM))


def _rope_tables(pos, width):
    half = HEAD_DIM // 2
    freq = ROPE_THETA ** (-jnp.arange(half, dtype=F32) / half)
    ang = pos.astype(F32)[:, None] * freq[None, :]
    cos = jnp.cos(ang)
    sin = jnp.sin(ang)
    reps = width // HEAD_DIM
    return (jnp.tile(jnp.concatenate([cos, cos], axis=1), (1, reps)),
            jnp.tile(jnp.concatenate([-sin, sin], axis=1), (1, reps)))


def _tiles(T):
    if T > 1:
        return dict(tm=1024, tn=1024, n_sub=2,
                    tm_r=512, tn_r=1024, n_sub_r=1,
                    tn_kv=512, tm_ffi=2048, tn_ffi=256, n_sub_ffi=4,
                    tm_ffo=512, tn_ffo=512, n_sub_ffo=1, tm_mod=512)
    return dict(tm=None, tn=1024, n_sub=1, tm_r=None, tn_r=1024, n_sub_r=1, tn_kv=512,
                tm_ffi=None, tn_ffi=256, n_sub_ffi=1, tm_ffo=None, tn_ffo=512, n_sub_ffo=1, tm_mod=None)


def _run(x, ada, pos, state, P, *, B, T, wq_in=None):
    D = D_MODEL
    M = B * T
    c = {k: (M if v is None else v) for k, v in _tiles(T).items()}
    tm, tn, tm_r, tn_r, tm_mod = c["tm"], c["tn"], c["tm_r"], c["tn_r"], c["tm_mod"]
    cos, sin = _rope_tables(pos, LANES)
    if T > 1:
        tps = T // tm
        rope_spec = pl.BlockSpec((tm, LANES), lambda j, i: (i % tps, 0))
    else:
        rope_spec = pl.BlockSpec((1, LANES), lambda j, i: (0, 0))

    out = {}
    tag = "_p" if T > 1 else "_s"
    wq = {}

    def lin(key, x_, w3, lead, **kw):
        if wq_in is None:
            res, wq[key] = linear(x_, w3, lead, keep_bf16=True, **kw)
            return res
        kw.pop("col_off", None)
        return linear(x_, wq_in[key], 0, **kw)

    sh1, sc1, ga1, sh2, sc2, ga2 = ada[0]
    hn = modulate(x, P["g_norm1"][0], sh1, sc1, T=T, tm=tm_mod)
    qd, vd = M_HEADS * M_QK, M_HEADS * M_V
    w_inT = jnp.swapaxes(P["w_m_in"], 1, 2)
    z = lin("m_in", hn, w_inT, 0, name="m_in" + tag, n_cols=2 * qd + 2 * vd, tm=tm, tn=tn, n_sub=c["n_sub"],
               epilogue=_ep_plain, transposed=True, out_dtype=BF if T > 1 else F32)
    wgT = jnp.pad(w_inT[:, 2 * qd + 2 * vd:, :], ((0, 0), (0, LANES - 2 * M_HEADS), (0, 0)))
    bg = jnp.pad(jnp.concatenate([P["b_m_i"][0], P["b_m_f"][0]]), (0, LANES - 2 * M_HEADS))
    gates = linear(hn, wgT, 0, name="m_gates" + tag, n_cols=LANES, tm=tm, tn=LANES, epilogue=_ep_gates,
                   extras=[_row_vec(bg, LANES)], transposed=True)
    if T > 1:
        hs, C, n, m = mlstm_prompt(z, gates, P["g_m_head"][0], B=B, T=T, L=256, hb=8)
        out["C"], out["n"], out["m"] = C[None], n[None], m[None]
    else:
        hs, out["C"], out["n"], out["m"] = mlstm_step(z, gates, P["g_m_head"][0], *state["mlstm"])
    x = lin("m_out", hs, P["w_m_out"], 0, name="m_out" + tag, n_cols=D, tm=tm_r, tn=tn_r, n_sub=c["n_sub_r"],
               epilogue=_ep_resid, extras=[_tile(x, tn_r, T, tm_r), _seq_vec(ga1, tn_r, T, tm_r)])

    convs = []
    for l in range(2):
        if l == 1:
            sh1, sc1, ga1, sh2, sc2, ga2 = ada[1]
            shk, sck = ada[2]
            hk, hn = modulate_pair(x, (P["g_kv"], shk, sck), (P["g_norm1"][1], sh1, sc1), T=T, tm=tm_mod)
            kvd = KV_HEADS * HEAD_DIM
            tn_kv = c["tn_kv"]
            k = lin("kv_k", hk, P["w_kv"], 0, name="kv_k" + tag, n_cols=kvd, tm=tm, tn=tn_kv, n_sub=c["n_sub"],
                       epilogue=functools.partial(_ep_rope, scale=1.0),
                       extras=[_row_vec(P["b_kv"][:kvd], tn_kv), (cos, rope_spec), (sin, rope_spec)])
            v = lin("kv_v", hk, P["w_kv"], 0, name="kv_v" + tag, n_cols=kvd, tm=tm, tn=tn_kv, n_sub=c["n_sub"],
                       col_off=kvd // tn_kv, epilogue=_ep_bias, extras=[_row_vec(P["b_kv"][kvd:], tn_kv)])
            q = lin("attn_q", hn, P["w_q"], 0, name="attn_q" + tag, n_cols=D, tm=tm, tn=tn, n_sub=c["n_sub"],
                       epilogue=functools.partial(_ep_rope, scale=HEAD_DIM ** -0.5),
                       extras=[_row_vec(P["b_q"][0], tn), (cos, rope_spec), (sin, rope_spec)],
                       out_dtype=BF if T > 1 else F32)
            if T > 1:
                o = attention_prompt(q, k, v, P["sinks"][0], B=B, T=T)
                last = lambda a: a.reshape(B, T, kvd)[:, T - WINDOW:, :].reshape(B, WINDOW, KV_HEADS, HEAD_DIM)
                out["k_win"], out["v_win"] = last(k), last(v)
            else:
                o, out["k_win"], out["v_win"] = attention_step(q, k, v, state["kbuf"], state["vbuf"], P["sinks"][0])
            x = lin("attn_o", o, P["w_o"], 0, name="attn_o" + tag, n_cols=D, tm=tm_r, tn=tn_r, n_sub=c["n_sub_r"],
                       epilogue=_ep_resid_bias,
                       extras=[_row_vec(P["b_o"][0], tn_r), _tile(x, tn_r, T, tm_r), _seq_vec(ga1, tn_r, T, tm_r)])

        il = T > 1
        hn = modulate(x, P["g_norm2"][l], sh2, sc2, T=T, tm=PERM_GROUP if il else tm_mod, interleave=il)
        if T > 1:
            act, cb, wq["ffn_in%d" % l] = ffn_in_prompt(hn, P["w_ffn_in"], l, P["w_conv"][l], P["b_conv"][l], B=B, T=T,
                                                        tm=c["tm_ffi"], tn=c["tn_ffi"], n_sub=c["n_sub_ffi"])
        else:
            act, cb = ffn_in_step(hn, *wq_in["ffn_in%d" % l], P["w_conv"][l], P["b_conv"][l], state["conv"][l],
                                  tn=c["tn_ffi"])
        convs.append(cb)
        tm_o, tn_o = c["tm_ffo"], c["tn_ffo"]
        x = lin("ffn_out%d" % l, act, P["w_ffn_out"], l, name="ffn_out" + tag, n_cols=D, tm=tm_o, tn=tn_o, n_sub=c["n_sub_ffo"],
                   epilogue=_ep_resid, extras=[_tile(x, tn_o, T, tm_o), _seq_vec(ga2, tn_o, T, tm_o)],
                   rows_interleaved=il)

    out["conv"] = jnp.stack(convs)
    out["y"] = rmsnorm(x, P["g_final"], tm=tm_mod)
    out["wq"] = wq
    return out


def kernel(x_prompt, x_sample, state_mlstm_C, state_mlstm_n, state_mlstm_m, cache_conv, cache_k_win, cache_v_win, c_prompt, c_sample, w_ada, g_norm1, g_norm2, w_m_in, b_m_i, b_m_f, g_m_head, w_m_out, w_ada_kv, g_kv, w_kv, b_kv, w_q, b_q, sinks, w_o, b_o, w_ffn_in, w_conv, b_conv, w_ffn_out, g_final):
    D = D_MODEL
    Bp, Tp, _ = x_prompt.shape
    Bs, Ts, _ = x_sample.shape
    assert Ts == 1
    P = dict(g_norm1=g_norm1, g_norm2=g_norm2, w_m_in=w_m_in, b_m_i=b_m_i, b_m_f=b_m_f, g_m_head=g_m_head,
             w_m_out=w_m_out, g_kv=g_kv, w_kv=w_kv[None], b_kv=b_kv, w_q=w_q, b_q=b_q, sinks=sinks, w_o=w_o,
             b_o=b_o, w_ffn_in=w_ffn_in, w_conv=w_conv, b_conv=b_conv, w_ffn_out=w_ffn_out, g_final=g_final)

    n_c = Bp + Bs
    pad_c = -n_c % 16
    cs = silu_cast(jnp.concatenate([c_prompt, c_sample, jnp.zeros((pad_c, D), F32)], axis=0))
    rows_c = n_c + pad_c
    ada_all = []
    for l in range(2):
        a = linear(cs, w_ada, l, name="ada", n_cols=6 * D, tm=rows_c, tn=512, epilogue=_ep_plain)
        ada_all.append(a)
    a_kv = linear(cs, w_ada_kv[None], 0, name="ada_kv", n_cols=2 * D, tm=rows_c, tn=512, epilogue=_ep_plain)

    def split(lo, hi):
        per_layer = [tuple(a[lo:hi, i * D:(i + 1) * D] for i in range(6)) for a in ada_all]
        return per_layer + [(a_kv[lo:hi, :D], a_kv[lo:hi, D:])]

    po = _run(x_prompt.reshape(Bp * Tp, D), split(0, Bp), jnp.arange(Tp, dtype=jnp.int32), None, P, B=Bp, T=Tp)
    state = dict(mlstm=(state_mlstm_C, state_mlstm_n, state_mlstm_m), conv=cache_conv,
                 kbuf=cache_k_win, vbuf=cache_v_win)
    so = _run(x_sample.reshape(Bs, D), split(Bp, Bp + Bs), PAST_LEN + jnp.arange(1, dtype=jnp.int32), state, P,
              B=Bs, T=1, wq_in=po["wq"])
    return (po["y"].reshape(Bp, Tp, D), so["y"].reshape(Bs, 1, D),
            po["C"], po["n"], po["m"], po["conv"], po["k_win"], po["v_win"],
            so["C"], so["n"], so["m"], so["conv"], so["k_win"], so["v_win"])
```

```python
import functools

import jax
import jax.numpy as jnp
from jax import lax
from jax.experimental import pallas as pl
from jax.experimental.pallas import tpu as pltpu

BF = jnp.bfloat16
F32 = jnp.float32

D_MODEL = 4096
M_HEADS = 8
M_QK = 256
M_V = 512
M_CHUNK = 64
GATE_CAP = 15.0
HEAD_DIM = 64
A_HEADS = 64
KV_HEADS = 8
GROUP = 8
WINDOW = 128
ROPE_THETA = 10000.0
D_FF = 11008
PAST_LEN = 16384
EPS = 1e-6

VMEM_LIMIT_V7X = 58 * 1024 * 1024
LANES = 128


def _params(sem):
    return pltpu.CompilerParams(dimension_semantics=sem, vmem_limit_bytes=VMEM_LIMIT_V7X)


def _sigmoid(x):
    return 1.0 / (1.0 + jnp.exp(-x))


def _lag_row(j, i):
    return i * jnp.minimum(j, 1)


def _lag_col(j):
    return jnp.maximum(j - 1, 0)


def _lagged(index_map):
    return lambda j, i: index_map(_lag_col(j), _lag_row(j, i))


def _chunk(j, i, nj, ni):
    last = j // nj
    return jnp.minimum(j, nj - 1), i + last * (ni - 1 - i)


def _stage_weight(i, w_ref, wb_ref, col0=0, wq_ref=None):
    ck, cw = w_ref.shape
    wq = w_ref[...].astype(BF)
    wb_ref[pl.ds(pl.multiple_of(i * ck, ck), ck), col0:col0 + cw] = wq
    if wq_ref is not None:
        wq_ref[...] = wq


def _by_slot(j, stage, step, before=None):
    @pl.when(j == 0)
    def _():
        stage(0)

    for parity in (0, 1):
        @pl.when((j > 0) & (j % 2 == parity))
        def _():
            if before is not None:
                before()
            stage(parity)
            step(1 - parity)


def _linear_kernel(x_ref, w_ref, *rest, n_extra, epilogue, n_sub, transposed, rows_interleaved, keep_bf16):
    extra = rest[:n_extra]
    o_ref = rest[n_extra]
    wq_ref = rest[n_extra + 1] if keep_bf16 else None
    wb_refs = rest[-2:]
    j = pl.program_id(0)
    i = pl.program_id(1)
    tm = x_ref.shape[0]
    sub = tm // n_sub

    def stage(slot):
        _stage_weight(i, w_ref, wb_refs[slot], wq_ref=wq_ref)

    def step(slot):
        w = wb_refs[slot][...]
        for r in range(n_sub):
            rows = slice(r * sub, (r + 1) * sub)
            xs = x_ref[rows, :].astype(BF)
            if transposed:
                acc = lax.dot_general(xs, w, (((1,), (1,)), ((), ())), preferred_element_type=F32)
            else:
                acc = jnp.dot(xs, w, preferred_element_type=F32)
            if rows_interleaved:
                assert sub == PERM_GROUP
                acc = pltpu.einshape("qsd->sqd", acc.reshape(PERM_Q, SUBLANES, acc.shape[1])).reshape(acc.shape)
            ex = [e[rows, :] if e.shape[0] == tm else e[...] for e in extra]
            o_ref[rows, :] = epilogue(acc, *ex).astype(o_ref.dtype)

    _by_slot(j, stage, step)


def linear(x, w3, lead, *, name, n_cols, tm, tn, epilogue, extras=(), out_dtype=F32,
           col_off=0, n_sub=1, transposed=False, rows_interleaved=False, keep_bf16=False):
    M, K = x.shape
    assert M % tm == 0 and n_cols % tn == 0 and tm % n_sub == 0
    nj, ni = n_cols // tn, M // tm
    chunk = lambda j, i: _chunk(j, i, nj, ni)
    if transposed:
        cn = tn // ni
        assert cn * ni == tn and cn % 16 == 0
        w_spec = pl.BlockSpec((None, cn, K),
                              lambda j, i: (lead, (chunk(j, i)[0] + col_off) * ni + chunk(j, i)[1], 0))
        wq_spec = pl.BlockSpec((None, cn, K), lambda j, i: (0, chunk(j, i)[0] * ni + chunk(j, i)[1], 0))
        wq_shape = (1, n_cols, K)
        wb_shape = (tn, K)
    else:
        ck = K // ni
        assert ck * ni == K and ck % 16 == 0
        w_spec = pl.BlockSpec((None, ck, tn), lambda j, i: (lead, chunk(j, i)[1], chunk(j, i)[0] + col_off))
        wq_spec = pl.BlockSpec((None, ck, tn), lambda j, i: (0, chunk(j, i)[1], chunk(j, i)[0]))
        wq_shape = (1, K, n_cols)
        wb_shape = (K, tn)
    in_specs = [pl.BlockSpec((tm, K), lambda j, i: (_lag_row(j, i), 0)), w_spec]
    in_specs += [pl.BlockSpec(s.block_shape, _lagged(s.index_map)) for _, s in extras]
    out_specs = [pl.BlockSpec((tm, tn), lambda j, i: (_lag_row(j, i), _lag_col(j)))]
    out_shape = [jax.ShapeDtypeStruct((M, n_cols), out_dtype)]
    if keep_bf16:
        out_specs.append(wq_spec)
        out_shape.append(jax.ShapeDtypeStruct(wq_shape, BF))
    res = pl.pallas_call(
        functools.partial(_linear_kernel, n_extra=len(extras), epilogue=epilogue, n_sub=n_sub,
                          transposed=transposed, rows_interleaved=rows_interleaved, keep_bf16=keep_bf16),
        grid=(nj + 1, ni),
        in_specs=in_specs,
        out_specs=out_specs,
        out_shape=out_shape,
        scratch_shapes=[pltpu.VMEM(wb_shape, BF)] * 2,
        compiler_params=_params(("arbitrary", "arbitrary")),
        name=name,
    )(x, w3, *[a for a, _ in extras])
    return tuple(res) if keep_bf16 else res[0]


def _row_vec(v, tn):
    return v.reshape(1, -1), pl.BlockSpec((1, tn), lambda j, i: (0, j))


def _seq_vec(a, tn, T, tm):
    if T == 1:
        return a, pl.BlockSpec((tm, tn), lambda j, i: (i, j))
    tps = T // tm
    return a.reshape(a.shape[0], 1, a.shape[1]), pl.BlockSpec((None, 1, tn), lambda j, i: (i // tps, 0, j))


def _tile(a, tn, T, tm, col0=0):
    return a, pl.BlockSpec((tm, tn), lambda j, i: (i, j + col0))


def _ep_plain(acc):
    return acc


def _ep_bias(acc, b):
    return acc + b


def _ep_resid(acc, r, g):
    return r + g * acc


def _ep_resid_bias(acc, b, r, g):
    return r + g * (acc + b)


def _ep_rope(acc, b, cos, sin, *, scale):
    y = acc + b
    n = y.shape[1]
    cos = jnp.concatenate([cos] * (n // LANES), axis=1)
    sin = jnp.concatenate([sin] * (n // LANES), axis=1)
    lane = lax.broadcasted_iota(jnp.int32, y.shape, 1)
    first_half = (lane % HEAD_DIM) < (HEAD_DIM // 2)
    partner = jnp.where(first_half, pltpu.roll(y, n - HEAD_DIM // 2, axis=1), pltpu.roll(y, HEAD_DIM // 2, axis=1))
    return (y * cos + partner * sin) * scale


def _ep_gates(acc, b):
    z = GATE_CAP * jnp.tanh((acc + b) / GATE_CAP)
    logsig = jnp.minimum(z, 0.0) - jnp.log(1.0 + jnp.exp(-jnp.abs(z)))
    lane = lax.broadcasted_iota(jnp.int32, z.shape, 1)
    return jnp.where(lane < M_HEADS, z, logsig)


def _silu_kernel(c_ref, o_ref):
    c = c_ref[...]
    o_ref[...] = (c * _sigmoid(c)).astype(o_ref.dtype)


def silu_cast(c):
    return pl.pallas_call(
        _silu_kernel,
        out_shape=jax.ShapeDtypeStruct(c.shape, BF),
        name="silu_c",
    )(c)


ROW_CHUNK = 16


def _modulate_kernel(x_ref, g_ref, sh_ref, sc_ref, o_ref):
    tm = x_ref.shape[0]
    per_row = sh_ref.shape[0] == tm
    g = g_ref[...]
    if not per_row:
        scale_all = 1.0 + sc_ref[...]
        shift_all = sh_ref[...]

    def body(c, carry):
        r = pl.ds(pl.multiple_of(c * ROW_CHUNK, ROW_CHUNK), ROW_CHUNK)
        x = x_ref[r, :]
        y = x * lax.rsqrt(jnp.mean(x * x, axis=-1, keepdims=True) + EPS) * g
        scale = 1.0 + sc_ref[r, :] if per_row else scale_all
        shift = sh_ref[r, :] if per_row else shift_all
        o_ref[r, :] = (y * scale + shift).astype(o_ref.dtype)
        return carry

    n_chunks = tm // ROW_CHUNK
    lax.fori_loop(0, n_chunks, body, 0, unroll=min(4, n_chunks))


def _modulate_pair_kernel(x_ref, ga_ref, sha_ref, sca_ref, gb_ref, shb_ref, scb_ref, oa_ref, ob_ref):
    tm = x_ref.shape[0]
    per_row = sha_ref.shape[0] == tm
    sets = ((ga_ref[...], sha_ref, sca_ref, oa_ref), (gb_ref[...], shb_ref, scb_ref, ob_ref))
    if not per_row:
        hoisted = [(1.0 + sc_ref[...], sh_ref[...]) for _, sh_ref, sc_ref, _ in sets]

    def body(c, carry):
        r = pl.ds(pl.multiple_of(c * ROW_CHUNK, ROW_CHUNK), ROW_CHUNK)
        x = x_ref[r, :]
        xn = x * lax.rsqrt(jnp.mean(x * x, axis=-1, keepdims=True) + EPS)
        for k, (g, sh_ref, sc_ref, o_ref) in enumerate(sets):
            scale, shift = (1.0 + sc_ref[r, :], sh_ref[r, :]) if per_row else hoisted[k]
            o_ref[r, :] = (xn * g * scale + shift).astype(o_ref.dtype)
        return carry

    n_chunks = tm // ROW_CHUNK
    lax.fori_loop(0, n_chunks, body, 0, unroll=min(4, n_chunks))


def modulate_pair(x, a, b, *, T, tm):
    M, D = x.shape
    fix = lambda s: pl.BlockSpec(s.block_shape, functools.partial(lambda im, i: im(0, i), s.index_map))
    args, specs = [], []
    for g, sh, sc in (a, b):
        sh_a, sh_s = _seq_vec(sh, D, T, tm)
        sc_a, sc_s = _seq_vec(sc, D, T, tm)
        args += [g.reshape(1, D), sh_a, sc_a]
        specs += [pl.BlockSpec((1, D), lambda i: (0, 0)), fix(sh_s), fix(sc_s)]
    row_spec = pl.BlockSpec((tm, D), lambda i: (i, 0))
    return pl.pallas_call(
        _modulate_pair_kernel,
        grid=(M // tm,),
        in_specs=[row_spec] + specs,
        out_specs=[row_spec, row_spec],
        out_shape=[jax.ShapeDtypeStruct((M, D), BF)] * 2,
        compiler_params=_params(("arbitrary",)),
        name="modulate_pair",
    )(x, *args)


def _modulate_interleave_kernel(x_ref, g_ref, sh_ref, sc_ref, o_ref):
    D = x_ref.shape[1]
    x = x_ref[...]
    rstd = lax.rsqrt(jnp.mean(x * x, axis=-1, keepdims=True) + EPS)
    for c in range(D // LANES):
        cols = slice(c * LANES, (c + 1) * LANES)
        y = x_ref[:, cols] * rstd * g_ref[:, cols]
        y = y * (1.0 + sc_ref[:, cols]) + sh_ref[:, cols]
        y = pltpu.einshape("sqd->qsd", y.reshape(SUBLANES, PERM_Q, LANES)).reshape(PERM_GROUP, LANES)
        o_ref[:, cols] = y.astype(o_ref.dtype)


def modulate(x, g, sh, sc, *, T, tm, interleave=False):
    M, D = x.shape
    assert not interleave or (tm == PERM_GROUP and T > 1)
    sh_a, sh_s = _seq_vec(sh, D, T, tm)
    sc_a, sc_s = _seq_vec(sc, D, T, tm)
    fix = lambda s: pl.BlockSpec(s.block_shape, functools.partial(lambda im, i: im(0, i), s.index_map))
    return pl.pallas_call(
        _modulate_interleave_kernel if interleave else _modulate_kernel,
        grid=(M // tm,),
        in_specs=[pl.BlockSpec((tm, D), lambda i: (i, 0)),
                  pl.BlockSpec((1, D), lambda i: (0, 0)),
                  fix(sh_s), fix(sc_s)],
        out_specs=pl.BlockSpec((tm, D), lambda i: (i, 0)),
        out_shape=jax.ShapeDtypeStruct((M, D), BF),
        compiler_params=_params(("arbitrary",)),
        name="modulate",
    )(x, g.reshape(1, D), sh_a, sc_a)


def _rmsnorm_kernel(x_ref, g_ref, o_ref):
    x = x_ref[...]
    o_ref[...] = x * lax.rsqrt(jnp.mean(x * x, axis=-1, keepdims=True) + EPS) * g_ref[...]


def rmsnorm(x, g, *, tm):
    M, D = x.shape
    return pl.pallas_call(
        _rmsnorm_kernel,
        grid=(M // tm,),
        in_specs=[pl.BlockSpec((tm, D), lambda i: (i, 0)), pl.BlockSpec((1, D), lambda i: (0, 0))],
        out_specs=pl.BlockSpec((tm, D), lambda i: (i, 0)),
        out_shape=jax.ShapeDtypeStruct((M, D), F32),
        compiler_params=_params(("arbitrary",)),
        name="final_norm",
    )(x, g.reshape(1, D))


def _conv_gate(ug, uu, pg1, pg2, pu1, pu2, wcg, wcu, bcg, bcu):
    yg = bcg + pg2 * wcg[0:1] + pg1 * wcg[1:2] + ug * wcg[2:3]
    yu = bcu + pu2 * wcu[0:1] + pu1 * wcu[1:2] + uu * wcu[2:3]
    return yg * _sigmoid(yg) * yu


SUBLANES = 8
PERM_GROUP = 512
PERM_Q = PERM_GROUP // SUBLANES


def _ffn_in_kernel(x_ref, wg_ref, wu_ref, wcg_ref, wcu_ref, bcg_ref, bcu_ref,
                   act_ref, cg_ref, cu_ref, wqg_ref, wqu_ref, wb0_ref, wb1_ref, car_ref, *, tps, n_sub):
    j = pl.program_id(0)
    i = pl.program_id(1)
    tm = x_ref.shape[0]
    tn = wg_ref.shape[1]
    G = PERM_GROUP
    assert tm == n_sub * G
    wb_refs = (wb0_ref, wb1_ref)

    def stage(slot):
        _stage_weight(i, wg_ref, wb_refs[slot], 0, wqg_ref)
        _stage_weight(i, wu_ref, wb_refs[slot], tn, wqu_ref)

    def step(slot):
        w = wb_refs[slot][...]
        wcg, wcu, bcg, bcu = wcg_ref[...], wcu_ref[...], bcg_ref[...], bcu_ref[...]
        car = car_ref[...]
        first = lax.broadcasted_iota(jnp.int32, (SUBLANES, 2 * tn), 0) == 0
        for r in range(n_sub):
            rows = slice(r * G, (r + 1) * G)
            u = jnp.dot(x_ref[rows, :], w, preferred_element_type=F32)
            fix1 = jnp.where(first, car[1:2], pltpu.roll(u[G - 8:G], 1, axis=0))
            fix2 = jnp.where(first, car[0:1], pltpu.roll(u[G - 16:G - 8], 1, axis=0))
            p1 = jnp.concatenate([fix1, u[:G - 8]], axis=0)
            p2 = jnp.concatenate([fix2, fix1, u[:G - 16]], axis=0)
            act_ref[rows, :] = _conv_gate(u[:, :tn], u[:, tn:], p1[:, :tn], p2[:, :tn], p1[:, tn:], p2[:, tn:],
                                          wcg, wcu, bcg, bcu).astype(act_ref.dtype)
            car = jnp.concatenate([u[G - 9:G - 8], u[G - 1:G]], axis=0)
        car_ref[...] = car
        cg_ref[...] = car[:, :tn]
        cu_ref[...] = car[:, tn:]

    def reset_history():
        @pl.when(i % tps == 0)
        def _():
            car_ref[...] = jnp.zeros_like(car_ref)

    _by_slot(j, stage, step, before=reset_history)


def ffn_in_prompt(x, w3, lead, w_conv, b_conv, *, B, T, tm, tn, n_sub):
    M, K = x.shape
    F = D_FF
    nj = F // tn
    ni = M // tm
    ck = K // ni
    assert ck * ni == K and ck % 16 == 0
    tps = T // tm
    wc = w_conv
    bc = b_conv.reshape(1, 2 * F)
    chunk = lambda j, i: _chunk(j, i, nj, ni)
    wq_spec = pl.BlockSpec((None, ck, tn), lambda j, i: (0, chunk(j, i)[1], chunk(j, i)[0]))
    act, cg, cu, wq_g, wq_u = pl.pallas_call(
        functools.partial(_ffn_in_kernel, tps=tps, n_sub=n_sub),
        grid=(nj + 1, ni),
        in_specs=[
            pl.BlockSpec((tm, K), lambda j, i: (_lag_row(j, i), 0)),
            pl.BlockSpec((None, ck, tn), lambda j, i: (lead, chunk(j, i)[1], chunk(j, i)[0])),
            pl.BlockSpec((None, ck, tn), lambda j, i: (lead, chunk(j, i)[1], chunk(j, i)[0] + nj)),
            pl.BlockSpec((3, tn), lambda j, i: (0, _lag_col(j))),
            pl.BlockSpec((3, tn), lambda j, i: (0, _lag_col(j) + nj)),
            pl.BlockSpec((1, tn), lambda j, i: (0, _lag_col(j))),
            pl.BlockSpec((1, tn), lambda j, i: (0, _lag_col(j) + nj)),
        ],
        out_specs=[
            pl.BlockSpec((tm, tn), lambda j, i: (_lag_row(j, i), _lag_col(j))),
            pl.BlockSpec((None, 2, tn), lambda j, i: (_lag_row(j, i) // tps, 0, _lag_col(j))),
            pl.BlockSpec((None, 2, tn), lambda j, i: (_lag_row(j, i) // tps, 0, _lag_col(j))),
            wq_spec, wq_spec,
        ],
        out_shape=[
            jax.ShapeDtypeStruct((M, F), BF),
            jax.ShapeDtypeStruct((B, 2, F), F32),
            jax.ShapeDtypeStruct((B, 2, F), F32),
            jax.ShapeDtypeStruct((1, K, F), BF),
            jax.ShapeDtypeStruct((1, K, F), BF),
        ],
        scratch_shapes=[pltpu.VMEM((K, 2 * tn), BF), pltpu.VMEM((K, 2 * tn), BF),
                        pltpu.VMEM((2, 2 * tn), F32)],
        compiler_params=_params(("arbitrary", "arbitrary")),
        name="ffn_in_p",
    )(x, w3, w3, wc, wc, bc, bc)
    return act, jnp.concatenate([cg, cu], axis=-1), (wq_g, wq_u)


def _ffn_in_step_kernel(x_ref, wg_ref, wu_ref, wcg_ref, wcu_ref, bcg_ref, bcu_ref,
                        c0g_ref, c0u_ref, c1g_ref, c1u_ref, act_ref, ug_ref, uu_ref):
    x = x_ref[...]
    ug = jnp.dot(x, wg_ref[...].astype(BF), preferred_element_type=F32)
    uu = jnp.dot(x, wu_ref[...].astype(BF), preferred_element_type=F32)
    act_ref[...] = _conv_gate(ug, uu, c1g_ref[...], c0g_ref[...], c1u_ref[...], c0u_ref[...],
                              wcg_ref[...], wcu_ref[...], bcg_ref[...], bcu_ref[...]).astype(act_ref.dtype)
    ug_ref[...] = ug
    uu_ref[...] = uu


def ffn_in_step(x, wg3, wu3, w_conv, b_conv, cache, *, tn):
    Bs, K = x.shape
    F = D_FF
    nj = F // tn
    bc = b_conv.reshape(1, 2 * F)
    cflat = cache.reshape(Bs, 4 * F)
    vec = lambda off: pl.BlockSpec((Bs, tn), lambda j: (0, j + off * nj))
    act, ug, uu = pl.pallas_call(
        _ffn_in_step_kernel,
        grid=(nj,),
        in_specs=[
            pl.BlockSpec((Bs, K), lambda j: (0, 0)),
            pl.BlockSpec((None, K, tn), lambda j: (0, 0, j)),
            pl.BlockSpec((None, K, tn), lambda j: (0, 0, j)),
            pl.BlockSpec((3, tn), lambda j: (0, j)),
            pl.BlockSpec((3, tn), lambda j: (0, j + nj)),
            pl.BlockSpec((1, tn), lambda j: (0, j)),
            pl.BlockSpec((1, tn), lambda j: (0, j + nj)),
            vec(0), vec(1), vec(2), vec(3),
        ],
        out_specs=[pl.BlockSpec((Bs, tn), lambda j: (0, j))] * 3,
        out_shape=[jax.ShapeDtypeStruct((Bs, F), BF),
                   jax.ShapeDtypeStruct((Bs, F), F32),
                   jax.ShapeDtypeStruct((Bs, F), F32)],
        compiler_params=_params(("arbitrary",)),
        name="ffn_in_s",
    )(x, wg3, wu3, w_conv, w_conv, bc, bc, cflat, cflat, cflat, cflat)
    new_cache = jnp.stack([cache[:, 1, :], jnp.concatenate([ug, uu], axis=-1)], axis=1)
    return act, new_cache


def _mlstm_chunk_kernel(q_ref, k_ref, v_ref, o_ref, g_ref, gt_ref, gh_ref,
                        hs_ref, C_ref, n_ref, m_ref, *, hb):
    hblk = pl.program_id(1)
    c = pl.program_id(2)
    L = q_ref.shape[0]

    @pl.when(c == 0)
    def _():
        C_ref[...] = jnp.zeros_like(C_ref)
        n_ref[...] = jnp.zeros_like(n_ref)
        m_ref[...] = jnp.zeros_like(m_ref)

    g = g_ref[...]
    gt = gt_ref[...]
    lane = lax.broadcasted_iota(jnp.int32, g.shape, 1)
    sub = lax.broadcasted_iota(jnp.int32, gt.shape, 0)
    t_idx = lax.broadcasted_iota(jnp.int32, (L, L), 0)
    s_idx = lax.broadcasted_iota(jnp.int32, (L, L), 1)
    causal = s_idx <= t_idx
    q_scale = jnp.asarray(M_QK ** -0.5, BF)

    for hh_i in range(hb):
        h = hblk * hb + hh_i
        ic_col = jnp.sum(jnp.where(lane == h, g, 0.0), axis=1, keepdims=True)
        fc_col = jnp.sum(jnp.where(lane == h + M_HEADS, g, 0.0), axis=1, keepdims=True)
        ic_row = jnp.sum(jnp.where(sub == h, gt, 0.0), axis=0, keepdims=True)
        fc_row = jnp.sum(jnp.where(sub == h + M_HEADS, gt, 0.0), axis=0, keepdims=True)
        b_col = jnp.sum(jnp.where(causal, fc_row, 0.0), axis=1, keepdims=True)
        b_row = jnp.sum(jnp.where(t_idx <= s_idx, fc_col, 0.0), axis=0, keepdims=True)

        m_old = m_ref[hh_i][0:1, 0:1]
        d = jnp.where(causal, b_col - b_row + ic_row, -jnp.inf)
        inter = b_col + m_old
        mt = jnp.maximum(inter, jnp.max(d, axis=1, keepdims=True))
        w = jnp.exp(d - mt)
        si = jnp.exp(inter - mt)

        qb = q_ref[:, hh_i * M_QK:(hh_i + 1) * M_QK] * q_scale
        kb = k_ref[:, hh_i * M_QK:(hh_i + 1) * M_QK]
        vb = v_ref[:, hh_i * M_V:(hh_i + 1) * M_V]
        C_old = C_ref[hh_i]
        n_old = n_ref[hh_i]
        qk = lax.dot_general(qb, kb, (((1,), (1,)), ((), ())), preferred_element_type=F32) * w
        num = (jnp.dot(qk.astype(BF), vb, preferred_element_type=F32)
               + si * jnp.dot(qb, C_old.astype(BF), preferred_element_type=F32))
        den = jnp.sum(qk, axis=1, keepdims=True) + si * jnp.sum(qb.astype(F32) * n_old, axis=1, keepdims=True)
        hh = num / jnp.maximum(jnp.abs(den), jnp.exp(-mt))

        b_last = b_col[L - 1:L, :]
        gl = b_last - b_col + ic_col
        m_new = jnp.maximum(b_last + m_old, jnp.max(gl, axis=0, keepdims=True))
        wl = jnp.exp(gl - m_new)
        sd = jnp.exp(b_last + m_old - m_new)
        kw = kb.astype(F32) * wl
        C_ref[hh_i] = sd * C_old + jnp.dot(kw.T.astype(BF), vb, preferred_element_type=F32)
        n_ref[hh_i] = sd * n_old + jnp.sum(kw, axis=0, keepdims=True)
        m_ref[hh_i] = jnp.broadcast_to(m_new, (1, LANES))

        hn = hh * lax.rsqrt(jnp.mean(hh * hh, axis=1, keepdims=True) + EPS)
        o = o_ref[:, hh_i * M_V:(hh_i + 1) * M_V].astype(F32)
        hs_ref[:, hh_i * M_V:(hh_i + 1) * M_V] = (
            hn * gh_ref[:, hh_i * M_V:(hh_i + 1) * M_V] * _sigmoid(o)).astype(hs_ref.dtype)


def mlstm_prompt(z, gates, g_head, *, B, T, L, hb):
    M = B * T
    nc = T // L
    H = M_HEADS
    gt = jnp.swapaxes(gates[:, :2 * H].reshape(B * nc, L, 2 * H), 1, 2)
    row = lambda b, h, c: b * nc + c
    nhb = H // hb
    kq = nhb
    kv = nhb
    ko = 2 * nhb
    hs, C, n, m = pl.pallas_call(
        functools.partial(_mlstm_chunk_kernel, hb=hb),
        grid=(B, nhb, nc),
        in_specs=[
            pl.BlockSpec((L, hb * M_QK), lambda b, h, c: (row(b, h, c), h)),
            pl.BlockSpec((L, hb * M_QK), lambda b, h, c: (row(b, h, c), kq + h)),
            pl.BlockSpec((L, hb * M_V), lambda b, h, c: (row(b, h, c), kv + h)),
            pl.BlockSpec((L, hb * M_V), lambda b, h, c: (row(b, h, c), ko + h)),
            pl.BlockSpec((L, LANES), lambda b, h, c: (row(b, h, c), 0)),
            pl.BlockSpec((None, 2 * H, L), lambda b, h, c: (row(b, h, c), 0, 0)),
            pl.BlockSpec((1, hb * M_V), lambda b, h, c: (0, h)),
        ],
        out_specs=[
            pl.BlockSpec((L, hb * M_V), lambda b, h, c: (row(b, h, c), h)),
            pl.BlockSpec((None, hb, M_QK, M_V), lambda b, h, c: (b, h, 0, 0)),
            pl.BlockSpec((None, hb, 1, M_QK), lambda b, h, c: (b, h, 0, 0)),
            pl.BlockSpec((None, hb, 1, LANES), lambda b, h, c: (b, h, 0, 0)),
        ],
        out_shape=[
            jax.ShapeDtypeStruct((M, D_MODEL), BF),
            jax.ShapeDtypeStruct((B, M_HEADS, M_QK, M_V), F32),
            jax.ShapeDtypeStruct((B, M_HEADS, 1, M_QK), F32),
            jax.ShapeDtypeStruct((B, M_HEADS, 1, LANES), F32),
        ],
        compiler_params=_params(("arbitrary", "arbitrary", "arbitrary")),
        name="mlstm_p",
    )(z, z, z, z, gates, gt, g_head.reshape(1, D_MODEL))
    return hs, C, n[:, :, 0, :], m[:, :, 0, 0]


def _mlstm_step_kernel(q_ref, k_ref, v_ref, o_ref, ig_ref, lf_ref, m0_ref, n0_ref, C0_ref, gh_ref,
                       hs_ref, C_ref, n_ref, m_ref):
    ic = ig_ref[...]
    fc = lf_ref[...]
    m_old = m0_ref[...]
    inter = fc + m_old
    mt = jnp.maximum(inter, ic)
    w = jnp.exp(ic - mt)
    si = jnp.exp(inter - mt)
    q = (q_ref[...] * (M_QK ** -0.5)).astype(BF).astype(F32)
    k = k_ref[...]
    v = v_ref[...]
    n_old = n0_ref[...]
    qk = jnp.sum(q * k.astype(BF).astype(F32), axis=1, keepdims=True) * w
    wl = jnp.exp(ic - mt)
    sd = jnp.exp(inter - mt)
    kw = k * wl
    qT = q.T
    kwT = kw.T
    rows = []
    for h in range(M_HEADS):
        C_old = C0_ref[h]
        rows.append(jnp.sum(qT[:, h:h + 1] * C_old, axis=0, keepdims=True))
        C_ref[h] = sd[h:h + 1, :] * C_old + kwT[:, h:h + 1] * v[h:h + 1, :]
    qC = jnp.concatenate(rows, axis=0)
    num = qk.astype(BF).astype(F32) * v.astype(BF).astype(F32) + si * qC
    den = qk + si * jnp.sum(q * n_old, axis=1, keepdims=True)
    hh = num / jnp.maximum(jnp.abs(den), jnp.exp(-mt))
    n_ref[...] = sd * n_old + kw
    m_ref[...] = mt
    hn = hh * lax.rsqrt(jnp.mean(hh * hh, axis=1, keepdims=True) + EPS)
    hs_ref[...] = (hn * gh_ref[...] * _sigmoid(o_ref[...])).astype(hs_ref.dtype)


def mlstm_step(z, gates, g_head, C0, n0, m0):
    Bs = z.shape[0]
    H = M_HEADS
    qd, vd = H * M_QK, H * M_V
    q = z[:, :qd].reshape(Bs, H, M_QK)
    k = z[:, qd:2 * qd].reshape(Bs, H, M_QK)
    v = z[:, 2 * qd:2 * qd + vd].reshape(Bs, H, M_V)
    o = z[:, 2 * qd + vd:].reshape(Bs, H, M_V)
    ig = gates[:, :H].reshape(Bs, H, 1)
    lf = gates[:, H:2 * H].reshape(Bs, H, 1)
    per_b = lambda *tail: pl.BlockSpec((None,) + tail, lambda b: (b,) + (0,) * len(tail))
    hs, C, n, m = pl.pallas_call(
        _mlstm_step_kernel,
        grid=(Bs,),
        in_specs=[
            per_b(H, M_QK), per_b(H, M_QK), per_b(H, M_V), per_b(H, M_V),
            per_b(H, 1), per_b(H, 1), per_b(H, 1),
            pl.BlockSpec((None, None, H, M_QK), lambda b: (0, b, 0, 0)),
            pl.BlockSpec((None, None, H, M_QK, M_V), lambda b: (0, b, 0, 0, 0)),
            pl.BlockSpec((H, M_V), lambda b: (0, 0)),
        ],
        out_specs=[
            per_b(H, M_V),
            pl.BlockSpec((None, None, H, M_QK, M_V), lambda b: (0, b, 0, 0, 0)),
            pl.BlockSpec((None, None, H, M_QK), lambda b: (0, b, 0, 0)),
            per_b(H, 1),
        ],
        out_shape=[
            jax.ShapeDtypeStruct((Bs, H, M_V), BF),
            jax.ShapeDtypeStruct(C0.shape, F32),
            jax.ShapeDtypeStruct(n0.shape, F32),
            jax.ShapeDtypeStruct((Bs, H, 1), F32),
        ],
        compiler_params=_params(("arbitrary",)),
        name="mlstm_s",
    )(q, k, v, o, ig, lf, m0.reshape(Bs, H, 1), n0, C0, g_head.reshape(H, M_V))
    return hs.reshape(Bs, D_MODEL), C, n, m.reshape(1, Bs, H)


def _attn_kernel(sink_ref, q_ref, kc_ref, kp_ref, vc_ref, vp_ref, o_ref):
    n = pl.program_id(1)
    W = WINDOW
    R = GROUP * W
    kp = kp_ref[...].astype(BF)
    kc = kc_ref[...].astype(BF)
    vpT = vp_ref[...].T.astype(BF)
    vcT = vc_ref[...].T.astype(BF)
    j_idx = lax.broadcasted_iota(jnp.int32, (W, R), 0)
    i_idx = lax.broadcasted_iota(jnp.int32, (W, R), 1) % W
    from_prev = j_idx > i_idx
    lane_g = lax.broadcasted_iota(jnp.int32, (1, R), 1) // W
    no_prev = jnp.where(n > 0, 0.0, -jnp.inf)
    nt = (((1,), (1,)), ((), ()))
    heads = lambda h: slice(h * HEAD_DIM, (h + 1) * HEAD_DIM)
    scores = []
    for h in range(KV_HEADS):
        qh = jnp.concatenate(
            [q_ref[:, (h * GROUP + g) * HEAD_DIM:(h * GROUP + g + 1) * HEAD_DIM] for g in range(GROUP)], axis=0)
        sp = lax.dot_general(kp[:, heads(h)], qh, nt, preferred_element_type=F32)
        sc = lax.dot_general(kc[:, heads(h)], qh, nt, preferred_element_type=F32)
        scores.append(jnp.where(from_prev, sp + no_prev, sc))
    probs = []
    for h, s in enumerate(scores):
        sk = jnp.zeros((1, R), F32)
        for g in range(GROUP):
            sk = jnp.where(lane_g == g, sink_ref[h * GROUP + g], sk)
        mx = jnp.maximum(jnp.max(s, axis=0, keepdims=True), sk)
        p = jnp.exp(s - mx)
        p = p / (jnp.sum(p, axis=0, keepdims=True) + jnp.exp(sk - mx))
        probs.append((jnp.where(from_prev, p, 0.0).astype(BF), jnp.where(from_prev, 0.0, p).astype(BF)))
    for h, (pp, pc) in enumerate(probs):
        oT = (jnp.dot(vpT[heads(h), :], pp, preferred_element_type=F32)
              + jnp.dot(vcT[heads(h), :], pc, preferred_element_type=F32))
        for g in range(GROUP):
            c0 = (h * GROUP + g) * HEAD_DIM
            o_ref[:, c0:c0 + HEAD_DIM] = oT[:, g * W:(g + 1) * W].T.astype(o_ref.dtype)


def attention_prompt(q, k, v, sinks, *, B, T):
    M = B * T
    W = WINDOW
    nb = T // W
    kvd = KV_HEADS * HEAD_DIM
    cur = lambda b, n: (b * nb + n, 0)
    prev = lambda b, n: (b * nb + jnp.maximum(n - 1, 0), 0)
    return pl.pallas_call(
        _attn_kernel,
        grid=(B, nb),
        in_specs=[
            pl.BlockSpec(memory_space=pltpu.SMEM),
            pl.BlockSpec((W, D_MODEL), cur),
            pl.BlockSpec((W, kvd), cur), pl.BlockSpec((W, kvd), prev),
            pl.BlockSpec((W, kvd), cur), pl.BlockSpec((W, kvd), prev),
        ],
        out_specs=pl.BlockSpec((W, D_MODEL), cur),
        out_shape=jax.ShapeDtypeStruct((M, D_MODEL), BF),
        compiler_params=_params(("arbitrary", "arbitrary")),
        name="attn_p",
    )(sinks, q, k, k, v, v)


def _per_head_rows(row):
    return jnp.concatenate(
        [jnp.broadcast_to(row[:, h * HEAD_DIM:(h + 1) * HEAD_DIM], (GROUP, HEAD_DIM)) for h in range(KV_HEADS)],
        axis=0)


def _attn_step_kernel(q_ref, kc_ref, vc_ref, kn_ref, vn_ref, sk_ref, o_ref, ko_ref, vo_ref, *, bb):
    W = WINDOW
    jl = lax.broadcasted_iota(jnp.int32, (A_HEADS, W), 1)
    sk = sk_ref[...]
    heads = lambda a, h: a[h * GROUP:(h + 1) * GROUP, :]
    scores = []
    for b in range(bb):
        q = q_ref[b]
        kcb = kc_ref[b].astype(BF)
        s = jnp.concatenate(
            [lax.dot_general(heads(q, h).astype(BF), kcb[:, h * HEAD_DIM:(h + 1) * HEAD_DIM],
                             (((1,), (1,)), ((), ())), preferred_element_type=F32) for h in range(KV_HEADS)],
            axis=0)
        kne = _per_head_rows(kn_ref[b]).astype(BF).astype(F32)
        sn = jnp.sum(q.astype(BF).astype(F32) * kne, axis=1, keepdims=True)
        scores.append((s, sn))
    probs = []
    for s, sn in scores:
        s = jnp.where(jl >= 1, s, -jnp.inf)
        mx = jnp.maximum(jnp.maximum(jnp.max(s, axis=1, keepdims=True), sn), sk)
        p = jnp.exp(s - mx)
        pn = jnp.exp(sn - mx)
        den = jnp.sum(p, axis=1, keepdims=True) + pn + jnp.exp(sk - mx)
        probs.append((p / den, pn / den))
    for b in range(bb):
        p, pn = probs[b]
        vcb = vc_ref[b].astype(BF)
        o = jnp.concatenate(
            [jnp.dot(heads(p, h).astype(BF), vcb[:, h * HEAD_DIM:(h + 1) * HEAD_DIM],
                     preferred_element_type=F32) for h in range(KV_HEADS)], axis=0)
        vne = _per_head_rows(vn_ref[b]).astype(BF).astype(F32)
        o_ref[b] = o + pn.astype(BF).astype(F32) * vne
        ko_ref[b, 0:W - 1, :] = kc_ref[b, 1:W, :]
        ko_ref[b, W - 1:W, :] = kn_ref[b]
        vo_ref[b, 0:W - 1, :] = vc_ref[b, 1:W, :]
        vo_ref[b, W - 1:W, :] = vn_ref[b]


def attention_step(q, k_new, v_new, kbuf, vbuf, sinks):
    Bs = q.shape[0]
    W = WINDOW
    kvd = KV_HEADS * HEAD_DIM
    bb = 4
    assert Bs % bb == 0
    per_b = lambda *tail: pl.BlockSpec((bb,) + tail, lambda b: (b,) + (0,) * len(tail))
    o, ko, vo = pl.pallas_call(
        functools.partial(_attn_step_kernel, bb=bb),
        grid=(Bs // bb,),
        in_specs=[per_b(A_HEADS, HEAD_DIM), per_b(W, kvd), per_b(W, kvd), per_b(1, kvd), per_b(1, kvd),
                  pl.BlockSpec((A_HEADS, 1), lambda b: (0, 0))],
        out_specs=[per_b(A_HEADS, HEAD_DIM), per_b(W, kvd), per_b(W, kvd)],
        out_shape=[jax.ShapeDtypeStruct((Bs, A_HEADS, HEAD_DIM), F32),
                   jax.ShapeDtypeStruct((Bs, W, kvd), F32),
                   jax.ShapeDtypeStruct((Bs, W, kvd), F32)],
        compiler_params=_params(("arbitrary",)),
        name="attn_s",
    )(q.reshape(Bs, A_HEADS, HEAD_DIM), kbuf.reshape(Bs, W, kvd), vbuf.reshape(Bs, W, kvd),
      k_new.reshape(Bs, 1, kvd), v_new.reshape(Bs, 1, kvd), sinks.reshape(A_HEADS, 1))
    return (o.reshape(Bs, D_MODEL), ko.reshape(Bs, W, KV_HEADS, HEAD_DIM), vo.reshape(Bs, W, KV_HEADS, HEAD_DIM))


def _rope_tables(pos, width):
    half = HEAD_DIM // 2
    freq = ROPE_THETA ** (-jnp.arange(half, dtype=F32) / half)
    ang = pos.astype(F32)[:, None] * freq[None, :]
    cos = jnp.cos(ang)
    sin = jnp.sin(ang)
    reps = width // HEAD_DIM
    return (jnp.tile(jnp.concatenate([cos, cos], axis=1), (1, reps)),
            jnp.tile(jnp.concatenate([-sin, sin], axis=1), (1, reps)))


def _tiles(T):
    if T > 1:
        return dict(tm=1024, tn=1024, n_sub=2,
                    tm_r=512, tn_r=1024, n_sub_r=1,
                    tn_kv=512, tm_ffi=2048, tn_ffi=256, n_sub_ffi=4,
                    tm_ffo=512, tn_ffo=512, n_sub_ffo=1, tm_mod=512)
    return dict(tm=None, tn=1024, n_sub=1, tm_r=None, tn_r=1024, n_sub_r=1, tn_kv=512,
                tm_ffi=None, tn_ffi=256, n_sub_ffi=1, tm_ffo=None, tn_ffo=512, n_sub_ffo=1, tm_mod=None)


def _run(x, ada, pos, state, P, *, B, T, wq_in=None):
    D = D_MODEL
    M = B * T
    c = {k: (M if v is None else v) for k, v in _tiles(T).items()}
    tm, tn, tm_r, tn_r, tm_mod = c["tm"], c["tn"], c["tm_r"], c["tn_r"], c["tm_mod"]
    cos, sin = _rope_tables(pos, LANES)
    if T > 1:
        tps = T // tm
        rope_spec = pl.BlockSpec((tm, LANES), lambda j, i: (i % tps, 0))
    else:
        rope_spec = pl.BlockSpec((1, LANES), lambda j, i: (0, 0))

    out = {}
    tag = "_p" if T > 1 else "_s"
    wq = {}

    def lin(key, x_, w3, lead, **kw):
        if wq_in is None:
            res, wq[key] = linear(x_, w3, lead, keep_bf16=True, **kw)
            return res
        kw.pop("col_off", None)
        return linear(x_, wq_in[key], 0, **kw)

    sh1, sc1, ga1, sh2, sc2, ga2 = ada[0]
    hn = modulate(x, P["g_norm1"][0], sh1, sc1, T=T, tm=tm_mod)
    qd, vd = M_HEADS * M_QK, M_HEADS * M_V
    w_inT = jnp.swapaxes(P["w_m_in"], 1, 2)
    z = lin("m_in", hn, w_inT, 0, name="m_in" + tag, n_cols=2 * qd + 2 * vd, tm=tm, tn=tn, n_sub=c["n_sub"],
               epilogue=_ep_plain, transposed=True, out_dtype=BF if T > 1 else F32)
    wgT = jnp.pad(w_inT[:, 2 * qd + 2 * vd:, :], ((0, 0), (0, LANES - 2 * M_HEADS), (0, 0)))
    bg = jnp.pad(jnp.concatenate([P["b_m_i"][0], P["b_m_f"][0]]), (0, LANES - 2 * M_HEADS))
    gates = linear(hn, wgT, 0, name="m_gates" + tag, n_cols=LANES, tm=tm, tn=LANES, epilogue=_ep_gates,
                   extras=[_row_vec(bg, LANES)], transposed=True)
    if T > 1:
        hs, C, n, m = mlstm_prompt(z, gates, P["g_m_head"][0], B=B, T=T, L=256, hb=8)
        out["C"], out["n"], out["m"] = C[None], n[None], m[None]
    else:
        hs, out["C"], out["n"], out["m"] = mlstm_step(z, gates, P["g_m_head"][0], *state["mlstm"])
    x = lin("m_out", hs, P["w_m_out"], 0, name="m_out" + tag, n_cols=D, tm=tm_r, tn=tn_r, n_sub=c["n_sub_r"],
               epilogue=_ep_resid, extras=[_tile(x, tn_r, T, tm_r), _seq_vec(ga1, tn_r, T, tm_r)])

    convs = []
    for l in range(2):
        if l == 1:
            sh1, sc1, ga1, sh2, sc2, ga2 = ada[1]
            shk, sck = ada[2]
            hk, hn = modulate_pair(x, (P["g_kv"], shk, sck), (P["g_norm1"][1], sh1, sc1), T=T, tm=tm_mod)
            kvd = KV_HEADS * HEAD_DIM
            tn_kv = c["tn_kv"]
            k = lin("kv_k", hk, P["w_kv"], 0, name="kv_k" + tag, n_cols=kvd, tm=tm, tn=tn_kv, n_sub=c["n_sub"],
                       epilogue=functools.partial(_ep_rope, scale=1.0),
                       extras=[_row_vec(P["b_kv"][:kvd], tn_kv), (cos, rope_spec), (sin, rope_spec)])
            v = lin("kv_v", hk, P["w_kv"], 0, name="kv_v" + tag, n_cols=kvd, tm=tm, tn=tn_kv, n_sub=c["n_sub"],
                       col_off=kvd // tn_kv, epilogue=_ep_bias, extras=[_row_vec(P["b_kv"][kvd:], tn_kv)])
            q = lin("attn_q", hn, P["w_q"], 0, name="attn_q" + tag, n_cols=D, tm=tm, tn=tn, n_sub=c["n_sub"],
                       epilogue=functools.partial(_ep_rope, scale=HEAD_DIM ** -0.5),
                       extras=[_row_vec(P["b_q"][0], tn), (cos, rope_spec), (sin, rope_spec)],
                       out_dtype=BF if T > 1 else F32)
            if T > 1:
                o = attention_prompt(q, k, v, P["sinks"][0], B=B, T=T)
                last = lambda a: a.reshape(B, T, kvd)[:, T - WINDOW:, :].reshape(B, WINDOW, KV_HEADS, HEAD_DIM)
                out["k_win"], out["v_win"] = last(k), last(v)
            else:
                o, out["k_win"], out["v_win"] = attention_step(q, k, v, state["kbuf"], state["vbuf"], P["sinks"][0])
            x = lin("attn_o", o, P["w_o"], 0, name="attn_o" + tag, n_cols=D, tm=tm_r, tn=tn_r, n_sub=c["n_sub_r"],
                       epilogue=_ep_resid_bias,
                       extras=[_row_vec(P["b_o"][0], tn_r), _tile(x, tn_r, T, tm_r), _seq_vec(ga1, tn_r, T, tm_r)])

        il = T > 1
        hn = modulate(x, P["g_norm2"][l], sh2, sc2, T=T, tm=PERM_GROUP if il else tm_mod, interleave=il)
        if T > 1:
            act, cb, wq["ffn_in%d" % l] = ffn_in_prompt(hn, P["w_ffn_in"], l, P["w_conv"][l], P["b_conv"][l], B=B, T=T,
                                                        tm=c["tm_ffi"], tn=c["tn_ffi"], n_sub=c["n_sub_ffi"])
        else:
            act, cb = ffn_in_step(hn, *wq_in["ffn_in%d" % l], P["w_conv"][l], P["b_conv"][l], state["conv"][l],
                                  tn=c["tn_ffi"])
        convs.append(cb)
        tm_o, tn_o = c["tm_ffo"], c["tn_ffo"]
        x = lin("ffn_out%d" % l, act, P["w_ffn_out"], l, name="ffn_out" + tag, n_cols=D, tm=tm_o, tn=tn_o, n_sub=c["n_sub_ffo"],
                   epilogue=_ep_resid, extras=[_tile(x, tn_o, T, tm_o), _seq_vec(ga2, tn_o, T, tm_o)],
                   rows_interleaved=il)

    out["conv"] = jnp.stack(convs)
    out["y"] = rmsnorm(x, P["g_final"], tm=tm_mod)
    out["wq"] = wq
    return out


def kernel(x_prompt, x_sample, state_mlstm_C, state_mlstm_n, state_mlstm_m, cache_conv, cache_k_win, cache_v_win, c_prompt, c_sample, w_ada, g_norm1, g_norm2, w_m_in, b_m_i, b_m_f, g_m_head, w_m_out, w_ada_kv, g_kv, w_kv, b_kv, w_q, b_q, sinks, w_o, b_o, w_ffn_in, w_conv, b_conv, w_ffn_out, g_final):
    D = D_MODEL
    Bp, Tp, _ = x_prompt.shape
    Bs, Ts, _ = x_sample.shape
    assert Ts == 1
    P = dict(g_norm1=g_norm1, g_norm2=g_norm2, w_m_in=w_m_in, b_m_i=b_m_i, b_m_f=b_m_f, g_m_head=g_m_head,
             w_m_out=w_m_out, g_kv=g_kv, w_kv=w_kv[None], b_kv=b_kv, w_q=w_q, b_q=b_q, sinks=sinks, w_o=w_o,
             b_o=b_o, w_ffn_in=w_ffn_in, w_conv=w_conv, b_conv=b_conv, w_ffn_out=w_ffn_out, g_final=g_final)

    n_c = Bp + Bs
    pad_c = -n_c % 16
    cs = silu_cast(jnp.concatenate([c_prompt, c_sample, jnp.zeros((pad_c, D), F32)], axis=0))
    rows_c = n_c + pad_c
    ada_all = []
    for l in range(2):
        a = linear(cs, w_ada, l, name="ada", n_cols=6 * D, tm=rows_c, tn=512, epilogue=_ep_plain)
        ada_all.append(a)
    a_kv = linear(cs, w_ada_kv[None], 0, name="ada_kv", n_cols=2 * D, tm=rows_c, tn=512, epilogue=_ep_plain)

    def split(lo, hi):
        per_layer = [tuple(a[lo:hi, i * D:(i + 1) * D] for i in range(6)) for a in ada_all]
        return per_layer + [(a_kv[lo:hi, :D], a_kv[lo:hi, D:])]

    po = _run(x_prompt.reshape(Bp * Tp, D), split(0, Bp), jnp.arange(Tp, dtype=jnp.int32), None, P, B=Bp, T=Tp)
    state = dict(mlstm=(state_mlstm_C, state_mlstm_n, state_mlstm_m), conv=cache_conv,
                 kbuf=cache_k_win, vbuf=cache_v_win)
    so = _run(x_sample.reshape(Bs, D), split(Bp, Bp + Bs), PAST_LEN + jnp.arange(1, dtype=jnp.int32), state, P,
              B=Bs, T=1, wq_in=po["wq"])
    return (po["y"].reshape(Bp, Tp, D), so["y"].reshape(Bs, 1, D),
            po["C"], po["n"], po["m"], po["conv"], po["k_win"], po["v_win"],
            so["C"], so["n"], so["m"], so["conv"], so["k_win"], so["v_win"])
```
